```python
import math
import jax, jax.numpy as jnp
from jax import lax
import numpy as np

D_MODEL = 1024
BATCH = 16
SEQ = 2048
DEPTH = 4

N_MIXERS = 4
D_FF = 2816
BLOCK = 128
LN_EPS = 1e-5
RMS_EPS = 1e-6
NEG = -1e30

SWA_HEADS = 16
SWA_KV_HEADS = 4
SWA_HEAD_DIM = 64
SWA_WINDOW = 128

MLA_HEADS = 16
MLA_Q_LORA = 768
MLA_KV_LORA = 256
MLA_NOPE = 64
MLA_ROPE = 32
MLA_V = 64
ROPE_THETA = 10000.0

DSA_HEADS = 16
DSA_KV_HEADS = 4
DSA_HEAD_DIM = 64
IDX_HEADS = 8
IDX_DIM = 64
IDX_TOPK = 256

SB_HEADS = 16
SB_HEAD_DIM = 64

ALPHA = (2.0 * DEPTH) ** 0.25
BETA = (8.0 * DEPTH) ** -0.25

kernel_name = "interleaved_hybrid_swa_mla_dsa_stickbreak_macaron"


def layer_norm(x, g, b):
    xf = x.astype(jnp.float32)
    mu = jnp.mean(xf, axis=-1, keepdims=True)
    var = jnp.mean(jnp.square(xf - mu), axis=-1, keepdims=True)
    return ((xf - mu) * lax.rsqrt(var + LN_EPS) * g.astype(jnp.float32) + b.astype(jnp.float32)).astype(x.dtype)


def rms_norm(x, g):
    xf = x.astype(jnp.float32)
    ms = jnp.mean(jnp.square(xf), axis=-1, keepdims=True)
    return (xf * lax.rsqrt(ms + RMS_EPS) * g.astype(jnp.float32)).astype(x.dtype)


def swiglu(x, w_gu, w_down):
    gate, up = jnp.split(x @ w_gu, 2, axis=-1)
    return (jax.nn.silu(gate) * up) @ w_down


def alibi_slopes(n_heads):
    return np.array([2.0 ** (-8.0 * (i + 1) / n_heads) for i in range(n_heads)], dtype=np.float32)


def rope(x, pos):
    d = x.shape[-1]
    inv = ROPE_THETA ** (-jnp.arange(0, d, 2, dtype=jnp.float32) / d)
    ang = pos.astype(jnp.float32)[:, None] * inv[None, :]
    cos = jnp.cos(ang)[None, :, None, :].astype(x.dtype)
    sin = jnp.sin(ang)[None, :, None, :].astype(x.dtype)
    x1, x2 = jnp.split(x, 2, axis=-1)
    return jnp.concatenate([x1 * cos - x2 * sin, x1 * sin + x2 * cos], axis=-1)


def to_blocks(a):
    b, s = a.shape[0], a.shape[1]
    return jnp.swapaxes(a.reshape(b, s // BLOCK, BLOCK, *a.shape[2:]), 0, 1)


def from_blocks(a):
    a = jnp.swapaxes(a, 0, 1)
    return a.reshape(a.shape[0], a.shape[1] * a.shape[2], *a.shape[3:])


def swa_sink_attention(x, w_in, sinks, w_o):
    B, S, _ = x.shape
    H, KV, dh = SWA_HEADS, SWA_KV_HEADS, SWA_HEAD_DIM
    G = H // KV
    nb = S // BLOCK
    q, k, v = jnp.split(x @ w_in, [H * dh, H * dh + KV * dh], axis=-1)
    q = q.reshape(B, nb, BLOCK, KV, G, dh)
    k = k.reshape(B, nb, BLOCK, KV, dh)
    v = v.reshape(B, nb, BLOCK, KV, dh)
    k_prev = jnp.concatenate([jnp.zeros_like(k[:, :1]), k[:, :-1]], axis=1)
    v_prev = jnp.concatenate([jnp.zeros_like(v[:, :1]), v[:, :-1]], axis=1)
    kb = jnp.concatenate([k_prev, k], axis=2)
    vb = jnp.concatenate([v_prev, v], axis=2)
    s = jnp.einsum('bnqgrd,bnkgd->bngrqk', q, kb).astype(jnp.float32) * (dh ** -0.5)
    qi = jnp.arange(BLOCK)[:, None]
    kj = jnp.arange(2 * BLOCK)[None, :]
    dist = qi + BLOCK - kj
    band = (dist >= 0) & (dist < SWA_WINDOW)
    valid = band[None] & ((jnp.arange(nb)[:, None, None] > 0) | (kj[None] >= BLOCK))
    slopes = alibi_slopes(H).reshape(KV, G)
    s = s - slopes[:, :, None, None] * dist.astype(jnp.float32)
    s = jnp.where(valid[None, :, None, None], s, NEG)
    sink = jnp.broadcast_to(sinks.astype(jnp.float32).reshape(KV, G)[None, None, :, :, None, None],
                            s.shape[:-1] + (1,))
    p = jax.nn.softmax(jnp.concatenate([s, sink], axis=-1), axis=-1)[..., :-1]
    o = jnp.einsum('bngrqk,bnkgd->bnqgrd', p.astype(vb.dtype), vb)
    return o.reshape(B, S, H * dh) @ w_o


def mla_attention(x, w_in, q_norm_g, w_uq, kv_norm_g, w_ukv, w_o):
    B, S, _ = x.shape
    H = MLA_HEADS
    c_q, c_kv, k_rope = jnp.split(x @ w_in, [MLA_Q_LORA, MLA_Q_LORA + MLA_KV_LORA], axis=-1)
    c_q = rms_norm(c_q, q_norm_g)
    c_kv = rms_norm(c_kv, kv_norm_g)
    pos = jnp.arange(S)
    q = (c_q @ w_uq).reshape(B, S, H, MLA_NOPE + MLA_ROPE)
    q_nope, q_rope = jnp.split(q, [MLA_NOPE], axis=-1)
    q_rope = rope(q_rope, pos)
    k_rope = rope(k_rope[:, :, None, :], pos)[:, :, 0, :]
    kv = (c_kv @ w_ukv).reshape(B, S, H, MLA_NOPE + MLA_V)
    k_nope, v = jnp.split(kv, [MLA_NOPE], axis=-1)
    scale = (MLA_NOPE + MLA_ROPE) ** -0.5
    nb = S // BLOCK
    key_pos = jnp.arange(S)

    def block_fn(args):
        qn, qr, bi = args
        s = (jnp.einsum('bqhd,bkhd->bhqk', qn, k_nope)
             + jnp.einsum('bqhd,bkd->bhqk', qr, k_rope)).astype(jnp.float32) * scale
        tq = bi * BLOCK + jnp.arange(BLOCK)
        s = jnp.where(tq[:, None] >= key_pos[None, :], s, NEG)
        p = jax.nn.softmax(s, axis=-1).astype(v.dtype)
        return jnp.einsum('bhqk,bkhd->bqhd', p, v)

    o = lax.map(block_fn, (to_blocks(q_nope), to_blocks(q_rope), jnp.arange(nb)))
    return from_blocks(o).reshape(B, S, H * MLA_V) @ w_o


def dsa_attention(x, w_in, w_o):
    B, S, _ = x.shape
    H, KV, dh = DSA_HEADS, DSA_KV_HEADS, DSA_HEAD_DIM
    G = H // KV
    topk = min(IDX_TOPK, S // 4)
    sizes = [H * dh, KV * dh, KV * dh, IDX_HEADS * IDX_DIM, IDX_DIM]
    cuts = np.cumsum(sizes).tolist()
    q, k, v, q_idx, k_idx, w_idx = jnp.split(x @ w_in, cuts, axis=-1)
    q = q.reshape(B, S, KV, G, dh)
    k = k.reshape(B, S, KV, dh)
    v = v.reshape(B, S, KV, dh)
    q_idx = q_idx.reshape(B, S, IDX_HEADS, IDX_DIM)
    w_idx = w_idx * (IDX_HEADS ** -0.5)
    slopes = alibi_slopes(H).reshape(KV, G)
    key_pos = jnp.arange(S)
    nb = S // BLOCK
    gather = jax.vmap(lambda kk, ii: kk[ii])

    def block_fn(args):
        qb, qib, wib, bi = args
        tq = bi * BLOCK + jnp.arange(BLOCK)
        dots = jnp.einsum('bqhd,bkd->bqhk', qib, k_idx).astype(jnp.float32) * (IDX_DIM ** -0.5)
        score = jnp.einsum('bqh,bqhk->bqk', wib.astype(jnp.float32), jax.nn.relu(dots))
        score = jnp.where(tq[None, :, None] >= key_pos[None, None, :], score, NEG)
        _, idx = lax.top_k(score, topk)
        sel_ok = idx <= tq[None, :, None]
        k_sel = gather(k, idx)
        v_sel = gather(v, idx)
        s = jnp.einsum('bqgrd,bqkgd->bgrqk', qb, k_sel).astype(jnp.float32) * (dh ** -0.5)
        dist = (tq[None, :, None] - idx).astype(jnp.float32)
        s = s - slopes[None, :, :, None, None] * dist[:, None, None]
        s = jnp.where(sel_ok[:, None, None], s, NEG)
        p = jax.nn.softmax(s, axis=-1).astype(v_sel.dtype)
        return jnp.einsum('bgrqk,bqkgd->bqgrd', p, v_sel)

    o = lax.map(block_fn, (to_blocks(q), to_blocks(q_idx), to_blocks(w_idx), jnp.arange(nb)))
    return from_blocks(o).reshape(B, S, H * dh) @ w_o


def stick_breaking_attention(x, w_in, w_o):
    B, S, _ = x.shape
    H, dh = SB_HEADS, SB_HEAD_DIM
    q, k, v = jnp.split(x @ w_in, 3, axis=-1)
    q = q.reshape(B, S, H, dh)
    k = k.reshape(B, S, H, dh)
    v = v.reshape(B, S, H, dh)
    key_pos = jnp.arange(S)
    nb = S // BLOCK

    def block_fn(args):
        qb, bi = args
        tq = bi * BLOCK + jnp.arange(BLOCK)
        z = jnp.einsum('bqhd,bkhd->bhqk', qb, k).astype(jnp.float32) * (dh ** -0.5)
        strict = key_pos[None, :] < tq[:, None]
        log_1m = jnp.where(strict, jax.nn.log_sigmoid(-z), 0.0)
        between = lax.cumsum(log_1m, axis=3, reverse=True) - log_1m
        a = jnp.where(strict, jnp.exp(jax.nn.log_sigmoid(z) + between), 0.0)
        return jnp.einsum('bhqk,bkhd->bqhd', a.astype(v.dtype), v)

    o = lax.map(block_fn, (to_blocks(q), jnp.arange(nb)))
    return from_blocks(o).reshape(B, S, H * dh) @ w_o


def _n_layers_of(m):
    return (DEPTH - m + N_MIXERS - 1) // N_MIXERS


def setup_inputs(seed: int = 0) -> dict:
    key = jax.random.key(seed)
    ks = iter(jax.random.split(key, 32))

    def dense(shape, fan_in, scale=1.0):
        return jax.random.normal(next(ks), shape, jnp.float32) * (scale * fan_in ** -0.5)

    def gain(shape):
        return 1.0 + 0.02 * jax.random.normal(next(ks), shape, jnp.float32)

    nA, nB, nC, nD = (_n_layers_of(m) for m in range(N_MIXERS))
    D = D_MODEL
    a_in = (SWA_HEADS + 2 * SWA_KV_HEADS) * SWA_HEAD_DIM
    b_in = MLA_Q_LORA + MLA_KV_LORA + MLA_ROPE
    c_in = (DSA_HEADS + 2 * DSA_KV_HEADS) * DSA_HEAD_DIM + IDX_HEADS * IDX_DIM + IDX_DIM + IDX_HEADS
    d_in = 3 * SB_HEADS * SB_HEAD_DIM
    return {
        "x": jax.random.normal(next(ks), (BATCH, SEQ, D), jnp.float32),
        "ffn1_w_gu": dense((DEPTH, D, 2 * D_FF), D),
        "ffn1_w_down": dense((DEPTH, D_FF, D), D_FF, BETA),
        "ffn2_w_gu": dense((DEPTH, D, 2 * D_FF), D),
        "ffn2_w_down": dense((DEPTH, D_FF, D), D_FF, BETA),
        "ln_g": gain((DEPTH, 3, D)),
        "ln_b": 0.02 * jax.random.normal(next(ks), (DEPTH, 3, D), jnp.float32),
        "a_w_in": dense((nA, D, a_in), D),
        "a_sinks": jax.random.normal(next(ks), (nA, SWA_HEADS), jnp.float32),
        "a_w_o": dense((nA, SWA_HEADS * SWA_HEAD_DIM, D), SWA_HEADS * SWA_HEAD_DIM, BETA),
        "b_w_in": dense((nB, D, b_in), D),
        "b_q_norm": gain((nB, MLA_Q_LORA)),
        "b_w_uq": dense((nB, MLA_Q_LORA, MLA_HEADS * (MLA_NOPE + MLA_ROPE)), MLA_Q_LORA),
        "b_kv_norm": gain((nB, MLA_KV_LORA)),
        "b_w_ukv": dense((nB, MLA_KV_LORA, MLA_HEADS * (MLA_NOPE + MLA_V)), MLA_KV_LORA),
        "b_w_o": dense((nB, MLA_HEADS * MLA_V, D), MLA_HEADS * MLA_V, BETA),
        "c_w_in": dense((nC, D, c_in), D),
        "c_w_o": dense((nC, DSA_HEADS * DSA_HEAD_DIM, D), DSA_HEADS * DSA_HEAD_DIM, BETA),
        "d_w_in": dense((nD, D, d_in), D),
        "d_w_o": dense((nD, SB_HEADS * SB_HEAD_DIM, D), SB_HEADS * SB_HEAD_DIM, BETA),
    }


def reference(x, ffn1_w_gu, ffn1_w_down, ffn2_w_gu, ffn2_w_down, ln_g, ln_b,
              a_w_in, a_sinks, a_w_o,
              b_w_in, b_q_norm, b_w_uq, b_kv_norm, b_w_ukv, b_w_o,
              c_w_in, c_w_o, d_w_in, d_w_o):
    for i in range(DEPTH):
        m, j = i % N_MIXERS, i // N_MIXERS
        x = layer_norm(ALPHA * x + 0.5 * swiglu(x, ffn1_w_gu[i], ffn1_w_down[i]), ln_g[i, 0], ln_b[i, 0])
        if m == 0:
            y = swa_sink_attention(x, a_w_in[j], a_sinks[j], a_w_o[j])
        elif m == 1:
            y = mla_attention(x, b_w_in[j], b_q_norm[j], b_w_uq[j], b_kv_norm[j], b_w_ukv[j], b_w_o[j])
        elif m == 2:
            y = dsa_attention(x, c_w_in[j], c_w_o[j])
        else:
            y = stick_breaking_attention(x, d_w_in[j], d_w_o[j])
        x = layer_norm(ALPHA * x + y, ln_g[i, 1], ln_b[i, 1])
        x = layer_norm(ALPHA * x + 0.5 * swiglu(x, ffn2_w_gu[i], ffn2_w_down[i]), ln_g[i, 2], ln_b[i, 2])
    return x
```

```python
import functools
import math

import numpy as np
import jax
import jax.numpy as jnp
from jax import lax
from jax.experimental import pallas as pl
from jax.experimental.pallas import tpu as pltpu

D_MODEL = 1024
DEPTH = 4
N_MIXERS = 4
D_FF = 2816
LN_EPS = 1e-5
RMS_EPS = 1e-6
NEG = -1e30

SWA_HEADS, SWA_KV_HEADS, SWA_HEAD_DIM, SWA_BLOCK = 16, 4, 64, 128
MLA_HEADS, MLA_Q_LORA, MLA_KV_LORA, MLA_NOPE, MLA_ROPE, MLA_V = 16, 768, 256, 64, 32, 64
ROPE_THETA = 10000.0
DSA_HEADS, DSA_KV_HEADS, DSA_HEAD_DIM = 16, 4, 64
IDX_HEADS, IDX_DIM, IDX_TOPK = 8, 64, 256
SB_HEADS, SB_HEAD_DIM = 16, 64

ALPHA = (2.0 * DEPTH) ** 0.25

LANES = 128
HALF = 64
VMEM_LIMIT = 56 * 1024 * 1024
BF16 = jnp.bfloat16
F32 = jnp.float32


def _params(*sem):
    return pltpu.CompilerParams(dimension_semantics=sem, vmem_limit_bytes=VMEM_LIMIT)


def _dot(a, b):
    return jnp.dot(a, b, preferred_element_type=F32)


def _dot_t(a, b):
    return lax.dot_general(a, b, (((1,), (1,)), ((), ())), preferred_element_type=F32)


def _alibi_slopes(n_heads):
    return [2.0 ** (-8.0 * (i + 1) / n_heads) for i in range(n_heads)]


def _layer_norm(y, g, b):
    mu = jnp.mean(y, axis=-1, keepdims=True)
    yc = y - mu
    var = jnp.mean(yc * yc, axis=-1, keepdims=True)
    return yc * lax.rsqrt(var + LN_EPS) * g + b


def _lane_lo():
    return lax.broadcasted_iota(jnp.int32, (1, LANES), 1) < HALF


def _mm_kernel(x_ref, w_ref, o_ref):
    o_ref[...] = _dot(x_ref[...].astype(BF16), w_ref[...]).astype(o_ref.dtype)


def _col_tile(m, cap=1024):
    best = LANES
    for t in range(LANES, min(m, cap) + 1, LANES):
        if m % t == 0:
            best = t
    return best


def _matmul(x, w, out_dtype, tm=1024):
    n, k = x.shape
    m = w.shape[1]
    tm = min(tm, n)
    tn = _col_tile(m)
    return pl.pallas_call(
        _mm_kernel,
        grid=(n // tm, m // tn),
        in_specs=[pl.BlockSpec((tm, k), lambda i, j: (i, 0)),
                  pl.BlockSpec((k, tn), lambda i, j: (0, j))],
        out_specs=pl.BlockSpec((tm, tn), lambda i, j: (i, j)),
        out_shape=jax.ShapeDtypeStruct((n, m), out_dtype),
        compiler_params=_params("parallel", "arbitrary"),
        name="proj_matmul",
    )(x, w)


def _proj_ln_kernel(o_ref, w_ref, x_ref, g_ref, b_ref, out_ref):
    y = ALPHA * x_ref[...] + _dot(o_ref[...], w_ref[...])
    out_ref[...] = _layer_norm(y, g_ref[...], b_ref[...])


def _proj_ln(o, w, x, g, b, tm=512):
    n, k = o.shape
    d = w.shape[1]
    tm = min(tm, n)
    return pl.pallas_call(
        _proj_ln_kernel,
        grid=(n // tm,),
        in_specs=[pl.BlockSpec((tm, k), lambda i: (i, 0)),
                  pl.BlockSpec((k, d), lambda i: (0, 0)),
                  pl.BlockSpec((tm, d), lambda i: (i, 0)),
                  pl.BlockSpec((1, d), lambda i: (0, 0)),
                  pl.BlockSpec((1, d), lambda i: (0, 0))],
        out_specs=pl.BlockSpec((tm, d), lambda i: (i, 0)),
        out_shape=jax.ShapeDtypeStruct((n, d), F32),
        compiler_params=_params("parallel"),
        name="out_proj_ln",
    )(o, w, x, g.reshape(1, d), b.reshape(1, d))


def _ffn_kernel(x_ref, wg_ref, wu_ref, wd_ref, g_ref, b_ref, o_ref, xb_ref, acc_ref, *, nf):
    j = pl.program_id(1)

    @pl.when(j == 0)
    def _():
        xb_ref[...] = x_ref[...].astype(BF16)

    xb = xb_ref[...]
    h = _dot(xb, wg_ref[...])
    u = _dot(xb, wu_ref[...])
    a = h * (1.0 / (1.0 + jnp.exp(-h))) * u
    part = _dot(a.astype(BF16), wd_ref[...])

    @pl.when(j == 0)
    def _():
        acc_ref[...] = part

    @pl.when(j > 0)
    def _():
        acc_ref[...] += part

    @pl.when(j == nf - 1)
    def _():
        y = ALPHA * x_ref[...] + 0.5 * acc_ref[...]
        o_ref[...] = _layer_norm(y, g_ref[...], b_ref[...])


def _ffn_ln(x, w_gu, w_down, g, b, tm=1024, tf=256):
    n, d = x.shape
    f = w_down.shape[0]
    tm = min(tm, n)
    nf = f // tf
    return pl.pallas_call(
        functools.partial(_ffn_kernel, nf=nf),
        grid=(n // tm, nf),
        in_specs=[pl.BlockSpec((tm, d), lambda i, j: (i, 0)),
                  pl.BlockSpec((d, tf), lambda i, j: (0, j)),
                  pl.BlockSpec((d, tf), lambda i, j: (0, j + nf)),
                  pl.BlockSpec((tf, d), lambda i, j: (j, 0)),
                  pl.BlockSpec((1, d), lambda i, j: (0, 0)),
                  pl.BlockSpec((1, d), lambda i, j: (0, 0))],
        out_specs=pl.BlockSpec((tm, d), lambda i, j: (i, 0)),
        out_shape=jax.ShapeDtypeStruct((n, d), F32),
        scratch_shapes=[pltpu.VMEM((tm, d), BF16), pltpu.VMEM((tm, d), F32)],
        compiler_params=_params("parallel", "arbitrary"),
        name="ffn_ln",
    )(x, w_gu, w_gu, w_down, g.reshape(1, d), b.reshape(1, d))


def _swa_kernel(sink_ref, q_ref, kc_ref, kp_ref, vc_ref, vp_ref, o_ref):
    n = pl.program_id(1)
    t = SWA_BLOCK
    row = lax.broadcasted_iota(jnp.int32, (t, t), 0)
    col = lax.broadcasted_iota(jnp.int32, (t, t), 1)
    dist_c = (row - col).astype(F32)
    dist_p = dist_c + float(t)
    valid_c = row >= col
    valid_p = (row - col) < jnp.where(n > 0, 0, -2 * t)
    lo = _lane_lo()
    slopes = _alibi_slopes(SWA_HEADS)
    rep = SWA_HEADS // SWA_KV_HEADS
    for m in range(SWA_HEADS // 2):
        g = (2 * m) // rep
        qp = q_ref[0, :, m * LANES:(m + 1) * LANES]
        kc = kc_ref[0, :, g * LANES:(g + 1) * LANES]
        kp = kp_ref[0, :, g * LANES:(g + 1) * LANES]
        vc = vc_ref[0, :, g * LANES:(g + 1) * LANES]
        vp = vp_ref[0, :, g * LANES:(g + 1) * LANES]
        outs = []
        for hh in range(2):
            h = 2 * m + hh
            qm = jnp.where(lo if hh == 0 else jnp.logical_not(lo), qp, jnp.zeros_like(qp))
            sc = _dot_t(qm, kc) * (SWA_HEAD_DIM ** -0.5)
            sp = _dot_t(qm, kp) * (SWA_HEAD_DIM ** -0.5)
            sc = jnp.where(valid_c, sc - slopes[h] * dist_c, NEG)
            sp = jnp.where(valid_p, sp - slopes[h] * dist_p, NEG)
            sink = sink_ref[h]
            mx = jnp.maximum(jnp.maximum(jnp.max(sc, axis=-1, keepdims=True),
                                         jnp.max(sp, axis=-1, keepdims=True)), sink)
            pc = jnp.exp(sc - mx)
            pp = jnp.exp(sp - mx)
            den = (jnp.sum(pc, axis=-1, keepdims=True) + jnp.sum(pp, axis=-1, keepdims=True)
                   + jnp.exp(sink - mx))
            o = _dot(pc.astype(BF16), vc) + _dot(pp.astype(BF16), vp)
            outs.append(o / den)
        o_ref[0, :, m * LANES:(m + 1) * LANES] = jnp.where(lo, outs[0], outs[1]).astype(BF16)


def _swa_attention(qkv, sinks, batch, seq):
    t = SWA_BLOCK
    nb = seq // t
    qw = SWA_HEADS * SWA_HEAD_DIM
    kw = SWA_KV_HEADS * LANES
    qblk, kblk, vblk = 0, qw // kw, qw // kw + 1
    prev = lambda b, n: jnp.maximum(n - 1, 0)
    return pl.pallas_call(
        _swa_kernel,
        grid=(batch, nb),
        in_specs=[pl.BlockSpec(memory_space=pltpu.SMEM),
                  pl.BlockSpec((1, t, qw), lambda b, n: (b, n, qblk)),
                  pl.BlockSpec((1, t, kw), lambda b, n: (b, n, kblk)),
                  pl.BlockSpec((1, t, kw), lambda b, n: (b, prev(b, n), kblk)),
                  pl.BlockSpec((1, t, kw), lambda b, n: (b, n, vblk)),
                  pl.BlockSpec((1, t, kw), lambda b, n: (b, prev(b, n), vblk))],
        out_specs=pl.BlockSpec((1, t, qw), lambda b, n: (b, n, 0)),
        out_shape=jax.ShapeDtypeStruct((batch, seq, qw), BF16),
        compiler_params=_params("parallel", "arbitrary"),
        name="swa_attention",
    )(sinks, qkv, qkv, qkv, qkv, qkv)


def _mla_up_kernel(c_ref, gq_ref, gkv_ref, wq_ref, wk_ref, wv_ref, cq_ref, s1q_ref, s2q_ref,
                   ck_ref, s1k_ref, s2k_ref, q_ref, k_ref, v_ref):
    c = c_ref[...]
    cq = c[:, :MLA_Q_LORA]
    ckv = c[:, MLA_Q_LORA:MLA_Q_LORA + MLA_KV_LORA]
    kr = c[:, MLA_Q_LORA + MLA_KV_LORA:]
    cq = cq * lax.rsqrt(jnp.mean(cq * cq, axis=-1, keepdims=True) + RMS_EPS) * gq_ref[...]
    ckv = ckv * lax.rsqrt(jnp.mean(ckv * ckv, axis=-1, keepdims=True) + RMS_EPS) * gkv_ref[...]
    ckv_b = ckv.astype(BF16)
    q = _dot(cq.astype(BF16), wq_ref[...])
    kn = _dot(ckv_b, wk_ref[...])
    v_ref[...] = _dot(ckv_b, wv_ref[...]).astype(BF16)

    def rot(xh, cc, s1, s2):
        return xh * cc + pltpu.roll(xh, 16, 1) * s1 + pltpu.roll(xh, LANES - 16, 1) * s2

    kr = rot(kr, ck_ref[...], s1k_ref[...], s2k_ref[...])
    cq_t, s1q, s2q = cq_ref[...], s1q_ref[...], s2q_ref[...]
    for h in range(MLA_HEADS):
        sl = slice(h * LANES, (h + 1) * LANES)
        q_ref[:, sl] = rot(q[:, sl], cq_t, s1q, s2q).astype(BF16)
        k_ref[:, sl] = (kn[:, sl] + kr).astype(BF16)


def _mla_rope_tables(seq, scale):
    half = MLA_ROPE // 2
    inv = ROPE_THETA ** (-jnp.arange(0, MLA_ROPE, 2, dtype=F32) / MLA_ROPE)
    ang = jnp.arange(seq, dtype=F32)[:, None] * inv[None, :]
    cos, sin = jnp.cos(ang), jnp.sin(ang)
    z = lambda w: jnp.zeros((seq, w), F32)
    one = jnp.ones((seq, MLA_NOPE), F32)
    cc = jnp.concatenate([one, cos, cos, z(LANES - MLA_NOPE - MLA_ROPE)], axis=1) * scale
    s1 = jnp.concatenate([z(MLA_NOPE + half), sin, z(LANES - MLA_NOPE - MLA_ROPE)], axis=1) * scale
    s2 = jnp.concatenate([z(MLA_NOPE), -sin, z(LANES - MLA_NOPE - half)], axis=1) * scale
    return cc, s1, s2


def _mla_up(c, gq, gkv, wq, wk, wv, seq, tm=256):
    n, cw = c.shape
    tm = min(tm, seq)
    spt = seq // tm
    scale = (MLA_NOPE + MLA_ROPE) ** -0.5
    tq = _mla_rope_tables(seq, scale)
    tk = _mla_rope_tables(seq, 1.0)
    qk_w = MLA_HEADS * LANES
    v_w = MLA_HEADS * MLA_V
    full = lambda a: pl.BlockSpec(a.shape, lambda i: (0, 0))
    tab = pl.BlockSpec((tm, LANES), lambda i: (i % spt, 0))
    gq2, gkv2 = gq.reshape(1, -1), gkv.reshape(1, -1)
    return pl.pallas_call(
        _mla_up_kernel,
        grid=(n // tm,),
        in_specs=[pl.BlockSpec((tm, cw), lambda i: (i, 0)), full(gq2), full(gkv2), full(wq), full(wk),
                  full(wv), tab, tab, tab, tab, tab, tab],
        out_specs=[pl.BlockSpec((tm, qk_w), lambda i: (i, 0)),
                   pl.BlockSpec((tm, qk_w), lambda i: (i, 0)),
                   pl.BlockSpec((tm, v_w), lambda i: (i, 0))],
        out_shape=[jax.ShapeDtypeStruct((n, qk_w), BF16), jax.ShapeDtypeStruct((n, qk_w), BF16),
                   jax.ShapeDtypeStruct((n, v_w), BF16)],
        compiler_params=_params("parallel"),
        name="mla_up_proj",
    )(c, gq2, gkv2, wq, wk, wv, *tq, *tk)


def _softmax_tile(s, v, carry):
    m, l, acc = carry
    m_new = jnp.maximum(m, jnp.max(s, axis=-1, keepdims=True))
    a = jnp.exp(m - m_new)
    p = jnp.exp(s - m_new)
    l = a * l + jnp.sum(p, axis=-1, keepdims=True)
    acc = a * acc + _dot(p.astype(BF16), v)
    return m_new, l, acc


def _mla_attn_kernel(q_ref, k_ref, v_ref, o_ref, *, t):
    i = pl.program_id(2)
    row = lax.broadcasted_iota(jnp.int32, (t, t), 0)
    col = lax.broadcasted_iota(jnp.int32, (t, t), 1)
    causal = row >= col
    outs = []
    for hh in range(2):
        sl = slice(hh * LANES, (hh + 1) * LANES)
        q = q_ref[0, :, sl]

        def tile(j, mask, carry):
            start = pl.multiple_of(j * t, t)
            k = k_ref[0, pl.ds(start, t), sl]
            v = v_ref[0, pl.ds(start, t), :]
            s = _dot_t(q, k)
            if mask:
                s = jnp.where(causal, s, NEG)
            return _softmax_tile(s, v, carry)

        init = (jnp.full((t, 1), NEG, F32), jnp.zeros((t, 1), F32), jnp.zeros((t, LANES), F32))
        carry = tile(i, True, init)
        m, l, acc = lax.fori_loop(0, i, lambda j, c: tile(j, False, c), carry)
        outs.append(acc / l)
    o_ref[0] = jnp.where(_lane_lo(), outs[0], outs[1]).astype(BF16)


def _mla_attention(q, k, v, t=256):
    batch, seq, _ = q.shape
    t = min(t, seq)
    return pl.pallas_call(
        functools.partial(_mla_attn_kernel, t=t),
        grid=(batch, MLA_HEADS // 2, seq // t),
        in_specs=[pl.BlockSpec((1, t, 2 * LANES), lambda b, m, i: (b, i, m)),
                  pl.BlockSpec((1, seq, 2 * LANES), lambda b, m, i: (b, 0, m)),
                  pl.BlockSpec((1, seq, LANES), lambda b, m, i: (b, 0, m))],
        out_specs=pl.BlockSpec((1, t, LANES), lambda b, m, i: (b, i, m)),
        out_shape=jax.ShapeDtypeStruct(v.shape, BF16),
        compiler_params=_params("parallel", "parallel", "arbitrary"),
        name="mla_attention",
    )(q, k, v)


def _order_key(x):
    bits = lax.bitcast_convert_type(x, jnp.int32)
    return bits ^ (lax.shift_right_arithmetic(bits, 31) & 0x7FFFFFFF)


def _dsa_kernel(slope_ref, qi_ref, ki_ref, wi_ref, q_ref, k_ref, v_ref, o_ref, key_ref, bias_ref,
                dist_ref, *, t, topk):
    i = pl.program_id(1)
    m = pl.program_id(2)
    lo = _lane_lo()
    row = lax.broadcasted_iota(jnp.int32, (t, t), 0)
    col = lax.broadcasted_iota(jnp.int32, (t, t), 1)
    rel = row - col

    @pl.when(m == 0)
    def _select():
        wi = wi_ref[0] * (IDX_HEADS ** -0.5)

        def score_tile(j, carry):
            start = pl.multiple_of(j * t, t)
            ki = ki_ref[0, pl.ds(start, t), :]
            sc = jnp.zeros((t, t), F32)
            for p in range(IDX_HEADS // 2):
                qp = qi_ref[0, :, p * LANES:(p + 1) * LANES]
                for hh in range(2):
                    h = 2 * p + hh
                    qm = jnp.where(lo if hh == 0 else jnp.logical_not(lo), qp, jnp.zeros_like(qp))
                    dots = _dot_t(qm, ki) * (IDX_DIM ** -0.5)
                    sc = sc + wi[:, h:h + 1] * jnp.maximum(dots, 0.0)
            sc = sc + 0.0
            sc = jnp.where(rel >= (j - i) * t, sc, NEG)
            key_ref[j] = _order_key(sc)
            dist_ref[j] = ((i - j) * t + rel).astype(F32)
            return carry

        lax.fori_loop(0, i + 1, score_tile, 0)

        def count_ge(cand):
            def body(j, part):
                ge = jnp.where(key_ref[j] >= cand, 1.0, 0.0)
                for c0 in range(0, t, LANES):
                    part = part + ge[:, c0:c0 + LANES]
                return part
            part = lax.fori_loop(0, i + 1, body, jnp.zeros((t, LANES), F32))
            return jnp.sum(part, axis=-1, keepdims=True)

        int_min = jnp.int32(-2 ** 31)
        base = jnp.where(count_ge(jnp.zeros((t, 1), jnp.int32)) >= topk, jnp.int32(0), int_min)

        def bit_step(b, base):
            cand = base | lax.shift_left(jnp.int32(1), 30 - b)
            return jnp.where(count_ge(cand) >= topk, cand, base)

        thr = lax.fori_loop(0, 31, bit_step, base)
        n_gt = count_ge(thr + 1)
        need = topk - n_gt
        tri = jnp.where(row < col, 1.0, 0.0).astype(BF16)

        def select_tile(j, seen):
            key = key_ref[j]
            eq = key == thr
            eq_b = jnp.where(eq, 1.0, 0.0).astype(BF16)
            rank = seen + _dot(eq_b, tri)
            sel = jnp.logical_or(key > thr, jnp.logical_and(eq, rank < need))
            sel = jnp.logical_and(sel, rel >= (j - i) * t)
            bias_ref[j] = jnp.where(sel, 0.0, NEG)
            return seen + jnp.sum(eq_b.astype(F32), axis=-1, keepdims=True)

        lax.fori_loop(0, i + 1, select_tile, jnp.zeros((t, 1), F32))

    qp = q_ref[0]
    outs = []
    for hh in range(2):
        qm = jnp.where(lo if hh == 0 else jnp.logical_not(lo), qp, jnp.zeros_like(qp))
        slope = slope_ref[2 * m + hh]

        def tile(j, carry):
            start = pl.multiple_of(j * t, t)
            k = k_ref[0, pl.ds(start, t), :]
            v = v_ref[0, pl.ds(start, t), :]
            s = _dot_t(qm, k) * (DSA_HEAD_DIM ** -0.5) - slope * dist_ref[j] + bias_ref[j]
            return _softmax_tile(s, v, carry)

        init = (jnp.full((t, 1), NEG, F32), jnp.zeros((t, 1), F32), jnp.zeros((t, LANES), F32))
        carry = tile(i, init)
        mx, l, acc = lax.fori_loop(0, i, tile, carry)
        outs.append(acc / l)
    o_ref[0] = jnp.where(lo, outs[0], outs[1]).astype(BF16)


def _dsa_attention(proj, wi, batch, seq, t=256):
    t = min(t, seq)
    nt = seq // t
    topk = min(IDX_TOPK, seq // 4)
    qw = DSA_HEADS * DSA_HEAD_DIM
    q0 = 0
    k0 = qw // LANES
    v0 = k0 + DSA_KV_HEADS
    qi0 = (v0 + DSA_KV_HEADS) * LANES // (IDX_HEADS * IDX_DIM)
    ki0 = v0 + DSA_KV_HEADS + IDX_HEADS * IDX_DIM // LANES
    pairs = DSA_HEADS // 2
    per_kv = pairs // DSA_KV_HEADS
    return pl.pallas_call(
        functools.partial(_dsa_kernel, t=t, topk=topk),
        grid=(batch, nt, pairs),
        in_specs=[pl.BlockSpec(memory_space=pltpu.SMEM),
                  pl.BlockSpec((1, t, IDX_HEADS * IDX_DIM), lambda b, i, m: (b, i, qi0)),
                  pl.BlockSpec((1, seq, LANES), lambda b, i, m: (b, 0, ki0)),
                  pl.BlockSpec((1, t, LANES), lambda b, i, m: (b, i, 0)),
                  pl.BlockSpec((1, t, LANES), lambda b, i, m: (b, i, q0 + m)),
                  pl.BlockSpec((1, seq, LANES), lambda b, i, m: (b, 0, k0 + m // per_kv)),
                  pl.BlockSpec((1, seq, LANES), lambda b, i, m: (b, 0, v0 + m // per_kv))],
        out_specs=pl.BlockSpec((1, t, LANES), lambda b, i, m: (b, i, m)),
        out_shape=jax.ShapeDtypeStruct((batch, seq, qw), BF16),
        scratch_shapes=[pltpu.VMEM((nt, t, t), jnp.int32), pltpu.VMEM((nt, t, t), F32),
                        pltpu.VMEM((nt, t, t), F32)],
        compiler_params=_params("parallel", "arbitrary", "arbitrary"),
        name="dsa_attention",
    )(jnp.asarray(_alibi_slopes(DSA_HEADS), F32), proj, proj, wi, proj, proj, proj)


def _sb_kernel(q_ref, k_ref, v_ref, o_ref, *, t):
    i = pl.program_id(2)
    lo = _lane_lo()
    row = lax.broadcasted_iota(jnp.int32, (t, t), 0)
    col = lax.broadcasted_iota(jnp.int32, (t, t), 1)
    strict = row > col
    suffix = jnp.where(row >= col, 1.0, 0.0).astype(BF16)
    qp = q_ref[0]
    outs = []
    for hh in range(2):
        qm = jnp.where(lo if hh == 0 else jnp.logical_not(lo), qp, jnp.zeros_like(qp))

        def tile(j, mask, carry):
            later, acc = carry
            start = pl.multiple_of(j * t, t)
            k = k_ref[0, pl.ds(start, t), :]
            v = v_ref[0, pl.ds(start, t), :]
            z = _dot_t(qm, k) * (SB_HEAD_DIM ** -0.5)
            lg = -(jnp.maximum(z, 0.0) + jnp.log(1.0 + jnp.exp(-jnp.abs(z))))
            if mask:
                lg = jnp.where(strict, lg, 0.0)
            hi = lg.astype(BF16)
            lw = (lg - hi.astype(F32)).astype(BF16)
            run = _dot(hi, suffix) + _dot(lw, suffix)
            a = jnp.exp(z + run + later)
            if mask:
                a = jnp.where(strict, a, 0.0)
            acc = acc + _dot(a.astype(BF16), v)
            later = later + jnp.sum(lg, axis=-1, keepdims=True)
            return later, acc

        carry = tile(i, True, (jnp.zeros((t, 1), F32), jnp.zeros((t, LANES), F32)))
        _, acc = lax.fori_loop(0, i, lambda jj, c: tile(i - 1 - jj, False, c), carry)
        outs.append(acc)
    o_ref[0] = jnp.where(lo, outs[0], outs[1]).astype(BF16)


def _sb_attention(qkv, batch, seq, t=256):
    t = min(t, seq)
    w = SB_HEADS * SB_HEAD_DIM
    pairs = SB_HEADS // 2
    return pl.pallas_call(
        functools.partial(_sb_kernel, t=t),
        grid=(batch, pairs, seq // t),
        in_specs=[pl.BlockSpec((1, t, LANES), lambda b, m, i: (b, i, m)),
                  pl.BlockSpec((1, seq, LANES), lambda b, m, i: (b, 0, pairs + m)),
                  pl.BlockSpec((1, seq, LANES), lambda b, m, i: (b, 0, 2 * pairs + m))],
        out_specs=pl.BlockSpec((1, t, LANES), lambda b, m, i: (b, i, m)),
        out_shape=jax.ShapeDtypeStruct((batch, seq, w), BF16),
        compiler_params=_params("parallel", "parallel", "arbitrary"),
        name="sb_attention",
    )(qkv, qkv, qkv)


def _dup_heads(w, n_heads, dim):
    d = w.shape[0]
    w = w.reshape(d, n_heads, 1, dim)
    return jnp.broadcast_to(w, (d, n_heads, 2, dim)).reshape(d, n_heads * 2 * dim)


def _pad_heads(w, n_heads, dim, to):
    d = w.shape[0]
    w = w.reshape(d, n_heads, dim)
    return jnp.pad(w, ((0, 0), (0, 0), (0, to - dim))).reshape(d, n_heads * to)


def _swa_w_in(w):
    qw = SWA_HEADS * SWA_HEAD_DIM
    kw = SWA_KV_HEADS * SWA_HEAD_DIM
    q, k, v = w[:, :qw], w[:, qw:qw + kw], w[:, qw + kw:]
    return jnp.concatenate([q, _dup_heads(k, SWA_KV_HEADS, SWA_HEAD_DIM),
                            _dup_heads(v, SWA_KV_HEADS, SWA_HEAD_DIM)], axis=1).astype(BF16)


def _mla_w_in(w):
    d = w.shape[0]
    lat = MLA_Q_LORA + MLA_KV_LORA
    z = lambda c: jnp.zeros((d, c), w.dtype)
    return jnp.concatenate([w[:, :lat], z(MLA_NOPE), w[:, lat:], z(LANES - MLA_NOPE - MLA_ROPE)],
                           axis=1).astype(BF16)


def _dsa_w_in(w):
    qw = DSA_HEADS * DSA_HEAD_DIM
    kw = DSA_KV_HEADS * DSA_HEAD_DIM
    iw = IDX_HEADS * IDX_DIM
    c = np.cumsum([qw, kw, kw, iw, IDX_DIM]).tolist()
    q, k, v, qi, ki, wi = (w[:, :c[0]], w[:, c[0]:c[1]], w[:, c[1]:c[2]], w[:, c[2]:c[3]],
                           w[:, c[3]:c[4]], w[:, c[4]:])
    main = jnp.concatenate([q, _dup_heads(k, DSA_KV_HEADS, DSA_HEAD_DIM),
                            _dup_heads(v, DSA_KV_HEADS, DSA_HEAD_DIM), qi, _dup_heads(ki, 1, IDX_DIM)],
                           axis=1).astype(BF16)
    wi = jnp.pad(wi, ((0, 0), (0, LANES - IDX_HEADS))).astype(BF16)
    return main, wi


def kernel(x, ffn1_w_gu, ffn1_w_down, ffn2_w_gu, ffn2_w_down, ln_g, ln_b, a_w_in, a_sinks, a_w_o,
           b_w_in, b_q_norm, b_w_uq, b_kv_norm, b_w_ukv, b_w_o, c_w_in, c_w_o, d_w_in, d_w_o):
    batch, seq, d = x.shape
    n = batch * seq
    x = x.reshape(n, d)
    for i in range(DEPTH):
        mixer, j = i % N_MIXERS, i // N_MIXERS
        x = _ffn_ln(x, ffn1_w_gu[i].astype(BF16), ffn1_w_down[i].astype(BF16), ln_g[i, 0], ln_b[i, 0])
        if mixer == 0:
            qkv = _matmul(x, _swa_w_in(a_w_in[j]), BF16)
            o = _swa_attention(qkv.reshape(batch, seq, -1), a_sinks[j], batch, seq)
            w_o = a_w_o[j]
        elif mixer == 1:
            c = _matmul(x, _mla_w_in(b_w_in[j]), F32)
            hd = MLA_NOPE + MLA_ROPE
            wq = _pad_heads(b_w_uq[j], MLA_HEADS, hd, LANES).astype(BF16)
            wkv = b_w_ukv[j].reshape(MLA_KV_LORA, MLA_HEADS, MLA_NOPE + MLA_V)
            wk = _pad_heads(wkv[:, :, :MLA_NOPE].reshape(MLA_KV_LORA, -1), MLA_HEADS, MLA_NOPE,
                            LANES).astype(BF16)
            wv = wkv[:, :, MLA_NOPE:].reshape(MLA_KV_LORA, -1).astype(BF16)
            q, k, v = _mla_up(c, b_q_norm[j], b_kv_norm[j], wq, wk, wv, seq)
            o = _mla_attention(q.reshape(batch, seq, -1), k.reshape(batch, seq, -1),
                               v.reshape(batch, seq, -1))
            w_o = b_w_o[j]
        elif mixer == 2:
            w_main, w_wi = _dsa_w_in(c_w_in[j])
            proj = _matmul(x, w_main, BF16)
            wi = _matmul(x, w_wi, F32)
            o = _dsa_attention(proj.reshape(batch, seq, -1), wi.reshape(batch, seq, -1), batch, seq)
            w_o = c_w_o[j]
        else:
            qkv = _matmul(x, d_w_in[j].astype(BF16), BF16)
            o = _sb_attention(qkv.reshape(batch, seq, -1), batch, seq)
            w_o = d_w_o[j]
        x = _proj_ln(o.reshape(n, -1), w_o.astype(BF16), x, ln_g[i, 1], ln_b[i, 1])
        x = _ffn_ln(x, ffn2_w_gu[i].astype(BF16), ffn2_w_down[i].astype(BF16), ln_g[i, 2], ln_b[i, 2])
    return x.reshape(batch, seq, d)
```

```python
import functools
import math

import numpy as np
import jax
import jax.numpy as jnp
from jax import lax
from jax.experimental import pallas as pl
from jax.experimental.pallas import tpu as pltpu

D_MODEL = 1024
DEPTH = 4
N_MIXERS = 4
D_FF = 2816
LN_EPS = 1e-5
RMS_EPS = 1e-6
NEG = -1e30

SWA_HEADS, SWA_KV_HEADS, SWA_HEAD_DIM, SWA_BLOCK = 16, 4, 64, 128
MLA_HEADS, MLA_Q_LORA, MLA_KV_LORA, MLA_NOPE, MLA_ROPE, MLA_V = 16, 768, 256, 64, 32, 64
ROPE_THETA = 10000.0
DSA_HEADS, DSA_KV_HEADS, DSA_HEAD_DIM = 16, 4, 64
IDX_HEADS, IDX_DIM, IDX_TOPK = 8, 64, 256
SB_HEADS, SB_HEAD_DIM = 16, 64

ALPHA = (2.0 * DEPTH) ** 0.25
LOG2E = math.log2(math.e)

LANES = 128
HALF = 64
VMEM_LIMIT = 56 * 1024 * 1024
BF16 = jnp.bfloat16
F32 = jnp.float32


def _params(*sem, flags=None):
    return pltpu.CompilerParams(dimension_semantics=sem, vmem_limit_bytes=VMEM_LIMIT, flags=flags)


def _dot(a, b):
    return jnp.dot(a, b, preferred_element_type=F32)


def _dot_t(a, b):
    return lax.dot_general(a, b, (((1,), (1,)), ((), ())), preferred_element_type=F32)


def _alibi_slopes(n_heads):
    return [2.0 ** (-8.0 * (i + 1) / n_heads) for i in range(n_heads)]


def _layer_norm(y, g, b):
    mu = jnp.mean(y, axis=-1, keepdims=True)
    yc = y - mu
    var = jnp.mean(yc * yc, axis=-1, keepdims=True)
    return yc * lax.rsqrt(var + LN_EPS) * g + b


def _lane_lo():
    return lax.broadcasted_iota(jnp.int32, (1, LANES), 1) < HALF


def _mm_kernel(x_ref, w_ref, o_ref):
    o_ref[...] = _dot(x_ref[...].astype(BF16), w_ref[...]).astype(o_ref.dtype)


def _col_tile(m, cap=1024):
    best = LANES
    for t in range(LANES, min(m, cap) + 1, LANES):
        if m % t == 0:
            best = t
    return best


def _matmul(x, w, out_dtype, tm=1024):
    n, k = x.shape
    m = w.shape[1]
    tm = min(tm, n)
    tn = _col_tile(m)
    return pl.pallas_call(
        _mm_kernel,
        grid=(n // tm, m // tn),
        in_specs=[pl.BlockSpec((tm, k), lambda i, j: (i, 0)),
                  pl.BlockSpec((k, tn), lambda i, j: (0, j))],
        out_specs=pl.BlockSpec((tm, tn), lambda i, j: (i, j)),
        out_shape=jax.ShapeDtypeStruct((n, m), out_dtype),
        compiler_params=_params("parallel", "arbitrary"),
        name="proj_matmul",
    )(x, w)


def _proj_ln_kernel(o_ref, w_ref, x_ref, g_ref, b_ref, out_ref):
    y = ALPHA * x_ref[...] + _dot(o_ref[...], w_ref[...])
    out_ref[...] = _layer_norm(y, g_ref[...], b_ref[...])


def _proj_ln(o, w, x, g, b, tm=512):
    n, k = o.shape
    d = w.shape[1]
    tm = min(tm, n)
    return pl.pallas_call(
        _proj_ln_kernel,
        grid=(n // tm,),
        in_specs=[pl.BlockSpec((tm, k), lambda i: (i, 0)),
                  pl.BlockSpec((k, d), lambda i: (0, 0)),
                  pl.BlockSpec((tm, d), lambda i: (i, 0)),
                  pl.BlockSpec((1, d), lambda i: (0, 0)),
                  pl.BlockSpec((1, d), lambda i: (0, 0))],
        out_specs=pl.BlockSpec((tm, d), lambda i: (i, 0)),
        out_shape=jax.ShapeDtypeStruct((n, d), F32),
        compiler_params=_params("parallel"),
        name="out_proj_ln",
    )(o, w, x, g.reshape(1, d), b.reshape(1, d))


def _ffn_kernel(x_ref, wg_ref, wu_ref, wd_ref, g_ref, b_ref, o_ref, xb_ref, acc_ref, *, nf):
    j = pl.program_id(1)

    @pl.when(j == 0)
    def _():
        xb_ref[...] = x_ref[...].astype(BF16)

    xb = xb_ref[...]
    h = _dot(xb, wg_ref[...])
    u = _dot(xb, wu_ref[...])
    a = h * (1.0 / (1.0 + jnp.exp(-h))) * u
    part = _dot(a.astype(BF16), wd_ref[...])

    @pl.when(j == 0)
    def _():
        acc_ref[...] = part

    @pl.when(j > 0)
    def _():
        acc_ref[...] += part

    @pl.when(j == nf - 1)
    def _():
        y = ALPHA * x_ref[...] + 0.5 * acc_ref[...]
        o_ref[...] = _layer_norm(y, g_ref[...], b_ref[...])


def _ffn_ln(x, w_gu, w_down, g, b, tm=1024, tf=256):
    n, d = x.shape
    f = w_down.shape[0]
    tm = min(tm, n)
    nf = f // tf
    return pl.pallas_call(
        functools.partial(_ffn_kernel, nf=nf),
        grid=(n // tm, nf),
        in_specs=[pl.BlockSpec((tm, d), lambda i, j: (i, 0)),
                  pl.BlockSpec((d, tf), lambda i, j: (0, j)),
                  pl.BlockSpec((d, tf), lambda i, j: (0, j + nf)),
                  pl.BlockSpec((tf, d), lambda i, j: (j, 0)),
                  pl.BlockSpec((1, d), lambda i, j: (0, 0)),
                  pl.BlockSpec((1, d), lambda i, j: (0, 0))],
        out_specs=pl.BlockSpec((tm, d), lambda i, j: (i, 0)),
        out_shape=jax.ShapeDtypeStruct((n, d), F32),
        scratch_shapes=[pltpu.VMEM((tm, d), BF16), pltpu.VMEM((tm, d), F32)],
        compiler_params=_params("parallel", "arbitrary"),
        name="ffn_ln",
    )(x, w_gu, w_gu, w_down, g.reshape(1, d), b.reshape(1, d))


def _swa_kernel(sink_ref, q_ref, kc_ref, kp_ref, vc_ref, vp_ref, o_ref):
    n = pl.program_id(1)
    t = SWA_BLOCK
    row = lax.broadcasted_iota(jnp.int32, (t, t), 0)
    col = lax.broadcasted_iota(jnp.int32, (t, t), 1)
    dist_c = (row - col).astype(F32)
    dist_p = dist_c + float(t)
    valid_c = row >= col
    valid_p = (row - col) < jnp.where(n > 0, 0, -2 * t)
    lo = _lane_lo()
    slopes = _alibi_slopes(SWA_HEADS)
    rep = SWA_HEADS // SWA_KV_HEADS
    for m in range(SWA_HEADS // 2):
        g = (2 * m) // rep
        qp = q_ref[0, :, m * LANES:(m + 1) * LANES]
        kc = kc_ref[0, :, g * LANES:(g + 1) * LANES]
        kp = kp_ref[0, :, g * LANES:(g + 1) * LANES]
        vc = vc_ref[0, :, g * LANES:(g + 1) * LANES]
        vp = vp_ref[0, :, g * LANES:(g + 1) * LANES]
        outs = []
        for hh in range(2):
            h = 2 * m + hh
            qm = jnp.where(lo if hh == 0 else jnp.logical_not(lo), qp, jnp.zeros_like(qp))
            sc = _dot_t(qm, kc) * (SWA_HEAD_DIM ** -0.5)
            sp = _dot_t(qm, kp) * (SWA_HEAD_DIM ** -0.5)
            sc = jnp.where(valid_c, sc - slopes[h] * dist_c, NEG)
            sp = jnp.where(valid_p, sp - slopes[h] * dist_p, NEG)
            sink = sink_ref[h]
            mx = jnp.maximum(jnp.maximum(jnp.max(sc, axis=-1, keepdims=True),
                                         jnp.max(sp, axis=-1, keepdims=True)), sink)
            pc = jnp.exp(sc - mx)
            pp = jnp.exp(sp - mx)
            den = (jnp.sum(pc, axis=-1, keepdims=True) + jnp.sum(pp, axis=-1, keepdims=True)
                   + jnp.exp(sink - mx))
            o = _dot(pc.astype(BF16), vc) + _dot(pp.astype(BF16), vp)
            outs.append(o / den)
        o_ref[0, :, m * LANES:(m + 1) * LANES] = jnp.where(lo, outs[0], outs[1]).astype(BF16)


def _swa_attention(qkv, sinks, batch, seq):
    t = SWA_BLOCK
    nb = seq // t
    qw = SWA_HEADS * SWA_HEAD_DIM
    kw = SWA_KV_HEADS * LANES
    qblk, kblk, vblk = 0, qw // kw, qw // kw + 1
    prev = lambda b, n: jnp.maximum(n - 1, 0)
    return pl.pallas_call(
        _swa_kernel,
        grid=(batch, nb),
        in_specs=[pl.BlockSpec(memory_space=pltpu.SMEM),
                  pl.BlockSpec((1, t, qw), lambda b, n: (b, n, qblk)),
                  pl.BlockSpec((1, t, kw), lambda b, n: (b, n, kblk)),
                  pl.BlockSpec((1, t, kw), lambda b, n: (b, prev(b, n), kblk)),
                  pl.BlockSpec((1, t, kw), lambda b, n: (b, n, vblk)),
                  pl.BlockSpec((1, t, kw), lambda b, n: (b, prev(b, n), vblk))],
        out_specs=pl.BlockSpec((1, t, qw), lambda b, n: (b, n, 0)),
        out_shape=jax.ShapeDtypeStruct((batch, seq, qw), BF16),
        compiler_params=_params("parallel", "arbitrary"),
        name="swa_attention",
    )(sinks, qkv, qkv, qkv, qkv, qkv)


def _mla_up_kernel(c_ref, gq_ref, gkv_ref, wq_ref, wk_ref, wv_ref, cq_ref, s1q_ref, s2q_ref,
                   ck_ref, s1k_ref, s2k_ref, q_ref, k_ref, v_ref):
    c = c_ref[...]
    cq = c[:, :MLA_Q_LORA]
    ckv = c[:, MLA_Q_LORA:MLA_Q_LORA + MLA_KV_LORA]
    kr = c[:, MLA_Q_LORA + MLA_KV_LORA:]
    cq = cq * lax.rsqrt(jnp.mean(cq * cq, axis=-1, keepdims=True) + RMS_EPS) * gq_ref[...]
    ckv = ckv * lax.rsqrt(jnp.mean(ckv * ckv, axis=-1, keepdims=True) + RMS_EPS) * gkv_ref[...]
    ckv_b = ckv.astype(BF16)
    q = _dot(cq.astype(BF16), wq_ref[...])
    kn = _dot(ckv_b, wk_ref[...])
    v_ref[...] = _dot(ckv_b, wv_ref[...]).astype(BF16)

    def rot(xh, cc, s1, s2):
        return xh * cc + pltpu.roll(xh, 16, 1) * s1 + pltpu.roll(xh, LANES - 16, 1) * s2

    kr = rot(kr, ck_ref[...], s1k_ref[...], s2k_ref[...])
    cq_t, s1q, s2q = cq_ref[...], s1q_ref[...], s2q_ref[...]
    for h in range(MLA_HEADS):
        sl = slice(h * LANES, (h + 1) * LANES)
        q_ref[:, sl] = rot(q[:, sl], cq_t, s1q, s2q).astype(BF16)
        k_ref[:, sl] = (kn[:, sl] + kr).astype(BF16)


def _mla_rope_tables(seq, scale):
    half = MLA_ROPE // 2
    inv = ROPE_THETA ** (-jnp.arange(0, MLA_ROPE, 2, dtype=F32) / MLA_ROPE)
    ang = jnp.arange(seq, dtype=F32)[:, None] * inv[None, :]
    cos, sin = jnp.cos(ang), jnp.sin(ang)
    z = lambda w: jnp.zeros((seq, w), F32)
    one = jnp.ones((seq, MLA_NOPE), F32)
    cc = jnp.concatenate([one, cos, cos, z(LANES - MLA_NOPE - MLA_ROPE)], axis=1) * scale
    s1 = jnp.concatenate([z(MLA_NOPE + half), sin, z(LANES - MLA_NOPE - MLA_ROPE)], axis=1) * scale
    s2 = jnp.concatenate([z(MLA_NOPE), -sin, z(LANES - MLA_NOPE - half)], axis=1) * scale
    return cc, s1, s2


def _mla_up(c, gq, gkv, wq, wk, wv, seq, tm=256):
    n, cw = c.shape
    tm = min(tm, seq)
    spt = seq // tm
    scale = LOG2E * (MLA_NOPE + MLA_ROPE) ** -0.5
    tq = _mla_rope_tables(seq, scale)
    tk = _mla_rope_tables(seq, 1.0)
    qk_w = MLA_HEADS * LANES
    v_w = MLA_HEADS * MLA_V
    full = lambda a: pl.BlockSpec(a.shape, lambda i: (0, 0))
    tab = pl.BlockSpec((tm, LANES), lambda i: (i % spt, 0))
    gq2, gkv2 = gq.reshape(1, -1), gkv.reshape(1, -1)
    return pl.pallas_call(
        _mla_up_kernel,
        grid=(n // tm,),
        in_specs=[pl.BlockSpec((tm, cw), lambda i: (i, 0)), full(gq2), full(gkv2), full(wq), full(wk),
                  full(wv), tab, tab, tab, tab, tab, tab],
        out_specs=[pl.BlockSpec((tm, qk_w), lambda i: (i, 0)),
                   pl.BlockSpec((tm, qk_w), lambda i: (i, 0)),
                   pl.BlockSpec((tm, v_w), lambda i: (i, 0))],
        out_shape=[jax.ShapeDtypeStruct((n, qk_w), BF16), jax.ShapeDtypeStruct((n, qk_w), BF16),
                   jax.ShapeDtypeStruct((n, v_w), BF16)],
        compiler_params=_params("parallel"),
        name="mla_up_proj",
    )(c, gq2, gkv2, wq, wk, wv, *tq, *tk)


def _flash_pair(ss, v, carry):
    hs = range(2)
    ms = [jnp.maximum(carry[hh][0], jnp.max(ss[hh], axis=-1, keepdims=True)) for hh in hs]
    ps = [jnp.exp2(ss[hh] - ms[hh]) for hh in hs]
    al = [jnp.exp2(carry[hh][0] - ms[hh]) for hh in hs]
    ls = [al[hh] * carry[hh][1] + jnp.sum(ps[hh], axis=-1, keepdims=True) for hh in hs]
    accs = [al[hh] * carry[hh][2] + _dot(ps[hh].astype(BF16), v) for hh in hs]
    return tuple((ms[hh], ls[hh], accs[hh]) for hh in hs)


def _tile_rel(tq, tk):
    return (lax.broadcasted_iota(jnp.int32, (tq, tk), 0) - lax.broadcasted_iota(jnp.int32, (tq, tk), 1))


def _mla_attn_kernel(q_ref, k_ref, v_ref, o_ref, *, tq, tk):
    i = pl.program_id(2)
    r = tq // tk
    rel = _tile_rel(tq, tk)
    sls = [slice(hh * LANES, (hh + 1) * LANES) for hh in range(2)]
    qs = [q_ref[0, :, sl] for sl in sls]

    def tile(j, diag, carry):
        start = pl.multiple_of(j * tk, tk)
        v = v_ref[0, pl.ds(start, tk), :]
        ss = [_dot_t(qs[hh], k_ref[0, pl.ds(start, tk), sls[hh]]) for hh in range(2)]
        if diag is not None:
            ss = [jnp.where(rel >= diag * tk, s, NEG) for s in ss]
        return _flash_pair(ss, v, carry)

    init = (jnp.full((tq, 1), NEG, F32), jnp.zeros((tq, 1), F32), jnp.zeros((tq, LANES), F32))
    carry = (init, init)
    for d in range(r):
        carry = tile(i * r + d, d, carry)
    carry = lax.fori_loop(0, i * r, lambda j, c: tile(j, None, c), carry)
    outs = [acc / l for (_, l, acc) in carry]
    o_ref[0] = jnp.where(_lane_lo(), outs[0], outs[1]).astype(BF16)


def _mla_attention(q, k, v, tq=512, tk=512):
    batch, seq, _ = q.shape
    tq, tk = min(tq, seq), min(tk, seq)
    return pl.pallas_call(
        functools.partial(_mla_attn_kernel, tq=tq, tk=tk),
        grid=(batch, MLA_HEADS // 2, seq // tq),
        in_specs=[pl.BlockSpec((1, tq, 2 * LANES), lambda b, m, i: (b, i, m)),
                  pl.BlockSpec((1, seq, 2 * LANES), lambda b, m, i: (b, 0, m)),
                  pl.BlockSpec((1, seq, LANES), lambda b, m, i: (b, 0, m))],
        out_specs=pl.BlockSpec((1, tq, LANES), lambda b, m, i: (b, i, m)),
        out_shape=jax.ShapeDtypeStruct(v.shape, BF16),
        compiler_params=_params("parallel", "parallel", "arbitrary"),
        name="mla_attention",
    )(q, k, v)


def _order_key(x):
    bits = lax.bitcast_convert_type(x, jnp.int32)
    return bits ^ (lax.shift_right_arithmetic(bits, 31) & 0x7FFFFFFF)


def _dsa_kernel(slope_ref, qi_ref, ki_ref, wi_ref, q_ref, k_ref, v_ref, o_ref, key_ref, bias_ref,
                dist_ref, *, t, topk):
    i = pl.program_id(1)
    m = pl.program_id(2)
    lo = _lane_lo()
    row = lax.broadcasted_iota(jnp.int32, (t, t), 0)
    col = lax.broadcasted_iota(jnp.int32, (t, t), 1)
    rel = row - col

    @pl.when(m == 0)
    def _select():
        wi = wi_ref[0] * (IDX_HEADS ** -0.5)

        def score_tile(j, carry):
            start = pl.multiple_of(j * t, t)
            ki = ki_ref[0, pl.ds(start, t), :]
            sc = jnp.zeros((t, t), F32)
            for p in range(IDX_HEADS // 2):
                qp = qi_ref[0, :, p * LANES:(p + 1) * LANES]
                for hh in range(2):
                    h = 2 * p + hh
                    qm = jnp.where(lo if hh == 0 else jnp.logical_not(lo), qp, jnp.zeros_like(qp))
                    dots = _dot_t(qm, ki) * (IDX_DIM ** -0.5)
                    sc = sc + wi[:, h:h + 1] * jnp.maximum(dots, 0.0)
            sc = sc + 0.0
            sc = jnp.where(rel >= (j - i) * t, sc, NEG)
            key_ref[j] = _order_key(sc)
            dist_ref[j] = ((i - j) * t + rel).astype(F32)
            return carry

        lax.fori_loop(0, i + 1, score_tile, 0)

        def count_ge(cand):
            def body(j, part):
                ge = jnp.where(key_ref[j] >= cand, 1.0, 0.0)
                for c0 in range(0, t, LANES):
                    part = part + ge[:, c0:c0 + LANES]
                return part
            part = lax.fori_loop(0, i + 1, body, jnp.zeros((t, LANES), F32))
            return jnp.sum(part, axis=-1, keepdims=True)

        int_min = jnp.int32(-2 ** 31)
        base = jnp.where(count_ge(jnp.zeros((t, 1), jnp.int32)) >= topk, jnp.int32(0), int_min)

        def bit_step(b, base):
            cand = base | lax.shift_left(jnp.int32(1), 30 - b)
            return jnp.where(count_ge(cand) >= topk, cand, base)

        thr = lax.fori_loop(0, 31, bit_step, base)
        n_gt = count_ge(thr + 1)
        need = topk - n_gt
        tri = jnp.where(row < col, 1.0, 0.0).astype(BF16)

        def select_tile(j, seen):
            key = key_ref[j]
            eq = key == thr
            eq_b = jnp.where(eq, 1.0, 0.0).astype(BF16)
            rank = seen + _dot(eq_b, tri)
            sel = jnp.logical_or(key > thr, jnp.logical_and(eq, rank < need))
            sel = jnp.logical_and(sel, rel >= (j - i) * t)
            bias_ref[j] = jnp.where(sel, 0.0, NEG)
            return seen + jnp.sum(eq_b.astype(F32), axis=-1, keepdims=True)

        lax.fori_loop(0, i + 1, select_tile, jnp.zeros((t, 1), F32))

    qp = q_ref[0]
    qs = [jnp.where(lo if hh == 0 else jnp.logical_not(lo), qp, jnp.zeros_like(qp)) for hh in range(2)]
    slopes = [slope_ref[2 * m + hh] for hh in range(2)]

    def tile(j, carry):
        start = pl.multiple_of(j * t, t)
        k = k_ref[0, pl.ds(start, t), :]
        v = v_ref[0, pl.ds(start, t), :]
        ss = [_dot_t(qs[hh], k) for hh in range(2)]
        dist, bias = dist_ref[j], bias_ref[j]
        ss = [ss[hh] - slopes[hh] * dist + bias for hh in range(2)]
        return _flash_pair(ss, v, carry)

    init = (jnp.full((t, 1), NEG, F32), jnp.zeros((t, 1), F32), jnp.zeros((t, LANES), F32))
    carry = tile(i, (init, init))
    carry = lax.fori_loop(0, i, tile, carry)
    outs = [acc / l for (_, l, acc) in carry]
    o_ref[0] = jnp.where(lo, outs[0], outs[1]).astype(BF16)


def _dsa_attention(proj, wi, batch, seq, t=512):
    t = min(t, seq)
    nt = seq // t
    topk = min(IDX_TOPK, seq // 4)
    qw = DSA_HEADS * DSA_HEAD_DIM
    q0 = 0
    k0 = qw // LANES
    v0 = k0 + DSA_KV_HEADS
    qi0 = (v0 + DSA_KV_HEADS) * LANES // (IDX_HEADS * IDX_DIM)
    ki0 = v0 + DSA_KV_HEADS + IDX_HEADS * IDX_DIM // LANES
    pairs = DSA_HEADS // 2
    per_kv = pairs // DSA_KV_HEADS
    return pl.pallas_call(
        functools.partial(_dsa_kernel, t=t, topk=topk),
        grid=(batch, nt, pairs),
        in_specs=[pl.BlockSpec(memory_space=pltpu.SMEM),
                  pl.BlockSpec((1, t, IDX_HEADS * IDX_DIM), lambda b, i, m: (b, i, qi0)),
                  pl.BlockSpec((1, seq, LANES), lambda b, i, m: (b, 0, ki0)),
                  pl.BlockSpec((1, t, LANES), lambda b, i, m: (b, i, 0)),
                  pl.BlockSpec((1, t, LANES), lambda b, i, m: (b, i, q0 + m)),
                  pl.BlockSpec((1, seq, LANES), lambda b, i, m: (b, 0, k0 + m // per_kv)),
                  pl.BlockSpec((1, seq, LANES), lambda b, i, m: (b, 0, v0 + m // per_kv))],
        out_specs=pl.BlockSpec((1, t, LANES), lambda b, i, m: (b, i, m)),
        out_shape=jax.ShapeDtypeStruct((batch, seq, qw), BF16),
        scratch_shapes=[pltpu.VMEM((nt, t, t), jnp.int32), pltpu.VMEM((nt, t, t), F32),
                        pltpu.VMEM((nt, t, t), F32)],
        compiler_params=_params("parallel", "arbitrary", "arbitrary"),
        name="dsa_attention",
    )(jnp.asarray([s * LOG2E for s in _alibi_slopes(DSA_HEADS)], F32), proj, proj, wi, proj, proj, proj)


def _sb_kernel(q_ref, k_ref, v_ref, o_ref, *, tq, tk):
    i = pl.program_id(2)
    r = tq // tk
    lo = _lane_lo()
    rel = _tile_rel(tq, tk)
    suffix = jnp.where(_tile_rel(tk, tk) >= 0, 1.0, 0.0).astype(BF16)
    qp = q_ref[0]
    qs = [jnp.where(lo if hh == 0 else jnp.logical_not(lo), qp, jnp.zeros_like(qp)) for hh in range(2)]

    def tile(j, diag, carry):
        start = pl.multiple_of(j * tk, tk)
        k = k_ref[0, pl.ds(start, tk), :]
        v = v_ref[0, pl.ds(start, tk), :]
        hs = range(2)
        zs = [_dot_t(qs[hh], k) for hh in hs]
        lgs = []
        for z in zs:
            nz = -z
            lgs.append(jnp.minimum(nz, 0.0) - jnp.log2(1.0 + jnp.exp2(jnp.minimum(z, nz))))
        if diag is not None:
            strict = rel > diag * tk
            lgs = [jnp.where(strict, lg, 0.0) for lg in lgs]
        lbs = [lg.astype(BF16) for lg in lgs]
        runs = [_dot(lb, suffix) for lb in lbs]
        a_s = [jnp.exp2(zs[hh] + runs[hh] + carry[hh][0]) for hh in hs]
        if diag is not None:
            a_s = [jnp.where(strict, a, 0.0) for a in a_s]
        accs = [carry[hh][1] + _dot(a_s[hh].astype(BF16), v) for hh in hs]
        laters = [carry[hh][0] + runs[hh][:, 0:1] for hh in hs]
        return tuple((laters[hh], accs[hh]) for hh in hs)

    init = (jnp.zeros((tq, 1), F32), jnp.zeros((tq, LANES), F32))
    carry = (init, init)
    for d in reversed(range(r)):
        carry = tile(i * r + d, d, carry)
    carry = lax.fori_loop(0, i * r, lambda jj, c: tile(i * r - 1 - jj, None, c), carry)
    o_ref[0] = jnp.where(lo, carry[0][1], carry[1][1]).astype(BF16)


def _sb_attention(qkv, batch, seq, tq=512, tk=256):
    tq, tk = min(tq, seq), min(tk, seq)
    w = SB_HEADS * SB_HEAD_DIM
    pairs = SB_HEADS // 2
    return pl.pallas_call(
        functools.partial(_sb_kernel, tq=tq, tk=tk),
        grid=(batch, pairs, seq // tq),
        in_specs=[pl.BlockSpec((1, tq, LANES), lambda b, m, i: (b, i, m)),
                  pl.BlockSpec((1, seq, LANES), lambda b, m, i: (b, 0, pairs + m)),
                  pl.BlockSpec((1, seq, LANES), lambda b, m, i: (b, 0, 2 * pairs + m))],
        out_specs=pl.BlockSpec((1, tq, LANES), lambda b, m, i: (b, i, m)),
        out_shape=jax.ShapeDtypeStruct((batch, seq, w), BF16),
        compiler_params=_params("parallel", "parallel", "arbitrary"),
        name="sb_attention",
    )(qkv, qkv, qkv)


def _dup_heads(w, n_heads, dim):
    d = w.shape[0]
    w = w.reshape(d, n_heads, 1, dim)
    return jnp.broadcast_to(w, (d, n_heads, 2, dim)).reshape(d, n_heads * 2 * dim)


def _pad_heads(w, n_heads, dim, to):
    d = w.shape[0]
    w = w.reshape(d, n_heads, dim)
    return jnp.pad(w, ((0, 0), (0, 0), (0, to - dim))).reshape(d, n_heads * to)


def _swa_w_in(w):
    qw = SWA_HEADS * SWA_HEAD_DIM
    kw = SWA_KV_HEADS * SWA_HEAD_DIM
    q, k, v = w[:, :qw], w[:, qw:qw + kw], w[:, qw + kw:]
    return jnp.concatenate([q, _dup_heads(k, SWA_KV_HEADS, SWA_HEAD_DIM),
                            _dup_heads(v, SWA_KV_HEADS, SWA_HEAD_DIM)], axis=1).astype(BF16)


def _mla_w_in(w):
    d = w.shape[0]
    lat = MLA_Q_LORA + MLA_KV_LORA
    z = lambda c: jnp.zeros((d, c), w.dtype)
    return jnp.concatenate([w[:, :lat], z(MLA_NOPE), w[:, lat:], z(LANES - MLA_NOPE - MLA_ROPE)],
                           axis=1).astype(BF16)


def _sb_w_in(w):
    qw = SB_HEADS * SB_HEAD_DIM
    return jnp.concatenate([w[:, :qw] * (LOG2E * SB_HEAD_DIM ** -0.5), w[:, qw:]], axis=1).astype(BF16)


def _dsa_w_in(w):
    qw = DSA_HEADS * DSA_HEAD_DIM
    kw = DSA_KV_HEADS * DSA_HEAD_DIM
    iw = IDX_HEADS * IDX_DIM
    c = np.cumsum([qw, kw, kw, iw, IDX_DIM]).tolist()
    q, k, v, qi, ki, wi = (w[:, :c[0]], w[:, c[0]:c[1]], w[:, c[1]:c[2]], w[:, c[2]:c[3]],
                           w[:, c[3]:c[4]], w[:, c[4]:])
    q = q * (LOG2E * DSA_HEAD_DIM ** -0.5)
    main = jnp.concatenate([q, _dup_heads(k, DSA_KV_HEADS, DSA_HEAD_DIM),
                            _dup_heads(v, DSA_KV_HEADS, DSA_HEAD_DIM), qi, _dup_heads(ki, 1, IDX_DIM)],
                           axis=1).astype(BF16)
    wi = jnp.pad(wi, ((0, 0), (0, LANES - IDX_HEADS))).astype(BF16)
    return main, wi


def kernel(x, ffn1_w_gu, ffn1_w_down, ffn2_w_gu, ffn2_w_down, ln_g, ln_b, a_w_in, a_sinks, a_w_o,
           b_w_in, b_q_norm, b_w_uq, b_kv_norm, b_w_ukv, b_w_o, c_w_in, c_w_o, d_w_in, d_w_o):
    batch, seq, d = x.shape
    n = batch * seq
    x = x.reshape(n, d)
    for i in range(DEPTH):
        mixer, j = i % N_MIXERS, i // N_MIXERS
        x = _ffn_ln(x, ffn1_w_gu[i].astype(BF16), ffn1_w_down[i].astype(BF16), ln_g[i, 0], ln_b[i, 0])
        if mixer == 0:
            qkv = _matmul(x, _swa_w_in(a_w_in[j]), BF16)
            o = _swa_attention(qkv.reshape(batch, seq, -1), a_sinks[j], batch, seq)
            w_o = a_w_o[j]
        elif mixer == 1:
            c = _matmul(x, _mla_w_in(b_w_in[j]), F32)
            hd = MLA_NOPE + MLA_ROPE
            wq = _pad_heads(b_w_uq[j], MLA_HEADS, hd, LANES).astype(BF16)
            wkv = b_w_ukv[j].reshape(MLA_KV_LORA, MLA_HEADS, MLA_NOPE + MLA_V)
            wk = _pad_heads(wkv[:, :, :MLA_NOPE].reshape(MLA_KV_LORA, -1), MLA_HEADS, MLA_NOPE,
                            LANES).astype(BF16)
            wv = wkv[:, :, MLA_NOPE:].reshape(MLA_KV_LORA, -1).astype(BF16)
            q, k, v = _mla_up(c, b_q_norm[j], b_kv_norm[j], wq, wk, wv, seq)
            o = _mla_attention(q.reshape(batch, seq, -1), k.reshape(batch, seq, -1),
                               v.reshape(batch, seq, -1))
            w_o = b_w_o[j]
        elif mixer == 2:
            w_main, w_wi = _dsa_w_in(c_w_in[j])
            proj = _matmul(x, w_main, BF16)
            wi = _matmul(x, w_wi, F32)
            o = _dsa_attention(proj.reshape(batch, seq, -1), wi.reshape(batch, seq, -1), batch, seq)
            w_o = c_w_o[j]
        else:
            qkv = _matmul(x, _sb_w_in(d_w_in[j]), BF16)
            o = _sb_attention(qkv.reshape(batch, seq, -1), batch, seq)
            w_o = d_w_o[j]
        x = _proj_ln(o.reshape(n, -1), w_o.astype(BF16), x, ln_g[i, 1], ln_b[i, 1])
        x = _ffn_ln(x, ffn2_w_gu[i].astype(BF16), ffn2_w_down[i].astype(BF16), ln_g[i, 2], ln_b[i, 2])
    return x.reshape(batch, seq, d)
```

```python
import functools
import math

import numpy as np
import jax
import jax.numpy as jnp
from jax import lax
from jax.experimental import pallas as pl
from jax.experimental.pallas import tpu as pltpu

D_MODEL = 1024
DEPTH = 4
N_MIXERS = 4
D_FF = 2816
LN_EPS = 1e-5
RMS_EPS = 1e-6
NEG = -1e30

SWA_HEADS, SWA_KV_HEADS, SWA_HEAD_DIM, SWA_BLOCK = 16, 4, 64, 128
MLA_HEADS, MLA_Q_LORA, MLA_KV_LORA, MLA_NOPE, MLA_ROPE, MLA_V = 16, 768, 256, 64, 32, 64
ROPE_THETA = 10000.0
DSA_HEADS, DSA_KV_HEADS, DSA_HEAD_DIM = 16, 4, 64
IDX_HEADS, IDX_DIM, IDX_TOPK = 8, 64, 256
SB_HEADS, SB_HEAD_DIM = 16, 64

ALPHA = (2.0 * DEPTH) ** 0.25
LOG2E = math.log2(math.e)

LANES = 128
HALF = 64
VMEM_LIMIT = 56 * 1024 * 1024
BF16 = jnp.bfloat16
F32 = jnp.float32


def _params(*sem, flags=None):
    return pltpu.CompilerParams(dimension_semantics=sem, vmem_limit_bytes=VMEM_LIMIT, flags=flags)


def _dot(a, b):
    return jnp.dot(a, b, preferred_element_type=F32)


def _dot_t(a, b):
    return lax.dot_general(a, b, (((1,), (1,)), ((), ())), preferred_element_type=F32)


def _alibi_slopes(n_heads):
    return [2.0 ** (-8.0 * (i + 1) / n_heads) for i in range(n_heads)]


def _layer_norm(y, g, b):
    mu = jnp.mean(y, axis=-1, keepdims=True)
    yc = y - mu
    var = jnp.mean(yc * yc, axis=-1, keepdims=True)
    return yc * lax.rsqrt(var + LN_EPS) * g + b


def _lane_lo():
    return lax.broadcasted_iota(jnp.int32, (1, LANES), 1) < HALF


def _mm_kernel(x_ref, w_ref, o_ref):
    o_ref[...] = _dot(x_ref[...].astype(BF16), w_ref[...]).astype(o_ref.dtype)


def _resident(shape):
    return pl.BlockSpec(shape, lambda i: (0,) * len(shape), pipeline_mode=pl.Buffered(1))


def _matmul(x, w, out_dtype, tm=512):
    n, k = x.shape
    m = w.shape[1]
    tm = min(tm, n)
    return pl.pallas_call(
        _mm_kernel,
        grid=(n // tm,),
        in_specs=[pl.BlockSpec((tm, k), lambda i: (i, 0)), _resident(w.shape)],
        out_specs=pl.BlockSpec((tm, m), lambda i: (i, 0)),
        out_shape=jax.ShapeDtypeStruct((n, m), out_dtype),
        compiler_params=_params("parallel"),
        name="proj_matmul",
    )(x, w)


def _proj_ln_kernel(o_ref, w_ref, x_ref, g_ref, b_ref, out_ref):
    y = ALPHA * x_ref[...] + _dot(o_ref[...], w_ref[...])
    out_ref[...] = _layer_norm(y, g_ref[...], b_ref[...])


def _proj_ln(o, w, x, g, b, tm=512):
    n, k = o.shape
    d = w.shape[1]
    tm = min(tm, n)
    return pl.pallas_call(
        _proj_ln_kernel,
        grid=(n // tm,),
        in_specs=[pl.BlockSpec((tm, k), lambda i: (i, 0)),
                  _resident((k, d)),
                  pl.BlockSpec((tm, d), lambda i: (i, 0)),
                  _resident((1, d)), _resident((1, d))],
        out_specs=pl.BlockSpec((tm, d), lambda i: (i, 0)),
        out_shape=jax.ShapeDtypeStruct((n, d), F32),
        compiler_params=_params("parallel"),
        name="out_proj_ln",
    )(o, w, x, g.reshape(1, d), b.reshape(1, d))


def _ffn_kernel(x_ref, wgu_ref, wd_ref, g_ref, b_ref, o_ref, *, f):
    x = x_ref[...]
    xb = x.astype(BF16)
    h = _dot(xb, wgu_ref[:, :f])
    u = _dot(xb, wgu_ref[:, f:])
    a = h * (1.0 / (1.0 + jnp.exp(-h))) * u
    y = ALPHA * x + 0.5 * _dot(a.astype(BF16), wd_ref[...])
    o_ref[...] = _layer_norm(y, g_ref[...], b_ref[...])


def _ffn_ln(x, w_gu, w_down, g, b, tm=512):
    n, d = x.shape
    f = w_down.shape[0]
    tm = min(tm, n)
    return pl.pallas_call(
        functools.partial(_ffn_kernel, f=f),
        grid=(n // tm,),
        in_specs=[pl.BlockSpec((tm, d), lambda i: (i, 0)),
                  _resident(w_gu.shape), _resident(w_down.shape), _resident((1, d)), _resident((1, d))],
        out_specs=pl.BlockSpec((tm, d), lambda i: (i, 0)),
        out_shape=jax.ShapeDtypeStruct((n, d), F32),
        compiler_params=_params("parallel"),
        name="ffn_ln",
    )(x, w_gu, w_down, g.reshape(1, d), b.reshape(1, d))


def _swa_kernel(sink_ref, q_ref, kc_ref, kp_ref, vc_ref, vp_ref, o_ref, *, tq):
    i = pl.program_id(1)
    t = SWA_BLOCK
    rep = SWA_HEADS // SWA_KV_HEADS
    kvs = range(SWA_KV_HEADS)
    lo = _lane_lo()
    rel = _tile_rel(t, 2 * t) + t
    col = lax.broadcasted_iota(jnp.int32, (t, 2 * t), 1)
    dist = rel.astype(F32)
    in_band = jnp.logical_and(rel >= 0, rel < SWA_BLOCK)
    slopes = [s * LOG2E for s in _alibi_slopes(SWA_HEADS)]
    lane = lambda g: slice(g * LANES, (g + 1) * LANES)
    for qb in range(tq // t):
        r0 = qb * t
        if qb == 0:
            ks = [jnp.concatenate([kp_ref[0, :, lane(g)], kc_ref[0, 0:t, lane(g)]], axis=0) for g in kvs]
            vs = [jnp.concatenate([vp_ref[0, :, lane(g)], vc_ref[0, 0:t, lane(g)]], axis=0) for g in kvs]
            valid = jnp.logical_and(in_band, col >= jnp.where(i == 0, t, 0))
        else:
            ks = [kc_ref[0, r0 - t:r0 + t, lane(g)] for g in kvs]
            vs = [vc_ref[0, r0 - t:r0 + t, lane(g)] for g in kvs]
            valid = in_band
        qst = []
        for g in kvs:
            parts = []
            for r in range(rep):
                h = g * rep + r
                qp = q_ref[0, r0:r0 + t, lane(h // 2)]
                parts.append(jnp.where(lo if h % 2 == 0 else jnp.logical_not(lo), qp, jnp.zeros_like(qp)))
            qst.append(jnp.concatenate(parts, axis=0))
        ss = [_dot_t(qst[g], ks[g]) for g in kvs]
        ps, dens = [], []
        for g in kvs:
            prow, drow = [], []
            for r in range(rep):
                h = g * rep + r
                s = jnp.where(valid, ss[g][r * t:(r + 1) * t] - slopes[h] * dist, NEG)
                sink = sink_ref[h]
                mx = jnp.maximum(jnp.max(s, axis=-1, keepdims=True), sink)
                p = jnp.exp2(s - mx)
                drow.append(jnp.sum(p, axis=-1, keepdims=True) + jnp.exp2(sink - mx))
                prow.append(p.astype(BF16))
            ps.append(jnp.concatenate(prow, axis=0))
            dens.append(drow)
        outs = [_dot(ps[g], vs[g]) for g in kvs]
        for g in kvs:
            for r in range(0, rep, 2):
                even = outs[g][r * t:(r + 1) * t] / dens[g][r]
                odd = outs[g][(r + 1) * t:(r + 2) * t] / dens[g][r + 1]
                o_ref[0, r0:r0 + t, lane((g * rep + r) // 2)] = jnp.where(lo, even, odd).astype(BF16)


def _swa_attention(qkv, sinks, batch, seq, tq=512):
    t = SWA_BLOCK
    tq = min(tq, seq)
    sub = tq // t
    qw = SWA_HEADS * SWA_HEAD_DIM
    kw = SWA_KV_HEADS * LANES
    qblk, kblk, vblk = 0, qw // kw, qw // kw + 1
    prev = lambda i: jnp.maximum(i * sub - 1, 0)
    return pl.pallas_call(
        functools.partial(_swa_kernel, tq=tq),
        grid=(batch, seq // tq),
        in_specs=[pl.BlockSpec(memory_space=pltpu.SMEM),
                  pl.BlockSpec((1, tq, qw), lambda b, i: (b, i, qblk)),
                  pl.BlockSpec((1, tq, kw), lambda b, i: (b, i, kblk)),
                  pl.BlockSpec((1, t, kw), lambda b, i: (b, prev(i), kblk)),
                  pl.BlockSpec((1, tq, kw), lambda b, i: (b, i, vblk)),
                  pl.BlockSpec((1, t, kw), lambda b, i: (b, prev(i), vblk))],
        out_specs=pl.BlockSpec((1, tq, qw), lambda b, i: (b, i, 0)),
        out_shape=jax.ShapeDtypeStruct((batch, seq, qw), BF16),
        compiler_params=_params("parallel", "arbitrary"),
        name="swa_attention",
    )(sinks, qkv, qkv, qkv, qkv, qkv)


def _mla_up_kernel(c_ref, gq_ref, gkv_ref, wq_ref, wk_ref, wv_ref, cq_ref, s1q_ref, s2q_ref,
                   ck_ref, s1k_ref, s2k_ref, q_ref, k_ref, v_ref):
    c = c_ref[...]
    cq = c[:, :MLA_Q_LORA]
    ckv = c[:, MLA_Q_LORA:MLA_Q_LORA + MLA_KV_LORA]
    kr = c[:, MLA_Q_LORA + MLA_KV_LORA:]
    cq = cq * lax.rsqrt(jnp.mean(cq * cq, axis=-1, keepdims=True) + RMS_EPS) * gq_ref[...]
    ckv = ckv * lax.rsqrt(jnp.mean(ckv * ckv, axis=-1, keepdims=True) + RMS_EPS) * gkv_ref[...]
    ckv_b = ckv.astype(BF16)
    q = _dot(cq.astype(BF16), wq_ref[...])
    kn = _dot(ckv_b, wk_ref[...])
    v_ref[...] = _dot(ckv_b, wv_ref[...]).astype(BF16)

    def rot(xh, cc, s1, s2):
        return xh * cc + pltpu.roll(xh, 16, 1) * s1 + pltpu.roll(xh, LANES - 16, 1) * s2

    kr = rot(kr, ck_ref[...], s1k_ref[...], s2k_ref[...])
    cq_t, s1q, s2q = cq_ref[...], s1q_ref[...], s2q_ref[...]
    for h in range(MLA_HEADS):
        sl = slice(h * LANES, (h + 1) * LANES)
        q_ref[:, sl] = rot(q[:, sl], cq_t, s1q, s2q).astype(BF16)
        k_ref[:, sl] = (kn[:, sl] + kr).astype(BF16)


def _mla_rope_tables(seq, scale):
    half = MLA_ROPE // 2
    inv = ROPE_THETA ** (-jnp.arange(0, MLA_ROPE, 2, dtype=F32) / MLA_ROPE)
    ang = jnp.arange(seq, dtype=F32)[:, None] * inv[None, :]
    cos, sin = jnp.cos(ang), jnp.sin(ang)
    z = lambda w: jnp.zeros((seq, w), F32)
    one = jnp.ones((seq, MLA_NOPE), F32)
    cc = jnp.concatenate([one, cos, cos, z(LANES - MLA_NOPE - MLA_ROPE)], axis=1) * scale
    s1 = jnp.concatenate([z(MLA_NOPE + half), sin, z(LANES - MLA_NOPE - MLA_ROPE)], axis=1) * scale
    s2 = jnp.concatenate([z(MLA_NOPE), -sin, z(LANES - MLA_NOPE - half)], axis=1) * scale
    return cc, s1, s2


def _mla_up(c, gq, gkv, wq, wk, wv, seq, tm=512):
    n, cw = c.shape
    tm = min(tm, seq)
    spt = seq // tm
    scale = LOG2E * (MLA_NOPE + MLA_ROPE) ** -0.5
    tq = _mla_rope_tables(seq, scale)
    tk = _mla_rope_tables(seq, 1.0)
    qk_w = MLA_HEADS * LANES
    v_w = MLA_HEADS * MLA_V
    full = lambda a: _resident(a.shape)
    tab = pl.BlockSpec((tm, LANES), lambda i: (i % spt, 0))
    gq2, gkv2 = gq.reshape(1, -1), gkv.reshape(1, -1)
    return pl.pallas_call(
        _mla_up_kernel,
        grid=(n // tm,),
        in_specs=[pl.BlockSpec((tm, cw), lambda i: (i, 0)), full(gq2), full(gkv2), full(wq), full(wk),
                  full(wv), tab, tab, tab, tab, tab, tab],
        out_specs=[pl.BlockSpec((tm, qk_w), lambda i: (i, 0)),
                   pl.BlockSpec((tm, qk_w), lambda i: (i, 0)),
                   pl.BlockSpec((tm, v_w), lambda i: (i, 0))],
        out_shape=[jax.ShapeDtypeStruct((n, qk_w), BF16), jax.ShapeDtypeStruct((n, qk_w), BF16),
                   jax.ShapeDtypeStruct((n, v_w), BF16)],
        compiler_params=_params("parallel"),
        name="mla_up_proj",
    )(c, gq2, gkv2, wq, wk, wv, *tq, *tk)


def _flash_pair(ss, v, carry):
    hs = range(2)
    ms = [jnp.maximum(carry[hh][0], jnp.max(ss[hh], axis=-1, keepdims=True)) for hh in hs]
    ps = [jnp.exp2(ss[hh] - ms[hh]) for hh in hs]
    al = [jnp.exp2(carry[hh][0] - ms[hh]) for hh in hs]
    ls = [al[hh] * carry[hh][1] + jnp.sum(ps[hh], axis=-1, keepdims=True) for hh in hs]
    accs = [al[hh] * carry[hh][2] + _dot(ps[hh].astype(BF16), v) for hh in hs]
    return tuple((ms[hh], ls[hh], accs[hh]) for hh in hs)


def _tile_rel(tq, tk):
    return (lax.broadcasted_iota(jnp.int32, (tq, tk), 0) - lax.broadcasted_iota(jnp.int32, (tq, tk), 1))


def _mla_attn_kernel(q_ref, k_ref, v_ref, o_ref, *, tq, tk):
    i = pl.program_id(2)
    r = tq // tk
    rel = _tile_rel(tq, tk)
    sls = [slice(hh * LANES, (hh + 1) * LANES) for hh in range(2)]
    qs = [q_ref[0, :, sl] for sl in sls]

    def tile(j, diag, carry):
        start = pl.multiple_of(j * tk, tk)
        v = v_ref[0, pl.ds(start, tk), :]
        ss = [_dot_t(qs[hh], k_ref[0, pl.ds(start, tk), sls[hh]]) for hh in range(2)]
        if diag is not None:
            ss = [jnp.where(rel >= diag * tk, s, NEG) for s in ss]
        return _flash_pair(ss, v, carry)

    init = (jnp.full((tq, 1), NEG, F32), jnp.zeros((tq, 1), F32), jnp.zeros((tq, LANES), F32))
    carry = (init, init)
    for d in range(r):
        carry = tile(i * r + d, d, carry)
    carry = lax.fori_loop(0, i * r, lambda j, c: tile(j, None, c), carry)
    outs = [acc / l for (_, l, acc) in carry]
    o_ref[0] = jnp.where(_lane_lo(), outs[0], outs[1]).astype(BF16)


def _mla_attention(q, k, v, tq=512, tk=512):
    batch, seq, _ = q.shape
    tq, tk = min(tq, seq), min(tk, seq)
    return pl.pallas_call(
        functools.partial(_mla_attn_kernel, tq=tq, tk=tk),
        grid=(batch, MLA_HEADS // 2, seq // tq),
        in_specs=[pl.BlockSpec((1, tq, 2 * LANES), lambda b, m, i: (b, i, m)),
                  pl.BlockSpec((1, seq, 2 * LANES), lambda b, m, i: (b, 0, m)),
                  pl.BlockSpec((1, seq, LANES), lambda b, m, i: (b, 0, m))],
        out_specs=pl.BlockSpec((1, tq, LANES), lambda b, m, i: (b, i, m)),
        out_shape=jax.ShapeDtypeStruct(v.shape, BF16),
        compiler_params=_params("parallel", "parallel", "arbitrary"),
        name="mla_attention",
    )(q, k, v)


def _order_key(x):
    bits = lax.bitcast_convert_type(x, jnp.int32)
    return bits ^ (lax.shift_right_arithmetic(bits, 31) & 0x7FFFFFFF)


def _dsa_kernel(slope_ref, qi_ref, ki_ref, wi_ref, q_ref, k_ref, v_ref, o_ref, key_ref, bias_ref,
                dist_ref, *, t, topk):
    i = pl.program_id(1)
    m = pl.program_id(2)
    lo = _lane_lo()
    row = lax.broadcasted_iota(jnp.int32, (t, t), 0)
    col = lax.broadcasted_iota(jnp.int32, (t, t), 1)
    rel = row - col

    @pl.when(m == 0)
    def _select():
        wi = wi_ref[0] * (IDX_HEADS ** -0.5)

        def score_tile(j, carry):
            start = pl.multiple_of(j * t, t)
            ki = ki_ref[0, pl.ds(start, t), :]
            sc = jnp.zeros((t, t), F32)
            for p in range(IDX_HEADS // 2):
                qp = qi_ref[0, :, p * LANES:(p + 1) * LANES]
                for hh in range(2):
                    h = 2 * p + hh
                    qm = jnp.where(lo if hh == 0 else jnp.logical_not(lo), qp, jnp.zeros_like(qp))
                    dots = _dot_t(qm, ki) * (IDX_DIM ** -0.5)
                    sc = sc + wi[:, h:h + 1] * jnp.maximum(dots, 0.0)
            sc = sc + 0.0
            sc = jnp.where(rel >= (j - i) * t, sc, NEG)
            key_ref[j] = _order_key(sc)
            dist_ref[j] = ((i - j) * t + rel).astype(F32)
            return carry

        lax.fori_loop(0, i + 1, score_tile, 0)

        def count_ge(cand):
            def body(j, part):
                ge = jnp.where(key_ref[j] >= cand, 1.0, 0.0)
                for c0 in range(0, t, LANES):
                    part = part + ge[:, c0:c0 + LANES]
                return part
            part = lax.fori_loop(0, i + 1, body, jnp.zeros((t, LANES), F32))
            return jnp.sum(part, axis=-1, keepdims=True)

        int_min = jnp.int32(-2 ** 31)
        base = jnp.where(count_ge(jnp.zeros((t, 1), jnp.int32)) >= topk, jnp.int32(0), int_min)

        def bit_step(b, base):
            cand = base | lax.shift_left(jnp.int32(1), 30 - b)
            return jnp.where(count_ge(cand) >= topk, cand, base)

        thr = lax.fori_loop(0, 31, bit_step, base)
        n_gt = count_ge(thr + 1)
        need = topk - n_gt
        tri = jnp.where(row < col, 1.0, 0.0).astype(BF16)

        def select_tile(j, seen):
            key = key_ref[j]
            eq = key == thr
            eq_b = jnp.where(eq, 1.0, 0.0).astype(BF16)
            rank = seen + _dot(eq_b, tri)
            sel = jnp.logical_or(key > thr, jnp.logical_and(eq, rank < need))
            sel = jnp.logical_and(sel, rel >= (j - i) * t)
            bias_ref[j] = jnp.where(sel, 0.0, NEG)
            return seen + jnp.sum(eq_b.astype(F32), axis=-1, keepdims=True)

        lax.fori_loop(0, i + 1, select_tile, jnp.zeros((t, 1), F32))

    qp = q_ref[0]
    qs = [jnp.where(lo if hh == 0 else jnp.logical_not(lo), qp, jnp.zeros_like(qp)) for hh in range(2)]
    slopes = [slope_ref[2 * m + hh] for hh in range(2)]

    def tile(j, carry):
        start = pl.multiple_of(j * t, t)
        k = k_ref[0, pl.ds(start, t), :]
        v = v_ref[0, pl.ds(start, t), :]
        ss = [_dot_t(qs[hh], k) for hh in range(2)]
        dist, bias = dist_ref[j], bias_ref[j]
        ss = [ss[hh] - slopes[hh] * dist + bias for hh in range(2)]
        return _flash_pair(ss, v, carry)

    init = (jnp.full((t, 1), NEG, F32), jnp.zeros((t, 1), F32), jnp.zeros((t, LANES), F32))
    carry = tile(i, (init, init))
    carry = lax.fori_loop(0, i, tile, carry)
    outs = [acc / l for (_, l, acc) in carry]
    o_ref[0] = jnp.where(lo, outs[0], outs[1]).astype(BF16)


def _dsa_attention(proj, wi, batch, seq, t=512):
    t = min(t, seq)
    nt = seq // t
    topk = min(IDX_TOPK, seq // 4)
    qw = DSA_HEADS * DSA_HEAD_DIM
    q0 = 0
    k0 = qw // LANES
    v0 = k0 + DSA_KV_HEADS
    qi0 = (v0 + DSA_KV_HEADS) * LANES // (IDX_HEADS * IDX_DIM)
    ki0 = v0 + DSA_KV_HEADS + IDX_HEADS * IDX_DIM // LANES
    pairs = DSA_HEADS // 2
    per_kv = pairs // DSA_KV_HEADS
    return pl.pallas_call(
        functools.partial(_dsa_kernel, t=t, topk=topk),
        grid=(batch, nt, pairs),
        in_specs=[pl.BlockSpec(memory_space=pltpu.SMEM),
                  pl.BlockSpec((1, t, IDX_HEADS * IDX_DIM), lambda b, i, m: (b, i, qi0)),
                  pl.BlockSpec((1, seq, LANES), lambda b, i, m: (b, 0, ki0)),
                  pl.BlockSpec((1, t, LANES), lambda b, i, m: (b, i, 0)),
                  pl.BlockSpec((1, t, LANES), lambda b, i, m: (b, i, q0 + m)),
                  pl.BlockSpec((1, seq, LANES), lambda b, i, m: (b, 0, k0 + m // per_kv)),
                  pl.BlockSpec((1, seq, LANES), lambda b, i, m: (b, 0, v0 + m // per_kv))],
        out_specs=pl.BlockSpec((1, t, LANES), lambda b, i, m: (b, i, m)),
        out_shape=jax.ShapeDtypeStruct((batch, seq, qw), BF16),
        scratch_shapes=[pltpu.VMEM((nt, t, t), jnp.int32), pltpu.VMEM((nt, t, t), F32),
                        pltpu.VMEM((nt, t, t), F32)],
        compiler_params=_params("parallel", "arbitrary", "arbitrary"),
        name="dsa_attention",
    )(jnp.asarray([s * LOG2E for s in _alibi_slopes(DSA_HEADS)], F32), proj, proj, wi, proj, proj, proj)


def _sb_kernel(q_ref, k_ref, v_ref, o_ref, *, tq, tk):
    i = pl.program_id(2)
    r = tq // tk
    lo = _lane_lo()
    rel = _tile_rel(tq, tk)
    suffix = jnp.where(_tile_rel(tk, tk) >= 0, 1.0, 0.0).astype(BF16)
    qp = q_ref[0]
    qs = [jnp.where(lo if hh == 0 else jnp.logical_not(lo), qp, jnp.zeros_like(qp)) for hh in range(2)]

    def tile(j, diag, carry):
        start = pl.multiple_of(j * tk, tk)
        k = k_ref[0, pl.ds(start, tk), :]
        v = v_ref[0, pl.ds(start, tk), :]
        hs = range(2)
        zs = [_dot_t(qs[hh], k) for hh in hs]
        lgs = []
        for z in zs:
            nz = -z
            lgs.append(jnp.minimum(nz, 0.0) - jnp.log2(1.0 + jnp.exp2(jnp.minimum(z, nz))))
        if diag is not None:
            strict = rel > diag * tk
            lgs = [jnp.where(strict, lg, 0.0) for lg in lgs]
        lbs = [lg.astype(BF16) for lg in lgs]
        runs = [_dot(lb, suffix) for lb in lbs]
        a_s = [jnp.exp2(zs[hh] + runs[hh] + carry[hh][0]) for hh in hs]
        if diag is not None:
            a_s = [jnp.where(strict, a, 0.0) for a in a_s]
        accs = [carry[hh][1] + _dot(a_s[hh].astype(BF16), v) for hh in hs]
        laters = [carry[hh][0] + runs[hh][:, 0:1] for hh in hs]
        return tuple((laters[hh], accs[hh]) for hh in hs)

    init = (jnp.zeros((tq, 1), F32), jnp.zeros((tq, LANES), F32))
    carry = (init, init)
    for d in reversed(range(r)):
        carry = tile(i * r + d, d, carry)
    carry = lax.fori_loop(0, i * r, lambda jj, c: tile(i * r - 1 - jj, None, c), carry)
    o_ref[0] = jnp.where(lo, carry[0][1], carry[1][1]).astype(BF16)


def _sb_attention(qkv, batch, seq, tq=512, tk=256):
    tq, tk = min(tq, seq), min(tk, seq)
    w = SB_HEADS * SB_HEAD_DIM
    pairs = SB_HEADS // 2
    return pl.pallas_call(
        functools.partial(_sb_kernel, tq=tq, tk=tk),
        grid=(batch, pairs, seq // tq),
        in_specs=[pl.BlockSpec((1, tq, LANES), lambda b, m, i: (b, i, m)),
                  pl.BlockSpec((1, seq, LANES), lambda b, m, i: (b, 0, pairs + m)),
                  pl.BlockSpec((1, seq, LANES), lambda b, m, i: (b, 0, 2 * pairs + m))],
        out_specs=pl.BlockSpec((1, tq, LANES), lambda b, m, i: (b, i, m)),
        out_shape=jax.ShapeDtypeStruct((batch, seq, w), BF16),
        compiler_params=_params("parallel", "parallel", "arbitrary"),
        name="sb_attention",
    )(qkv, qkv, qkv)


def _dup_heads(w, n_heads, dim):
    d = w.shape[0]
    w = w.reshape(d, n_heads, 1, dim)
    return jnp.broadcast_to(w, (d, n_heads, 2, dim)).reshape(d, n_heads * 2 * dim)


def _pad_heads(w, n_heads, dim, to):
    d = w.shape[0]
    w = w.reshape(d, n_heads, dim)
    return jnp.pad(w, ((0, 0), (0, 0), (0, to - dim))).reshape(d, n_heads * to)


def _swa_w_in(w):
    qw = SWA_HEADS * SWA_HEAD_DIM
    kw = SWA_KV_HEADS * SWA_HEAD_DIM
    q, k, v = w[:, :qw], w[:, qw:qw + kw], w[:, qw + kw:]
    q = q * (LOG2E * SWA_HEAD_DIM ** -0.5)
    return jnp.concatenate([q, _dup_heads(k, SWA_KV_HEADS, SWA_HEAD_DIM),
                            _dup_heads(v, SWA_KV_HEADS, SWA_HEAD_DIM)], axis=1).astype(BF16)


def _mla_w_in(w):
    d = w.shape[0]
    lat = MLA_Q_LORA + MLA_KV_LORA
    z = lambda c: jnp.zeros((d, c), w.dtype)
    return jnp.concatenate([w[:, :lat], z(MLA_NOPE), w[:, lat:], z(LANES - MLA_NOPE - MLA_ROPE)],
                           axis=1).astype(BF16)


def _sb_w_in(w):
    qw = SB_HEADS * SB_HEAD_DIM
    return jnp.concatenate([w[:, :qw] * (LOG2E * SB_HEAD_DIM ** -0.5), w[:, qw:]], axis=1).astype(BF16)


def _dsa_w_in(w):
    qw = DSA_HEADS * DSA_HEAD_DIM
    kw = DSA_KV_HEADS * DSA_HEAD_DIM
    iw = IDX_HEADS * IDX_DIM
    c = np.cumsum([qw, kw, kw, iw, IDX_DIM]).tolist()
    q, k, v, qi, ki, wi = (w[:, :c[0]], w[:, c[0]:c[1]], w[:, c[1]:c[2]], w[:, c[2]:c[3]],
                           w[:, c[3]:c[4]], w[:, c[4]:])
    q = q * (LOG2E * DSA_HEAD_DIM ** -0.5)
    main = jnp.concatenate([q, _dup_heads(k, DSA_KV_HEADS, DSA_HEAD_DIM),
                            _dup_heads(v, DSA_KV_HEADS, DSA_HEAD_DIM), qi, _dup_heads(ki, 1, IDX_DIM)],
                           axis=1).astype(BF16)
    wi = jnp.pad(wi, ((0, 0), (0, LANES - IDX_HEADS))).astype(BF16)
    return main, wi


def kernel(x, ffn1_w_gu, ffn1_w_down, ffn2_w_gu, ffn2_w_down, ln_g, ln_b, a_w_in, a_sinks, a_w_o,
           b_w_in, b_q_norm, b_w_uq, b_kv_norm, b_w_ukv, b_w_o, c_w_in, c_w_o, d_w_in, d_w_o):
    batch, seq, d = x.shape
    n = batch * seq
    x = x.reshape(n, d)
    for i in range(DEPTH):
        mixer, j = i % N_MIXERS, i // N_MIXERS
        x = _ffn_ln(x, ffn1_w_gu[i].astype(BF16), ffn1_w_down[i].astype(BF16), ln_g[i, 0], ln_b[i, 0])
        if mixer == 0:
            qkv = _matmul(x, _swa_w_in(a_w_in[j]), BF16)
            o = _swa_attention(qkv.reshape(batch, seq, -1), a_sinks[j] * LOG2E, batch, seq)
            w_o = a_w_o[j]
        elif mixer == 1:
            c = _matmul(x, _mla_w_in(b_w_in[j]), F32)
            hd = MLA_NOPE + MLA_ROPE
            wq = _pad_heads(b_w_uq[j], MLA_HEADS, hd, LANES).astype(BF16)
            wkv = b_w_ukv[j].reshape(MLA_KV_LORA, MLA_HEADS, MLA_NOPE + MLA_V)
            wk = _pad_heads(wkv[:, :, :MLA_NOPE].reshape(MLA_KV_LORA, -1), MLA_HEADS, MLA_NOPE,
                            LANES).astype(BF16)
            wv = wkv[:, :, MLA_NOPE:].reshape(MLA_KV_LORA, -1).astype(BF16)
            q, k, v = _mla_up(c, b_q_norm[j], b_kv_norm[j], wq, wk, wv, seq)
            o = _mla_attention(q.reshape(batch, seq, -1), k.reshape(batch, seq, -1),
                               v.reshape(batch, seq, -1))
            w_o = b_w_o[j]
        elif mixer == 2:
            w_main, w_wi = _dsa_w_in(c_w_in[j])
            proj = _matmul(x, w_main, BF16)
            wi = _matmul(x, w_wi, F32)
            o = _dsa_attention(proj.reshape(batch, seq, -1), wi.reshape(batch, seq, -1), batch, seq)
            w_o = c_w_o[j]
        else:
            qkv = _matmul(x, _sb_w_in(d_w_in[j]), BF16)
            o = _sb_attention(qkv.reshape(batch, seq, -1), batch, seq)
            w_o = d_w_o[j]
        x = _proj_ln(o.reshape(n, -1), w_o.astype(BF16), x, ln_g[i, 1], ln_b[i, 1])
        x = _ffn_ln(x, ffn2_w_gu[i].astype(BF16), ffn2_w_down[i].astype(BF16), ln_g[i, 2], ln_b[i, 2])
    return x.reshape(batch, seq, d)
```

```python
import functools
import math

import numpy as np
import jax
import jax.numpy as jnp
from jax import lax
from jax.experimental import pallas as pl
from jax.experimental.pallas import tpu as pltpu

D_MODEL = 1024
DEPTH = 4
N_MIXERS = 4
D_FF = 2816
LN_EPS = 1e-5
RMS_EPS = 1e-6
NEG = -1e30

SWA_HEADS, SWA_KV_HEADS, SWA_HEAD_DIM, SWA_BLOCK = 16, 4, 64, 128
MLA_HEADS, MLA_Q_LORA, MLA_KV_LORA, MLA_NOPE, MLA_ROPE, MLA_V = 16, 768, 256, 64, 32, 64
ROPE_THETA = 10000.0
DSA_HEADS, DSA_KV_HEADS, DSA_HEAD_DIM = 16, 4, 64
IDX_HEADS, IDX_DIM, IDX_TOPK = 8, 64, 256
SB_HEADS, SB_HEAD_DIM = 16, 64

ALPHA = (2.0 * DEPTH) ** 0.25
LOG2E = math.log2(math.e)

LANES = 128
HALF = 64
VMEM_LIMIT = 56 * 1024 * 1024
BF16 = jnp.bfloat16
F32 = jnp.float32


def _params(*sem, flags=None):
    return pltpu.CompilerParams(dimension_semantics=sem, vmem_limit_bytes=VMEM_LIMIT, flags=flags)


def _dot(a, b):
    return jnp.dot(a, b, preferred_element_type=F32)


def _dot_t(a, b):
    return lax.dot_general(a, b, (((1,), (1,)), ((), ())), preferred_element_type=F32)


def _alibi_slopes(n_heads):
    return [2.0 ** (-8.0 * (i + 1) / n_heads) for i in range(n_heads)]


def _layer_norm(y, g, b):
    mu = jnp.mean(y, axis=-1, keepdims=True)
    yc = y - mu
    var = jnp.mean(yc * yc, axis=-1, keepdims=True)
    return yc * lax.rsqrt(var + LN_EPS) * g + b


def _lane_lo():
    return lax.broadcasted_iota(jnp.int32, (1, LANES), 1) < HALF


def _resident(shape):
    return pl.BlockSpec(shape, lambda i: (0,) * len(shape), pipeline_mode=pl.Buffered(1))


def _rows(tm, width):
    return pl.BlockSpec((tm, width), lambda i: (i, 0))


def _ffn_ln_value(x, wgu_ref, wd_ref, g_ref, b_ref):
    f = wd_ref.shape[0]
    xb = x.astype(BF16)
    h = _dot(xb, wgu_ref[:, :f])
    u = _dot(xb, wgu_ref[:, f:])
    a = h * (1.0 / (1.0 + jnp.exp(-h))) * u
    y = ALPHA * x + 0.5 * _dot(a.astype(BF16), wd_ref[...])
    return _layer_norm(y, g_ref[...], b_ref[...])


def _ffn_proj_kernel(x_ref, wgu_ref, wd_ref, g_ref, b_ref, *refs, n_proj):
    w_refs, o_ref, p_refs = refs[:n_proj], refs[n_proj], refs[n_proj + 1:]
    y = _ffn_ln_value(x_ref[...], wgu_ref, wd_ref, g_ref, b_ref)
    o_ref[...] = y
    yb = y.astype(BF16)
    for w_ref, p_ref in zip(w_refs, p_refs):
        p_ref[...] = _dot(yb, w_ref[...]).astype(p_ref.dtype)


def _ffn_proj(x, w_gu, w_down, g, b, projs, tm=512):
    n, d = x.shape
    tm = min(tm, n)
    ws = [w for w, _ in projs]
    return pl.pallas_call(
        functools.partial(_ffn_proj_kernel, n_proj=len(projs)),
        grid=(n // tm,),
        in_specs=[_rows(tm, d), _resident(w_gu.shape), _resident(w_down.shape), _resident((1, d)),
                  _resident((1, d))] + [_resident(w.shape) for w in ws],
        out_specs=[_rows(tm, d)] + [_rows(tm, w.shape[1]) for w in ws],
        out_shape=[jax.ShapeDtypeStruct((n, d), F32)]
        + [jax.ShapeDtypeStruct((n, w.shape[1]), dt) for w, dt in projs],
        compiler_params=_params("parallel"),
        name="ffn_in_proj",
    )(x, w_gu, w_down, g.reshape(1, d), b.reshape(1, d), *ws)


def _proj_ffn_kernel(o_ref, wo_ref, x_ref, g1_ref, b1_ref, wgu_ref, wd_ref, g2_ref, b2_ref, out_ref):
    y = ALPHA * x_ref[...] + _dot(o_ref[...], wo_ref[...])
    x2 = _layer_norm(y, g1_ref[...], b1_ref[...])
    out_ref[...] = _ffn_ln_value(x2, wgu_ref, wd_ref, g2_ref, b2_ref)


def _proj_ffn(o, w_o, x, g1, b1, w_gu, w_down, g2, b2, tm=512):
    n, k = o.shape
    d = w_o.shape[1]
    tm = min(tm, n)
    vec = lambda a: a.reshape(1, d)
    return pl.pallas_call(
        _proj_ffn_kernel,
        grid=(n // tm,),
        in_specs=[_rows(tm, k), _resident(w_o.shape), _rows(tm, d), _resident((1, d)), _resident((1, d)),
                  _resident(w_gu.shape), _resident(w_down.shape), _resident((1, d)), _resident((1, d))],
        out_specs=_rows(tm, d),
        out_shape=jax.ShapeDtypeStruct((n, d), F32),
        compiler_params=_params("parallel"),
        name="out_proj_ffn",
    )(o, w_o, x, vec(g1), vec(b1), w_gu, w_down, vec(g2), vec(b2))


def _swa_kernel(sink_ref, q_ref, kc_ref, kp_ref, vc_ref, vp_ref, o_ref, *, tq):
    i = pl.program_id(1)
    t = SWA_BLOCK
    rep = SWA_HEADS // SWA_KV_HEADS
    kvs = range(SWA_KV_HEADS)
    lo = _lane_lo()
    rel = _tile_rel(t, 2 * t) + t
    col = lax.broadcasted_iota(jnp.int32, (t, 2 * t), 1)
    dist = rel.astype(F32)
    in_band = jnp.logical_and(rel >= 0, rel < SWA_BLOCK)
    slopes = [s * LOG2E for s in _alibi_slopes(SWA_HEADS)]
    lane = lambda g: slice(g * LANES, (g + 1) * LANES)
    for qb in range(tq // t):
        r0 = qb * t
        if qb == 0:
            ks = [jnp.concatenate([kp_ref[0, :, lane(g)], kc_ref[0, 0:t, lane(g)]], axis=0) for g in kvs]
            vs = [jnp.concatenate([vp_ref[0, :, lane(g)], vc_ref[0, 0:t, lane(g)]], axis=0) for g in kvs]
            valid = jnp.logical_and(in_band, col >= jnp.where(i == 0, t, 0))
        else:
            ks = [kc_ref[0, r0 - t:r0 + t, lane(g)] for g in kvs]
            vs = [vc_ref[0, r0 - t:r0 + t, lane(g)] for g in kvs]
            valid = in_band
        qst = []
        for g in kvs:
            parts = []
            for r in range(rep):
                h = g * rep + r
                qp = q_ref[0, r0:r0 + t, lane(h // 2)]
                parts.append(jnp.where(lo if h % 2 == 0 else jnp.logical_not(lo), qp, jnp.zeros_like(qp)))
            qst.append(jnp.concatenate(parts, axis=0))
        ss = [_dot_t(qst[g], ks[g]) for g in kvs]
        ps, dens = [], []
        for g in kvs:
            prow, drow = [], []
            for r in range(rep):
                h = g * rep + r
                s = jnp.where(valid, ss[g][r * t:(r + 1) * t] - slopes[h] * dist, NEG)
                sink = sink_ref[h]
                mx = jnp.maximum(jnp.max(s, axis=-1, keepdims=True), sink)
                p = jnp.exp2(s - mx)
                drow.append(jnp.sum(p, axis=-1, keepdims=True) + jnp.exp2(sink - mx))
                prow.append(p.astype(BF16))
            ps.append(jnp.concatenate(prow, axis=0))
            dens.append(drow)
        outs = [_dot(ps[g], vs[g]) for g in kvs]
        for g in kvs:
            for r in range(0, rep, 2):
                even = outs[g][r * t:(r + 1) * t] / dens[g][r]
                odd = outs[g][(r + 1) * t:(r + 2) * t] / dens[g][r + 1]
                o_ref[0, r0:r0 + t, lane((g * rep + r) // 2)] = jnp.where(lo, even, odd).astype(BF16)


def _swa_attention(qkv, sinks, batch, seq, tq=512):
    t = SWA_BLOCK
    tq = min(tq, seq)
    sub = tq // t
    qw = SWA_HEADS * SWA_HEAD_DIM
    kw = SWA_KV_HEADS * LANES
    qblk, kblk, vblk = 0, qw // kw, qw // kw + 1
    prev = lambda i: jnp.maximum(i * sub - 1, 0)
    return pl.pallas_call(
        functools.partial(_swa_kernel, tq=tq),
        grid=(batch, seq // tq),
        in_specs=[pl.BlockSpec(memory_space=pltpu.SMEM),
                  pl.BlockSpec((1, tq, qw), lambda b, i: (b, i, qblk)),
                  pl.BlockSpec((1, tq, kw), lambda b, i: (b, i, kblk)),
                  pl.BlockSpec((1, t, kw), lambda b, i: (b, prev(i), kblk)),
                  pl.BlockSpec((1, tq, kw), lambda b, i: (b, i, vblk)),
                  pl.BlockSpec((1, t, kw), lambda b, i: (b, prev(i), vblk))],
        out_specs=pl.BlockSpec((1, tq, qw), lambda b, i: (b, i, 0)),
        out_shape=jax.ShapeDtypeStruct((batch, seq, qw), BF16),
        compiler_params=_params("parallel", "arbitrary"),
        name="swa_attention",
    )(sinks, qkv, qkv, qkv, qkv, qkv)


def _mla_up_kernel(c_ref, gq_ref, gkv_ref, wq_ref, wk_ref, wv_ref, cq_ref, s1q_ref, s2q_ref,
                   ck_ref, s1k_ref, s2k_ref, q_ref, k_ref, v_ref):
    c = c_ref[...]
    cq = c[:, :MLA_Q_LORA]
    ckv = c[:, MLA_Q_LORA:MLA_Q_LORA + MLA_KV_LORA]
    kr = c[:, MLA_Q_LORA + MLA_KV_LORA:]
    cq = cq * lax.rsqrt(jnp.mean(cq * cq, axis=-1, keepdims=True) + RMS_EPS) * gq_ref[...]
    ckv = ckv * lax.rsqrt(jnp.mean(ckv * ckv, axis=-1, keepdims=True) + RMS_EPS) * gkv_ref[...]
    ckv_b = ckv.astype(BF16)
    q = _dot(cq.astype(BF16), wq_ref[...])
    kn = _dot(ckv_b, wk_ref[...])
    v_ref[...] = _dot(ckv_b, wv_ref[...]).astype(BF16)

    def rot(xh, cc, s1, s2):
        return xh * cc + pltpu.roll(xh, 16, 1) * s1 + pltpu.roll(xh, LANES - 16, 1) * s2

    kr = rot(kr, ck_ref[...], s1k_ref[...], s2k_ref[...])
    cq_t, s1q, s2q = cq_ref[...], s1q_ref[...], s2q_ref[...]
    for h in range(MLA_HEADS):
        sl = slice(h * LANES, (h + 1) * LANES)
        q_ref[:, sl] = rot(q[:, sl], cq_t, s1q, s2q).astype(BF16)
        k_ref[:, sl] = (kn[:, sl] + kr).astype(BF16)


def _mla_rope_tables(seq, scale):
    half = MLA_ROPE // 2
    inv = ROPE_THETA ** (-jnp.arange(0, MLA_ROPE, 2, dtype=F32) / MLA_ROPE)
    ang = jnp.arange(seq, dtype=F32)[:, None] * inv[None, :]
    cos, sin = jnp.cos(ang), jnp.sin(ang)
    z = lambda w: jnp.zeros((seq, w), F32)
    one = jnp.ones((seq, MLA_NOPE), F32)
    cc = jnp.concatenate([one, cos, cos, z(LANES - MLA_NOPE - MLA_ROPE)], axis=1) * scale
    s1 = jnp.concatenate([z(MLA_NOPE + half), sin, z(LANES - MLA_NOPE - MLA_ROPE)], axis=1) * scale
    s2 = jnp.concatenate([z(MLA_NOPE), -sin, z(LANES - MLA_NOPE - half)], axis=1) * scale
    return cc, s1, s2


def _mla_up(c, gq, gkv, wq, wk, wv, seq, tm=512):
    n, cw = c.shape
    tm = min(tm, seq)
    spt = seq // tm
    scale = LOG2E * (MLA_NOPE + MLA_ROPE) ** -0.5
    tq = _mla_rope_tables(seq, scale)
    tk = _mla_rope_tables(seq, 1.0)
    qk_w = MLA_HEADS * LANES
    v_w = MLA_HEADS * MLA_V
    full = lambda a: _resident(a.shape)
    tab = pl.BlockSpec((tm, LANES), lambda i: (i % spt, 0))
    gq2, gkv2 = gq.reshape(1, -1), gkv.reshape(1, -1)
    return pl.pallas_call(
        _mla_up_kernel,
        grid=(n // tm,),
        in_specs=[pl.BlockSpec((tm, cw), lambda i: (i, 0)), full(gq2), full(gkv2), full(wq), full(wk),
                  full(wv), tab, tab, tab, tab, tab, tab],
        out_specs=[pl.BlockSpec((tm, qk_w), lambda i: (i, 0)),
                   pl.BlockSpec((tm, qk_w), lambda i: (i, 0)),
                   pl.BlockSpec((tm, v_w), lambda i: (i, 0))],
        out_shape=[jax.ShapeDtypeStruct((n, qk_w), BF16), jax.ShapeDtypeStruct((n, qk_w), BF16),
                   jax.ShapeDtypeStruct((n, v_w), BF16)],
        compiler_params=_params("parallel"),
        name="mla_up_proj",
    )(c, gq2, gkv2, wq, wk, wv, *tq, *tk)


def _flash_init(tq):
    one = (jnp.full((tq, 1), NEG, F32), jnp.zeros((tq, 1), F32))
    return (one, one)


def _flash_pair(ss, v, carry, acc_ref):
    hs = range(2)
    ms = [jnp.maximum(carry[hh][0], jnp.max(ss[hh], axis=-1, keepdims=True)) for hh in hs]
    ps = [jnp.exp2(ss[hh] - ms[hh]) for hh in hs]
    al = [jnp.exp2(carry[hh][0] - ms[hh]) for hh in hs]
    ls = [al[hh] * carry[hh][1] + jnp.sum(ps[hh], axis=-1, keepdims=True) for hh in hs]
    for hh in hs:
        acc_ref[hh] = al[hh] * acc_ref[hh] + _dot(ps[hh].astype(BF16), v)
    return tuple((ms[hh], ls[hh]) for hh in hs)


def _tile_rel(tq, tk):
    return (lax.broadcasted_iota(jnp.int32, (tq, tk), 0) - lax.broadcasted_iota(jnp.int32, (tq, tk), 1))


def _mla_attn_kernel(q_ref, k_ref, v_ref, o_ref, acc_ref, *, tq, tk):
    i = pl.program_id(2)
    r = tq // tk
    rel = _tile_rel(tq, tk)
    sls = [slice(hh * LANES, (hh + 1) * LANES) for hh in range(2)]
    qs = [q_ref[0, :, sl] for sl in sls]

    def tile(j, diag, carry):
        start = pl.multiple_of(j * tk, tk)
        v = v_ref[0, pl.ds(start, tk), :]
        ss = [_dot_t(qs[hh], k_ref[0, pl.ds(start, tk), sls[hh]]) for hh in range(2)]
        if diag is not None:
            ss = [jnp.where(rel >= diag * tk, s, NEG) for s in ss]
        return _flash_pair(ss, v, carry, acc_ref)

    acc_ref[...] = jnp.zeros(acc_ref.shape, F32)
    carry = _flash_init(tq)
    for d in range(r):
        carry = tile(i * r + d, d, carry)
    carry = lax.fori_loop(0, i * r, lambda j, c: tile(j, None, c), carry)
    outs = [acc_ref[hh] / carry[hh][1] for hh in range(2)]
    o_ref[0] = jnp.where(_lane_lo(), outs[0], outs[1]).astype(BF16)


def _mla_attention(q, k, v, tq=512, tk=512):
    batch, seq, _ = q.shape
    tq, tk = min(tq, seq), min(tk, seq)
    return pl.pallas_call(
        functools.partial(_mla_attn_kernel, tq=tq, tk=tk),
        grid=(batch, MLA_HEADS // 2, seq // tq),
        in_specs=[pl.BlockSpec((1, tq, 2 * LANES), lambda b, m, i: (b, i, m)),
                  pl.BlockSpec((1, seq, 2 * LANES), lambda b, m, i: (b, 0, m)),
                  pl.BlockSpec((1, seq, LANES), lambda b, m, i: (b, 0, m))],
        out_specs=pl.BlockSpec((1, tq, LANES), lambda b, m, i: (b, i, m)),
        out_shape=jax.ShapeDtypeStruct(v.shape, BF16),
        scratch_shapes=[pltpu.VMEM((2, tq, LANES), F32)],
        compiler_params=_params("parallel", "parallel", "arbitrary"),
        name="mla_attention",
    )(q, k, v)


def _order_key(x):
    bits = lax.bitcast_convert_type(x, jnp.int32)
    return bits ^ (lax.shift_right_arithmetic(bits, 31) & 0x7FFFFFFF)


def _dsa_kernel(slope_ref, qi_ref, ki_ref, wi_ref, q_ref, k_ref, v_ref, o_ref, key_ref, bias_ref,
                dist_ref, acc_ref, *, t, topk):
    i = pl.program_id(1)
    m = pl.program_id(2)
    lo = _lane_lo()
    row = lax.broadcasted_iota(jnp.int32, (t, t), 0)
    col = lax.broadcasted_iota(jnp.int32, (t, t), 1)
    rel = row - col

    @pl.when(m == 0)
    def _select():
        wi = wi_ref[0] * (IDX_HEADS ** -0.5)

        def score_tile(j, carry):
            start = pl.multiple_of(j * t, t)
            ki = ki_ref[0, pl.ds(start, t), :]
            sc = jnp.zeros((t, t), F32)
            for p in range(IDX_HEADS // 2):
                qp = qi_ref[0, :, p * LANES:(p + 1) * LANES]
                for hh in range(2):
                    h = 2 * p + hh
                    qm = jnp.where(lo if hh == 0 else jnp.logical_not(lo), qp, jnp.zeros_like(qp))
                    dots = _dot_t(qm, ki) * (IDX_DIM ** -0.5)
                    sc = sc + wi[:, h:h + 1] * jnp.maximum(dots, 0.0)
            sc = sc + 0.0
            sc = jnp.where(rel >= (j - i) * t, sc, NEG)
            key_ref[j] = _order_key(sc)
            dist_ref[j] = ((i - j) * t + rel).astype(F32)
            return carry

        lax.fori_loop(0, i + 1, score_tile, 0)

        def count_ge(cand):
            def body(j, part):
                ge = jnp.where(key_ref[j] >= cand, 1.0, 0.0)
                for c0 in range(0, t, LANES):
                    part = part + ge[:, c0:c0 + LANES]
                return part
            part = lax.fori_loop(0, i + 1, body, jnp.zeros((t, LANES), F32))
            return jnp.sum(part, axis=-1, keepdims=True)

        int_min = jnp.int32(-2 ** 31)
        base = jnp.where(count_ge(jnp.zeros((t, 1), jnp.int32)) >= topk, jnp.int32(0), int_min)

        def bit_step(b, base):
            cand = base | lax.shift_left(jnp.int32(1), 30 - b)
            return jnp.where(count_ge(cand) >= topk, cand, base)

        thr = lax.fori_loop(0, 31, bit_step, base)
        n_gt = count_ge(thr + 1)
        need = topk - n_gt
        tri = jnp.where(row < col, 1.0, 0.0).astype(BF16)

        def select_tile(j, seen):
            key = key_ref[j]
            eq = key == thr
            eq_b = jnp.where(eq, 1.0, 0.0).astype(BF16)
            rank = seen + _dot(eq_b, tri)
            sel = jnp.logical_or(key > thr, jnp.logical_and(eq, rank < need))
            sel = jnp.logical_and(sel, rel >= (j - i) * t)
            bias_ref[j] = jnp.where(sel, 0.0, NEG)
            return seen + jnp.sum(eq_b.astype(F32), axis=-1, keepdims=True)

        lax.fori_loop(0, i + 1, select_tile, jnp.zeros((t, 1), F32))

    qp = q_ref[0]
    qs = [jnp.where(lo if hh == 0 else jnp.logical_not(lo), qp, jnp.zeros_like(qp)) for hh in range(2)]
    slopes = [slope_ref[2 * m + hh] for hh in range(2)]

    def tile(j, carry):
        start = pl.multiple_of(j * t, t)
        k = k_ref[0, pl.ds(start, t), :]
        v = v_ref[0, pl.ds(start, t), :]
        ss = [_dot_t(qs[hh], k) for hh in range(2)]
        dist, bias = dist_ref[j], bias_ref[j]
        ss = [ss[hh] - slopes[hh] * dist + bias for hh in range(2)]
        return _flash_pair(ss, v, carry, acc_ref)

    acc_ref[...] = jnp.zeros(acc_ref.shape, F32)
    carry = tile(i, _flash_init(t))
    carry = lax.fori_loop(0, i, tile, carry)
    outs = [acc_ref[hh] / carry[hh][1] for hh in range(2)]
    o_ref[0] = jnp.where(lo, outs[0], outs[1]).astype(BF16)


def _dsa_attention(proj, wi, batch, seq, t=512):
    t = min(t, seq)
    nt = seq // t
    topk = min(IDX_TOPK, seq // 4)
    qw = DSA_HEADS * DSA_HEAD_DIM
    q0 = 0
    k0 = qw // LANES
    v0 = k0 + DSA_KV_HEADS
    qi0 = (v0 + DSA_KV_HEADS) * LANES // (IDX_HEADS * IDX_DIM)
    ki0 = v0 + DSA_KV_HEADS + IDX_HEADS * IDX_DIM // LANES
    pairs = DSA_HEADS // 2
    per_kv = pairs // DSA_KV_HEADS
    return pl.pallas_call(
        functools.partial(_dsa_kernel, t=t, topk=topk),
        grid=(batch, nt, pairs),
        in_specs=[pl.BlockSpec(memory_space=pltpu.SMEM),
                  pl.BlockSpec((1, t, IDX_HEADS * IDX_DIM), lambda b, i, m: (b, i, qi0)),
                  pl.BlockSpec((1, seq, LANES), lambda b, i, m: (b, 0, ki0)),
                  pl.BlockSpec((1, t, LANES), lambda b, i, m: (b, i, 0)),
                  pl.BlockSpec((1, t, LANES), lambda b, i, m: (b, i, q0 + m)),
                  pl.BlockSpec((1, seq, LANES), lambda b, i, m: (b, 0, k0 + m // per_kv)),
                  pl.BlockSpec((1, seq, LANES), lambda b, i, m: (b, 0, v0 + m // per_kv))],
        out_specs=pl.BlockSpec((1, t, LANES), lambda b, i, m: (b, i, m)),
        out_shape=jax.ShapeDtypeStruct((batch, seq, qw), BF16),
        scratch_shapes=[pltpu.VMEM((nt, t, t), jnp.int32), pltpu.VMEM((nt, t, t), F32),
                        pltpu.VMEM((nt, t, t), F32), pltpu.VMEM((2, t, LANES), F32)],
        compiler_params=_params("parallel", "arbitrary", "arbitrary"),
        name="dsa_attention",
    )(jnp.asarray([s * LOG2E for s in _alibi_slopes(DSA_HEADS)], F32), proj, proj, wi, proj, proj, proj)


def _sb_kernel(q_ref, k_ref, v_ref, o_ref, later_ref, acc_ref, *, tq, tk):
    i = pl.program_id(2)
    r = tq // tk
    lo = _lane_lo()
    rel = _tile_rel(tq, tk)
    suffix = jnp.where(_tile_rel(tk, tk) >= 0, 1.0, 0.0).astype(BF16)
    qp = q_ref[0]
    qs = [jnp.where(lo if hh == 0 else jnp.logical_not(lo), qp, jnp.zeros_like(qp)) for hh in range(2)]

    def tile(j, diag):
        start = pl.multiple_of(j * tk, tk)
        k = k_ref[0, pl.ds(start, tk), :]
        v = v_ref[0, pl.ds(start, tk), :]
        hs = range(2)
        zs = [_dot_t(qs[hh], k) for hh in hs]
        lgs = []
        for z in zs:
            nz = -z
            lgs.append(jnp.minimum(nz, 0.0) - jnp.log2(1.0 + jnp.exp2(jnp.minimum(z, nz))))
        if diag is not None:
            strict = rel > diag * tk
            lgs = [jnp.where(strict, lg, 0.0) for lg in lgs]
        lbs = [lg.astype(BF16) for lg in lgs]
        runs = [_dot(lb, suffix) for lb in lbs]
        laters = [later_ref[hh] for hh in hs]
        a_s = [jnp.exp2(zs[hh] + runs[hh] + laters[hh]) for hh in hs]
        if diag is not None:
            a_s = [jnp.where(strict, a, 0.0) for a in a_s]
        for hh in hs:
            later_ref[hh] = laters[hh] + runs[hh][:, 0:1]
        for hh in hs:
            acc_ref[hh] += _dot(a_s[hh].astype(BF16), v)

    later_ref[...] = jnp.zeros(later_ref.shape, F32)
    acc_ref[...] = jnp.zeros(acc_ref.shape, F32)
    for d in reversed(range(r)):
        tile(i * r + d, d)

    @pl.loop(0, i * r)
    def _(jj):
        tile(i * r - 1 - jj, None)

    o_ref[0] = jnp.where(lo, acc_ref[0], acc_ref[1]).astype(BF16)


def _sb_attention(qkv, batch, seq, tq=512, tk=256):
    tq, tk = min(tq, seq), min(tk, seq)
    w = SB_HEADS * SB_HEAD_DIM
    pairs = SB_HEADS // 2
    return pl.pallas_call(
        functools.partial(_sb_kernel, tq=tq, tk=tk),
        grid=(batch, pairs, seq // tq),
        in_specs=[pl.BlockSpec((1, tq, LANES), lambda b, m, i: (b, i, m)),
                  pl.BlockSpec((1, seq, LANES), lambda b, m, i: (b, 0, pairs + m)),
                  pl.BlockSpec((1, seq, LANES), lambda b, m, i: (b, 0, 2 * pairs + m))],
        out_specs=pl.BlockSpec((1, tq, LANES), lambda b, m, i: (b, i, m)),
        out_shape=jax.ShapeDtypeStruct((batch, seq, w), BF16),
        scratch_shapes=[pltpu.VMEM((2, tq, 1), F32), pltpu.VMEM((2, tq, LANES), F32)],
        compiler_params=_params("parallel", "parallel", "arbitrary"),
        name="sb_attention",
    )(qkv, qkv, qkv)


def _dup_heads(w, n_heads, dim):
    d = w.shape[0]
    w = w.reshape(d, n_heads, 1, dim)
    return jnp.broadcast_to(w, (d, n_heads, 2, dim)).reshape(d, n_heads * 2 * dim)


def _pad_heads(w, n_heads, dim, to):
    d = w.shape[0]
    w = w.reshape(d, n_heads, dim)
    return jnp.pad(w, ((0, 0), (0, 0), (0, to - dim))).reshape(d, n_heads * to)


def _swa_w_in(w):
    qw = SWA_HEADS * SWA_HEAD_DIM
    kw = SWA_KV_HEADS * SWA_HEAD_DIM
    q, k, v = w[:, :qw], w[:, qw:qw + kw], w[:, qw + kw:]
    q = q * (LOG2E * SWA_HEAD_DIM ** -0.5)
    return jnp.concatenate([q, _dup_heads(k, SWA_KV_HEADS, SWA_HEAD_DIM),
                            _dup_heads(v, SWA_KV_HEADS, SWA_HEAD_DIM)], axis=1).astype(BF16)


def _mla_w_in(w):
    d = w.shape[0]
    lat = MLA_Q_LORA + MLA_KV_LORA
    z = lambda c: jnp.zeros((d, c), w.dtype)
    return jnp.concatenate([w[:, :lat], z(MLA_NOPE), w[:, lat:], z(LANES - MLA_NOPE - MLA_ROPE)],
                           axis=1).astype(BF16)


def _sb_w_in(w):
    qw = SB_HEADS * SB_HEAD_DIM
    return jnp.concatenate([w[:, :qw] * (LOG2E * SB_HEAD_DIM ** -0.5), w[:, qw:]], axis=1).astype(BF16)


def _dsa_w_in(w):
    qw = DSA_HEADS * DSA_HEAD_DIM
    kw = DSA_KV_HEADS * DSA_HEAD_DIM
    iw = IDX_HEADS * IDX_DIM
    c = np.cumsum([qw, kw, kw, iw, IDX_DIM]).tolist()
    q, k, v, qi, ki, wi = (w[:, :c[0]], w[:, c[0]:c[1]], w[:, c[1]:c[2]], w[:, c[2]:c[3]],
                           w[:, c[3]:c[4]], w[:, c[4]:])
    q = q * (LOG2E * DSA_HEAD_DIM ** -0.5)
    main = jnp.concatenate([q, _dup_heads(k, DSA_KV_HEADS, DSA_HEAD_DIM),
                            _dup_heads(v, DSA_KV_HEADS, DSA_HEAD_DIM), qi, _dup_heads(ki, 1, IDX_DIM)],
                           axis=1).astype(BF16)
    wi = jnp.pad(wi, ((0, 0), (0, LANES - IDX_HEADS))).astype(BF16)
    return main, wi


def kernel(x, ffn1_w_gu, ffn1_w_down, ffn2_w_gu, ffn2_w_down, ln_g, ln_b, a_w_in, a_sinks, a_w_o,
           b_w_in, b_q_norm, b_w_uq, b_kv_norm, b_w_ukv, b_w_o, c_w_in, c_w_o, d_w_in, d_w_o):
    batch, seq, d = x.shape
    n = batch * seq
    x = x.reshape(n, d)
    for i in range(DEPTH):
        mixer, j = i % N_MIXERS, i // N_MIXERS

        def ffn1(projs):
            return _ffn_proj(x, ffn1_w_gu[i].astype(BF16), ffn1_w_down[i].astype(BF16), ln_g[i, 0],
                             ln_b[i, 0], projs)

        if mixer == 0:
            x, qkv = ffn1([(_swa_w_in(a_w_in[j]), BF16)])
            o = _swa_attention(qkv.reshape(batch, seq, -1), a_sinks[j] * LOG2E, batch, seq)
            w_o = a_w_o[j]
        elif mixer == 1:
            x, c = ffn1([(_mla_w_in(b_w_in[j]), F32)])
            hd = MLA_NOPE + MLA_ROPE
            wq = _pad_heads(b_w_uq[j], MLA_HEADS, hd, LANES).astype(BF16)
            wkv = b_w_ukv[j].reshape(MLA_KV_LORA, MLA_HEADS, MLA_NOPE + MLA_V)
            wk = _pad_heads(wkv[:, :, :MLA_NOPE].reshape(MLA_KV_LORA, -1), MLA_HEADS, MLA_NOPE,
                            LANES).astype(BF16)
            wv = wkv[:, :, MLA_NOPE:].reshape(MLA_KV_LORA, -1).astype(BF16)
            q, k, v = _mla_up(c, b_q_norm[j], b_kv_norm[j], wq, wk, wv, seq)
            o = _mla_attention(q.reshape(batch, seq, -1), k.reshape(batch, seq, -1),
                               v.reshape(batch, seq, -1))
            w_o = b_w_o[j]
        elif mixer == 2:
            w_main, w_wi = _dsa_w_in(c_w_in[j])
            x, proj, wi = ffn1([(w_main, BF16), (w_wi, F32)])
            o = _dsa_attention(proj.reshape(batch, seq, -1), wi.reshape(batch, seq, -1), batch, seq)
            w_o = c_w_o[j]
        else:
            x, qkv = ffn1([(_sb_w_in(d_w_in[j]), BF16)])
            o = _sb_attention(qkv.reshape(batch, seq, -1), batch, seq)
            w_o = d_w_o[j]
        x = _proj_ffn(o.reshape(n, -1), w_o.astype(BF16), x, ln_g[i, 1], ln_b[i, 1],
                      ffn2_w_gu[i].astype(BF16), ffn2_w_down[i].astype(BF16), ln_g[i, 2], ln_b[i, 2])
    return x.reshape(batch, seq, d)
```

```python
import functools
import math

import numpy as np
import jax
import jax.numpy as jnp
from jax import lax
from jax.experimental import pallas as pl
from jax.experimental.pallas import tpu as pltpu

D_MODEL = 1024
DEPTH = 4
N_MIXERS = 4
D_FF = 2816
LN_EPS = 1e-5
RMS_EPS = 1e-6
NEG = -1e30

SWA_HEADS, SWA_KV_HEADS, SWA_HEAD_DIM, SWA_BLOCK = 16, 4, 64, 128
MLA_HEADS, MLA_Q_LORA, MLA_KV_LORA, MLA_NOPE, MLA_ROPE, MLA_V = 16, 768, 256, 64, 32, 64
ROPE_THETA = 10000.0
DSA_HEADS, DSA_KV_HEADS, DSA_HEAD_DIM = 16, 4, 64
IDX_HEADS, IDX_DIM, IDX_TOPK = 8, 64, 256
SB_HEADS, SB_HEAD_DIM = 16, 64

ALPHA = (2.0 * DEPTH) ** 0.25
LOG2E = math.log2(math.e)

LANES = 128
HALF = 64
VMEM_LIMIT = 56 * 1024 * 1024
BF16 = jnp.bfloat16
F32 = jnp.float32


def _params(*sem, flags=None):
    return pltpu.CompilerParams(dimension_semantics=sem, vmem_limit_bytes=VMEM_LIMIT, flags=flags)


def _dot(a, b):
    return jnp.dot(a, b, preferred_element_type=F32)


def _dot_t(a, b):
    return lax.dot_general(a, b, (((1,), (1,)), ((), ())), preferred_element_type=F32)


def _alibi_slopes(n_heads):
    return [2.0 ** (-8.0 * (i + 1) / n_heads) for i in range(n_heads)]


def _layer_norm(y, g, b):
    mu = jnp.mean(y, axis=-1, keepdims=True)
    yc = y - mu
    var = jnp.mean(yc * yc, axis=-1, keepdims=True)
    return yc * lax.rsqrt(var + LN_EPS) * g + b


def _lane_lo():
    return lax.broadcasted_iota(jnp.int32, (1, LANES), 1) < HALF


def _resident(shape):
    return pl.BlockSpec(shape, lambda i: (0,) * len(shape), pipeline_mode=pl.Buffered(1))


def _rows(tm, width):
    return pl.BlockSpec((tm, width), lambda i: (i, 0))


def _ffn_ln_value(x, wgu_ref, wd_ref, g_ref, b_ref):
    f = wd_ref.shape[0]
    xb = x.astype(BF16)
    h = _dot(xb, wgu_ref[:, :f])
    u = _dot(xb, wgu_ref[:, f:])
    a = h * (1.0 / (1.0 + jnp.exp(-h))) * u
    y = ALPHA * x + 0.5 * _dot(a.astype(BF16), wd_ref[...])
    return _layer_norm(y, g_ref[...], b_ref[...])


def _ffn_proj_kernel(x_ref, wgu_ref, wd_ref, g_ref, b_ref, *refs, n_proj):
    w_refs, o_ref, p_refs = refs[:n_proj], refs[n_proj], refs[n_proj + 1:]
    y = _ffn_ln_value(x_ref[...], wgu_ref, wd_ref, g_ref, b_ref)
    o_ref[...] = y
    yb = y.astype(BF16)
    for w_ref, p_ref in zip(w_refs, p_refs):
        p_ref[...] = _dot(yb, w_ref[...]).astype(p_ref.dtype)


def _ffn_proj(x, w_gu, w_down, g, b, projs, tm=512):
    n, d = x.shape
    tm = min(tm, n)
    ws = [w for w, _ in projs]
    return pl.pallas_call(
        functools.partial(_ffn_proj_kernel, n_proj=len(projs)),
        grid=(n // tm,),
        in_specs=[_rows(tm, d), _resident(w_gu.shape), _resident(w_down.shape), _resident((1, d)),
                  _resident((1, d))] + [_resident(w.shape) for w in ws],
        out_specs=[_rows(tm, d)] + [_rows(tm, w.shape[1]) for w in ws],
        out_shape=[jax.ShapeDtypeStruct((n, d), F32)]
        + [jax.ShapeDtypeStruct((n, w.shape[1]), dt) for w, dt in projs],
        compiler_params=_params("parallel"),
        name="ffn_in_proj",
    )(x, w_gu, w_down, g.reshape(1, d), b.reshape(1, d), *ws)


def _proj_ffn_kernel(o_ref, wo_ref, x_ref, g1_ref, b1_ref, wgu_ref, wd_ref, g2_ref, b2_ref, out_ref):
    y = ALPHA * x_ref[...] + _dot(o_ref[...], wo_ref[...])
    x2 = _layer_norm(y, g1_ref[...], b1_ref[...])
    out_ref[...] = _ffn_ln_value(x2, wgu_ref, wd_ref, g2_ref, b2_ref)


def _proj_ffn(o, w_o, x, g1, b1, w_gu, w_down, g2, b2, tm=512):
    n, k = o.shape
    d = w_o.shape[1]
    tm = min(tm, n)
    vec = lambda a: a.reshape(1, d)
    return pl.pallas_call(
        _proj_ffn_kernel,
        grid=(n // tm,),
        in_specs=[_rows(tm, k), _resident(w_o.shape), _rows(tm, d), _resident((1, d)), _resident((1, d)),
                  _resident(w_gu.shape), _resident(w_down.shape), _resident((1, d)), _resident((1, d))],
        out_specs=_rows(tm, d),
        out_shape=jax.ShapeDtypeStruct((n, d), F32),
        compiler_params=_params("parallel"),
        name="out_proj_ffn",
    )(o, w_o, x, vec(g1), vec(b1), w_gu, w_down, vec(g2), vec(b2))


def _swa_kernel(sink_ref, q_ref, kc_ref, kp_ref, vc_ref, vp_ref, o_ref, *, tq):
    i = pl.program_id(1)
    t = SWA_BLOCK
    rep = SWA_HEADS // SWA_KV_HEADS
    kvs = range(SWA_KV_HEADS)
    lo = _lane_lo()
    rel = _tile_rel(t, 2 * t) + t
    col = lax.broadcasted_iota(jnp.int32, (t, 2 * t), 1)
    dist = rel.astype(F32)
    in_band = jnp.logical_and(rel >= 0, rel < SWA_BLOCK)
    slopes = [s * LOG2E for s in _alibi_slopes(SWA_HEADS)]
    lane = lambda g: slice(g * LANES, (g + 1) * LANES)
    for qb in range(tq // t):
        r0 = qb * t
        if qb == 0:
            ks = [jnp.concatenate([kp_ref[0, :, lane(g)], kc_ref[0, 0:t, lane(g)]], axis=0) for g in kvs]
            vs = [jnp.concatenate([vp_ref[0, :, lane(g)], vc_ref[0, 0:t, lane(g)]], axis=0) for g in kvs]
            valid = jnp.logical_and(in_band, col >= jnp.where(i == 0, t, 0))
        else:
            ks = [kc_ref[0, r0 - t:r0 + t, lane(g)] for g in kvs]
            vs = [vc_ref[0, r0 - t:r0 + t, lane(g)] for g in kvs]
            valid = in_band
        qst = []
        for g in kvs:
            parts = []
            for r in range(rep):
                h = g * rep + r
                qp = q_ref[0, r0:r0 + t, lane(h // 2)]
                parts.append(jnp.where(lo if h % 2 == 0 else jnp.logical_not(lo), qp, jnp.zeros_like(qp)))
            qst.append(jnp.concatenate(parts, axis=0))
        ss = [_dot_t(qst[g], ks[g]) for g in kvs]
        ps, dens = [], []
        for g in kvs:
            prow, drow = [], []
            for r in range(rep):
                h = g * rep + r
                s = jnp.where(valid, ss[g][r * t:(r + 1) * t] - slopes[h] * dist, NEG)
                sink = sink_ref[h]
                mx = jnp.maximum(jnp.max(s, axis=-1, keepdims=True), sink)
                p = jnp.exp2(s - mx)
                drow.append(jnp.sum(p, axis=-1, keepdims=True) + jnp.exp2(sink - mx))
                prow.append(p.astype(BF16))
            ps.append(jnp.concatenate(prow, axis=0))
            dens.append(drow)
        outs = [_dot(ps[g], vs[g]) for g in kvs]
        for g in kvs:
            for r in range(0, rep, 2):
                even = outs[g][r * t:(r + 1) * t] / dens[g][r]
                odd = outs[g][(r + 1) * t:(r + 2) * t] / dens[g][r + 1]
                o_ref[0, r0:r0 + t, lane((g * rep + r) // 2)] = jnp.where(lo, even, odd).astype(BF16)


def _swa_attention(qkv, sinks, batch, seq, tq=512):
    t = SWA_BLOCK
    tq = min(tq, seq)
    sub = tq // t
    qw = SWA_HEADS * SWA_HEAD_DIM
    kw = SWA_KV_HEADS * LANES
    qblk, kblk, vblk = 0, qw // kw, qw // kw + 1
    prev = lambda i: jnp.maximum(i * sub - 1, 0)
    return pl.pallas_call(
        functools.partial(_swa_kernel, tq=tq),
        grid=(batch, seq // tq),
        in_specs=[pl.BlockSpec(memory_space=pltpu.SMEM),
                  pl.BlockSpec((1, tq, qw), lambda b, i: (b, i, qblk)),
                  pl.BlockSpec((1, tq, kw), lambda b, i: (b, i, kblk)),
                  pl.BlockSpec((1, t, kw), lambda b, i: (b, prev(i), kblk)),
                  pl.BlockSpec((1, tq, kw), lambda b, i: (b, i, vblk)),
                  pl.BlockSpec((1, t, kw), lambda b, i: (b, prev(i), vblk))],
        out_specs=pl.BlockSpec((1, tq, qw), lambda b, i: (b, i, 0)),
        out_shape=jax.ShapeDtypeStruct((batch, seq, qw), BF16),
        compiler_params=_params("parallel", "arbitrary"),
        name="swa_attention",
    )(sinks, qkv, qkv, qkv, qkv, qkv)


def _mla_up_kernel(c_ref, gq_ref, gkv_ref, wq_ref, wk_ref, wv_ref, cq_ref, s1q_ref, s2q_ref,
                   ck_ref, s1k_ref, s2k_ref, q_ref, k_ref, v_ref):
    c = c_ref[...]
    cq = c[:, :MLA_Q_LORA]
    ckv = c[:, MLA_Q_LORA:MLA_Q_LORA + MLA_KV_LORA]
    kr = c[:, MLA_Q_LORA + MLA_KV_LORA:]
    cq = cq * lax.rsqrt(jnp.mean(cq * cq, axis=-1, keepdims=True) + RMS_EPS) * gq_ref[...]
    ckv = ckv * lax.rsqrt(jnp.mean(ckv * ckv, axis=-1, keepdims=True) + RMS_EPS) * gkv_ref[...]
    ckv_b = ckv.astype(BF16)
    q = _dot(cq.astype(BF16), wq_ref[...])
    kn = _dot(ckv_b, wk_ref[...])
    v_ref[...] = _dot(ckv_b, wv_ref[...]).astype(BF16)

    def rot(xh, cc, s1, s2):
        return xh * cc + pltpu.roll(xh, 16, 1) * s1 + pltpu.roll(xh, LANES - 16, 1) * s2

    kr = rot(kr, ck_ref[...], s1k_ref[...], s2k_ref[...])
    cq_t, s1q, s2q = cq_ref[...], s1q_ref[...], s2q_ref[...]
    for h in range(MLA_HEADS):
        sl = slice(h * LANES, (h + 1) * LANES)
        q_ref[:, sl] = rot(q[:, sl], cq_t, s1q, s2q).astype(BF16)
        k_ref[:, sl] = (kn[:, sl] + kr).astype(BF16)


def _mla_rope_tables(seq, scale):
    half = MLA_ROPE // 2
    inv = ROPE_THETA ** (-jnp.arange(0, MLA_ROPE, 2, dtype=F32) / MLA_ROPE)
    ang = jnp.arange(seq, dtype=F32)[:, None] * inv[None, :]
    cos, sin = jnp.cos(ang), jnp.sin(ang)
    z = lambda w: jnp.zeros((seq, w), F32)
    one = jnp.ones((seq, MLA_NOPE), F32)
    cc = jnp.concatenate([one, cos, cos, z(LANES - MLA_NOPE - MLA_ROPE)], axis=1) * scale
    s1 = jnp.concatenate([z(MLA_NOPE + half), sin, z(LANES - MLA_NOPE - MLA_ROPE)], axis=1) * scale
    s2 = jnp.concatenate([z(MLA_NOPE), -sin, z(LANES - MLA_NOPE - half)], axis=1) * scale
    return cc, s1, s2


def _mla_up(c, gq, gkv, wq, wk, wv, seq, tm=512):
    n, cw = c.shape
    tm = min(tm, seq)
    spt = seq // tm
    scale = LOG2E * (MLA_NOPE + MLA_ROPE) ** -0.5
    tq = _mla_rope_tables(seq, scale)
    tk = _mla_rope_tables(seq, 1.0)
    qk_w = MLA_HEADS * LANES
    v_w = MLA_HEADS * MLA_V
    full = lambda a: _resident(a.shape)
    tab = pl.BlockSpec((tm, LANES), lambda i: (i % spt, 0))
    gq2, gkv2 = gq.reshape(1, -1), gkv.reshape(1, -1)
    return pl.pallas_call(
        _mla_up_kernel,
        grid=(n // tm,),
        in_specs=[pl.BlockSpec((tm, cw), lambda i: (i, 0)), full(gq2), full(gkv2), full(wq), full(wk),
                  full(wv), tab, tab, tab, tab, tab, tab],
        out_specs=[pl.BlockSpec((tm, qk_w), lambda i: (i, 0)),
                   pl.BlockSpec((tm, qk_w), lambda i: (i, 0)),
                   pl.BlockSpec((tm, v_w), lambda i: (i, 0))],
        out_shape=[jax.ShapeDtypeStruct((n, qk_w), BF16), jax.ShapeDtypeStruct((n, qk_w), BF16),
                   jax.ShapeDtypeStruct((n, v_w), BF16)],
        compiler_params=_params("parallel"),
        name="mla_up_proj",
    )(c, gq2, gkv2, wq, wk, wv, *tq, *tk)


def _flash_init(tq):
    one = (jnp.full((tq, 1), NEG, F32), jnp.zeros((tq, 1), F32))
    return (one, one)


def _flash_pair(ss, v, carry, acc_ref):
    hs = range(2)
    ms = [jnp.maximum(carry[hh][0], jnp.max(ss[hh], axis=-1, keepdims=True)) for hh in hs]
    ps = [jnp.exp2(ss[hh] - ms[hh]) for hh in hs]
    al = [jnp.exp2(carry[hh][0] - ms[hh]) for hh in hs]
    ls = [al[hh] * carry[hh][1] + jnp.sum(ps[hh], axis=-1, keepdims=True) for hh in hs]
    for hh in hs:
        acc_ref[hh] = al[hh] * acc_ref[hh] + _dot(ps[hh].astype(BF16), v)
    return tuple((ms[hh], ls[hh]) for hh in hs)


def _tile_rel(tq, tk):
    return (lax.broadcasted_iota(jnp.int32, (tq, tk), 0) - lax.broadcasted_iota(jnp.int32, (tq, tk), 1))


def _mla_attn_kernel(q_ref, k_ref, v_ref, o_ref, acc_ref, *, tq, tk):
    i = pl.program_id(2)
    r = tq // tk
    rel = _tile_rel(tq, tk)
    sls = [slice(hh * LANES, (hh + 1) * LANES) for hh in range(2)]
    qs = [q_ref[0, :, sl] for sl in sls]

    def tile(j, diag, carry):
        start = pl.multiple_of(j * tk, tk)
        v = v_ref[0, pl.ds(start, tk), :]
        ss = [_dot_t(qs[hh], k_ref[0, pl.ds(start, tk), sls[hh]]) for hh in range(2)]
        if diag is not None:
            ss = [jnp.where(rel >= diag * tk, s, NEG) for s in ss]
        return _flash_pair(ss, v, carry, acc_ref)

    acc_ref[...] = jnp.zeros(acc_ref.shape, F32)
    carry = _flash_init(tq)
    for d in range(r):
        carry = tile(i * r + d, d, carry)
    carry = lax.fori_loop(0, i * r, lambda j, c: tile(j, None, c), carry)
    outs = [acc_ref[hh] / carry[hh][1] for hh in range(2)]
    o_ref[0] = jnp.where(_lane_lo(), outs[0], outs[1]).astype(BF16)


def _mla_attention(q, k, v, tq=512, tk=512):
    batch, seq, _ = q.shape
    tq, tk = min(tq, seq), min(tk, seq)
    return pl.pallas_call(
        functools.partial(_mla_attn_kernel, tq=tq, tk=tk),
        grid=(batch, MLA_HEADS // 2, seq // tq),
        in_specs=[pl.BlockSpec((1, tq, 2 * LANES), lambda b, m, i: (b, i, m)),
                  pl.BlockSpec((1, seq, 2 * LANES), lambda b, m, i: (b, 0, m)),
                  pl.BlockSpec((1, seq, LANES), lambda b, m, i: (b, 0, m))],
        out_specs=pl.BlockSpec((1, tq, LANES), lambda b, m, i: (b, i, m)),
        out_shape=jax.ShapeDtypeStruct(v.shape, BF16),
        scratch_shapes=[pltpu.VMEM((2, tq, LANES), F32)],
        compiler_params=_params("parallel", "parallel", "arbitrary"),
        name="mla_attention",
    )(q, k, v)


def _order_key(x):
    bits = lax.bitcast_convert_type(x, jnp.int32)
    return bits ^ (lax.shift_right_arithmetic(bits, 31) & 0x7FFFFFFF)


def _dsa_kernel(slope_ref, qi_ref, ki_ref, wi_ref, q_ref, k_ref, v_ref, o_ref, key_ref, bias_ref,
                dist_ref, acc_ref, byte_ref, *, t, topk):
    i = pl.program_id(1)
    m = pl.program_id(2)
    lo = _lane_lo()
    row = lax.broadcasted_iota(jnp.int32, (t, t), 0)
    col = lax.broadcasted_iota(jnp.int32, (t, t), 1)
    rel = row - col

    @pl.when(m == 0)
    def _select():
        wi = wi_ref[0] * (IDX_HEADS ** -0.5)

        def score_tile(j, carry):
            start = pl.multiple_of(j * t, t)
            ki = ki_ref[0, pl.ds(start, t), :]
            sc = jnp.zeros((t, t), F32)
            for p in range(IDX_HEADS // 2):
                qp = qi_ref[0, :, p * LANES:(p + 1) * LANES]
                for hh in range(2):
                    h = 2 * p + hh
                    qm = jnp.where(lo if hh == 0 else jnp.logical_not(lo), qp, jnp.zeros_like(qp))
                    dots = _dot_t(qm, ki)
                    sc = sc + wi[:, h:h + 1] * jnp.maximum(dots, 0.0)
            sc = sc + 0.0
            sc = jnp.where(rel >= (j - i) * t, sc, NEG)
            key_ref[j] = _order_key(sc)
            dist_ref[j] = ((i - j) * t + rel).astype(F32)
            return carry

        lax.fori_loop(0, i + 1, score_tile, 0)

        one_b, zero_b = jnp.ones((), BF16), jnp.zeros((), BF16)

        def count_ge(cand):
            cand_b = cand.astype(BF16)

            def body(j, part):
                ge = jnp.where(byte_ref[j] >= cand_b, one_b, zero_b)
                for c0 in range(0, t, LANES):
                    part = part + ge[:, c0:c0 + LANES]
                return part
            part = lax.fori_loop(0, i + 1, body, jnp.zeros((t, LANES), BF16))
            return jnp.sum(part.astype(F32), axis=-1, keepdims=True)

        prefix = jnp.zeros((t, 1), jnp.int32)
        above = jnp.zeros((t, 1), F32)
        for stage in range(4):
            shift = 24 - 8 * stage

            def fill(j, carry, stage=stage, shift=shift, prefix=prefix):
                key = key_ref[j]
                if stage == 0:
                    byte = lax.shift_right_arithmetic(key, shift) + 128
                else:
                    match = lax.shift_right_arithmetic(key, shift + 8) == prefix
                    byte = jnp.where(match, lax.shift_right_logical(key, shift) & 255, -1)
                byte_ref[j] = byte.astype(F32).astype(BF16)
                return carry

            lax.fori_loop(0, i + 1, fill, 0)
            val = jnp.zeros((t, 1), F32)
            rej = jnp.zeros((t, 1), F32)
            for bit in (128, 64, 32, 16, 8, 4, 2, 1):
                cand = val + float(bit)
                cnt = count_ge(cand)
                ok = above + cnt >= topk
                val = jnp.where(ok, cand, val)
                rej = jnp.where(ok, rej, cnt)
            above = above + rej
            byte_i = val.astype(jnp.int32) - (128 if stage == 0 else 0)
            prefix = byte_i if stage == 0 else lax.shift_left(prefix, 8) | byte_i
        thr = prefix
        need = topk - above
        tri = jnp.where(row < col, 1.0, 0.0).astype(BF16)

        def select_tile(j, seen):
            key = key_ref[j]
            eq = key == thr
            eq_b = jnp.where(eq, 1.0, 0.0).astype(BF16)
            rank = seen + _dot(eq_b, tri)
            sel = jnp.logical_or(key > thr, jnp.logical_and(eq, rank < need))
            sel = jnp.logical_and(sel, rel >= (j - i) * t)
            bias_ref[j] = jnp.where(sel, 0.0, NEG)
            return seen + jnp.sum(eq_b.astype(F32), axis=-1, keepdims=True)

        lax.fori_loop(0, i + 1, select_tile, jnp.zeros((t, 1), F32))

    qp = q_ref[0]
    qs = [jnp.where(lo if hh == 0 else jnp.logical_not(lo), qp, jnp.zeros_like(qp)) for hh in range(2)]
    slopes = [slope_ref[2 * m + hh] for hh in range(2)]

    def tile(j, carry):
        start = pl.multiple_of(j * t, t)
        k = k_ref[0, pl.ds(start, t), :]
        v = v_ref[0, pl.ds(start, t), :]
        ss = [_dot_t(qs[hh], k) for hh in range(2)]
        dist, bias = dist_ref[j], bias_ref[j]
        ss = [ss[hh] - slopes[hh] * dist + bias for hh in range(2)]
        return _flash_pair(ss, v, carry, acc_ref)

    acc_ref[...] = jnp.zeros(acc_ref.shape, F32)
    carry = tile(i, _flash_init(t))
    carry = lax.fori_loop(0, i, tile, carry)
    outs = [acc_ref[hh] / carry[hh][1] for hh in range(2)]
    o_ref[0] = jnp.where(lo, outs[0], outs[1]).astype(BF16)


def _dsa_attention(proj, wi, batch, seq, t=512):
    t = min(t, seq)
    nt = seq // t
    topk = min(IDX_TOPK, seq // 4)
    qw = DSA_HEADS * DSA_HEAD_DIM
    q0 = 0
    k0 = qw // LANES
    v0 = k0 + DSA_KV_HEADS
    qi0 = (v0 + DSA_KV_HEADS) * LANES // (IDX_HEADS * IDX_DIM)
    ki0 = v0 + DSA_KV_HEADS + IDX_HEADS * IDX_DIM // LANES
    pairs = DSA_HEADS // 2
    per_kv = pairs // DSA_KV_HEADS
    return pl.pallas_call(
        functools.partial(_dsa_kernel, t=t, topk=topk),
        grid=(batch, nt, pairs),
        in_specs=[pl.BlockSpec(memory_space=pltpu.SMEM),
                  pl.BlockSpec((1, t, IDX_HEADS * IDX_DIM), lambda b, i, m: (b, i, qi0)),
                  pl.BlockSpec((1, seq, LANES), lambda b, i, m: (b, 0, ki0)),
                  pl.BlockSpec((1, t, LANES), lambda b, i, m: (b, i, 0)),
                  pl.BlockSpec((1, t, LANES), lambda b, i, m: (b, i, q0 + m)),
                  pl.BlockSpec((1, seq, LANES), lambda b, i, m: (b, 0, k0 + m // per_kv)),
                  pl.BlockSpec((1, seq, LANES), lambda b, i, m: (b, 0, v0 + m // per_kv))],
        out_specs=pl.BlockSpec((1, t, LANES), lambda b, i, m: (b, i, m)),
        out_shape=jax.ShapeDtypeStruct((batch, seq, qw), BF16),
        scratch_shapes=[pltpu.VMEM((nt, t, t), jnp.int32), pltpu.VMEM((nt, t, t), F32),
                        pltpu.VMEM((nt, t, t), F32), pltpu.VMEM((2, t, LANES), F32),
                        pltpu.VMEM((nt, t, t), BF16)],
        compiler_params=_params("parallel", "arbitrary", "arbitrary"),
        name="dsa_attention",
    )(jnp.asarray([s * LOG2E for s in _alibi_slopes(DSA_HEADS)], F32), proj, proj, wi, proj, proj, proj)


def _sb_kernel(q_ref, k_ref, v_ref, o_ref, later_ref, acc_ref, *, tq, tk):
    i = pl.program_id(2)
    r = tq // tk
    lo = _lane_lo()
    rel = _tile_rel(tq, tk)
    suffix = jnp.where(_tile_rel(tk, tk) >= 0, 1.0, 0.0).astype(BF16)
    qp = q_ref[0]
    qs = [jnp.where(lo if hh == 0 else jnp.logical_not(lo), qp, jnp.zeros_like(qp)) for hh in range(2)]

    def tile(j, diag):
        start = pl.multiple_of(j * tk, tk)
        k = k_ref[0, pl.ds(start, tk), :]
        v = v_ref[0, pl.ds(start, tk), :]
        hs = range(2)
        zs = [_dot_t(qs[hh], k) for hh in hs]
        lgs = []
        for z in zs:
            nz = -z
            lgs.append(jnp.minimum(nz, 0.0) - jnp.log2(1.0 + jnp.exp2(jnp.minimum(z, nz))))
        if diag is not None:
            strict = rel > diag * tk
            lgs = [jnp.where(strict, lg, 0.0) for lg in lgs]
        lbs = [lg.astype(BF16) for lg in lgs]
        runs = [_dot(lb, suffix) for lb in lbs]
        laters = [later_ref[hh] for hh in hs]
        a_s = [jnp.exp2(zs[hh] + runs[hh] + laters[hh]) for hh in hs]
        if diag is not None:
            a_s = [jnp.where(strict, a, 0.0) for a in a_s]
        for hh in hs:
            later_ref[hh] = laters[hh] + runs[hh][:, 0:1]
        for hh in hs:
            acc_ref[hh] += _dot(a_s[hh].astype(BF16), v)

    later_ref[...] = jnp.zeros(later_ref.shape, F32)
    acc_ref[...] = jnp.zeros(acc_ref.shape, F32)
    for d in reversed(range(r)):
        tile(i * r + d, d)

    @pl.loop(0, i * r)
    def _(jj):
        tile(i * r - 1 - jj, None)

    o_ref[0] = jnp.where(lo, acc_ref[0], acc_ref[1]).astype(BF16)


def _sb_attention(qkv, batch, seq, tq=512, tk=256):
    tq, tk = min(tq, seq), min(tk, seq)
    w = SB_HEADS * SB_HEAD_DIM
    pairs = SB_HEADS // 2
    return pl.pallas_call(
        functools.partial(_sb_kernel, tq=tq, tk=tk),
        grid=(batch, pairs, seq // tq),
        in_specs=[pl.BlockSpec((1, tq, LANES), lambda b, m, i: (b, i, m)),
                  pl.BlockSpec((1, seq, LANES), lambda b, m, i: (b, 0, pairs + m)),
                  pl.BlockSpec((1, seq, LANES), lambda b, m, i: (b, 0, 2 * pairs + m))],
        out_specs=pl.BlockSpec((1, tq, LANES), lambda b, m, i: (b, i, m)),
        out_shape=jax.ShapeDtypeStruct((batch, seq, w), BF16),
        scratch_shapes=[pltpu.VMEM((2, tq, 1), F32), pltpu.VMEM((2, tq, LANES), F32)],
        compiler_params=_params("parallel", "parallel", "arbitrary"),
        name="sb_attention",
    )(qkv, qkv, qkv)


def _dup_heads(w, n_heads, dim):
    d = w.shape[0]
    w = w.reshape(d, n_heads, 1, dim)
    return jnp.broadcast_to(w, (d, n_heads, 2, dim)).reshape(d, n_heads * 2 * dim)


def _pad_heads(w, n_heads, dim, to):
    d = w.shape[0]
    w = w.reshape(d, n_heads, dim)
    return jnp.pad(w, ((0, 0), (0, 0), (0, to - dim))).reshape(d, n_heads * to)


def _swa_w_in(w):
    qw = SWA_HEADS * SWA_HEAD_DIM
    kw = SWA_KV_HEADS * SWA_HEAD_DIM
    q, k, v = w[:, :qw], w[:, qw:qw + kw], w[:, qw + kw:]
    q = q * (LOG2E * SWA_HEAD_DIM ** -0.5)
    return jnp.concatenate([q, _dup_heads(k, SWA_KV_HEADS, SWA_HEAD_DIM),
                            _dup_heads(v, SWA_KV_HEADS, SWA_HEAD_DIM)], axis=1).astype(BF16)


def _mla_w_in(w):
    d = w.shape[0]
    lat = MLA_Q_LORA + MLA_KV_LORA
    z = lambda c: jnp.zeros((d, c), w.dtype)
    return jnp.concatenate([w[:, :lat], z(MLA_NOPE), w[:, lat:], z(LANES - MLA_NOPE - MLA_ROPE)],
                           axis=1).astype(BF16)


def _sb_w_in(w):
    qw = SB_HEADS * SB_HEAD_DIM
    return jnp.concatenate([w[:, :qw] * (LOG2E * SB_HEAD_DIM ** -0.5), w[:, qw:]], axis=1).astype(BF16)


def _dsa_w_in(w):
    qw = DSA_HEADS * DSA_HEAD_DIM
    kw = DSA_KV_HEADS * DSA_HEAD_DIM
    iw = IDX_HEADS * IDX_DIM
    c = np.cumsum([qw, kw, kw, iw, IDX_DIM]).tolist()
    q, k, v, qi, ki, wi = (w[:, :c[0]], w[:, c[0]:c[1]], w[:, c[1]:c[2]], w[:, c[2]:c[3]],
                           w[:, c[3]:c[4]], w[:, c[4]:])
    q = q * (LOG2E * DSA_HEAD_DIM ** -0.5)
    qi = qi * (IDX_DIM ** -0.5)
    main = jnp.concatenate([q, _dup_heads(k, DSA_KV_HEADS, DSA_HEAD_DIM),
                            _dup_heads(v, DSA_KV_HEADS, DSA_HEAD_DIM), qi, _dup_heads(ki, 1, IDX_DIM)],
                           axis=1).astype(BF16)
    wi = jnp.pad(wi, ((0, 0), (0, LANES - IDX_HEADS))).astype(BF16)
    return main, wi


def kernel(x, ffn1_w_gu, ffn1_w_down, ffn2_w_gu, ffn2_w_down, ln_g, ln_b, a_w_in, a_sinks, a_w_o,
           b_w_in, b_q_norm, b_w_uq, b_kv_norm, b_w_ukv, b_w_o, c_w_in, c_w_o, d_w_in, d_w_o):
    batch, seq, d = x.shape
    n = batch * seq
    x = x.reshape(n, d)
    for i in range(DEPTH):
        mixer, j = i % N_MIXERS, i // N_MIXERS

        def ffn1(projs):
            return _ffn_proj(x, ffn1_w_gu[i].astype(BF16), ffn1_w_down[i].astype(BF16), ln_g[i, 0],
                             ln_b[i, 0], projs)

        if mixer == 0:
            x, qkv = ffn1([(_swa_w_in(a_w_in[j]), BF16)])
            o = _swa_attention(qkv.reshape(batch, seq, -1), a_sinks[j] * LOG2E, batch, seq)
            w_o = a_w_o[j]
        elif mixer == 1:
            x, c = ffn1([(_mla_w_in(b_w_in[j]), F32)])
            hd = MLA_NOPE + MLA_ROPE
            wq = _pad_heads(b_w_uq[j], MLA_HEADS, hd, LANES).astype(BF16)
            wkv = b_w_ukv[j].reshape(MLA_KV_LORA, MLA_HEADS, MLA_NOPE + MLA_V)
            wk = _pad_heads(wkv[:, :, :MLA_NOPE].reshape(MLA_KV_LORA, -1), MLA_HEADS, MLA_NOPE,
                            LANES).astype(BF16)
            wv = wkv[:, :, MLA_NOPE:].reshape(MLA_KV_LORA, -1).astype(BF16)
            q, k, v = _mla_up(c, b_q_norm[j], b_kv_norm[j], wq, wk, wv, seq)
            o = _mla_attention(q.reshape(batch, seq, -1), k.reshape(batch, seq, -1),
                               v.reshape(batch, seq, -1))
            w_o = b_w_o[j]
        elif mixer == 2:
            w_main, w_wi = _dsa_w_in(c_w_in[j])
            x, proj, wi = ffn1([(w_main, BF16), (w_wi, F32)])
            o = _dsa_attention(proj.reshape(batch, seq, -1), wi.reshape(batch, seq, -1), batch, seq)
            w_o = c_w_o[j]
        else:
            x, qkv = ffn1([(_sb_w_in(d_w_in[j]), BF16)])
            o = _sb_attention(qkv.reshape(batch, seq, -1), batch, seq)
            w_o = d_w_o[j]
        x = _proj_ffn(o.reshape(n, -1), w_o.astype(BF16), x, ln_g[i, 1], ln_b[i, 1],
                      ffn2_w_gu[i].astype(BF16), ffn2_w_down[i].astype(BF16), ln_g[i, 2], ln_b[i, 2])
    return x.reshape(batch, seq, d)
```

```python
import functools
import math

import numpy as np
import jax
import jax.numpy as jnp
from jax import lax
from jax.experimental import pallas as pl
from jax.experimental.pallas import tpu as pltpu

D_MODEL = 1024
DEPTH = 4
N_MIXERS = 4
D_FF = 2816
LN_EPS = 1e-5
RMS_EPS = 1e-6
NEG = -1e30

SWA_HEADS, SWA_KV_HEADS, SWA_HEAD_DIM, SWA_BLOCK = 16, 4, 64, 128
MLA_HEADS, MLA_Q_LORA, MLA_KV_LORA, MLA_NOPE, MLA_ROPE, MLA_V = 16, 768, 256, 64, 32, 64
ROPE_THETA = 10000.0
DSA_HEADS, DSA_KV_HEADS, DSA_HEAD_DIM = 16, 4, 64
IDX_HEADS, IDX_DIM, IDX_TOPK = 8, 64, 256
SB_HEADS, SB_HEAD_DIM = 16, 64

ALPHA = (2.0 * DEPTH) ** 0.25
LOG2E = math.log2(math.e)

LANES = 128
HALF = 64
VMEM_LIMIT = 56 * 1024 * 1024
BF16 = jnp.bfloat16
F32 = jnp.float32


def _params(*sem, flags=None):
    return pltpu.CompilerParams(dimension_semantics=sem, vmem_limit_bytes=VMEM_LIMIT, flags=flags)


def _dot(a, b):
    return jnp.dot(a, b, preferred_element_type=F32)


def _dot_t(a, b):
    return lax.dot_general(a, b, (((1,), (1,)), ((), ())), preferred_element_type=F32)


def _alibi_slopes(n_heads):
    return [2.0 ** (-8.0 * (i + 1) / n_heads) for i in range(n_heads)]


def _layer_norm(y, g, b):
    mu = jnp.mean(y, axis=-1, keepdims=True)
    yc = y - mu
    var = jnp.mean(yc * yc, axis=-1, keepdims=True)
    return yc * lax.rsqrt(var + LN_EPS) * g + b


def _lane_lo():
    return lax.broadcasted_iota(jnp.int32, (1, LANES), 1) < HALF


def _resident(shape):
    return pl.BlockSpec(shape, lambda i: (0,) * len(shape), pipeline_mode=pl.Buffered(1))


def _rows(tm, width):
    return pl.BlockSpec((tm, width), lambda i: (i, 0))


def _ffn_ln_value(x, wgu_ref, wd_ref, g_ref, b_ref):
    f = wd_ref.shape[0]
    xb = x.astype(BF16)
    h = _dot(xb, wgu_ref[:, :f])
    u = _dot(xb, wgu_ref[:, f:])
    a = h * (1.0 / (1.0 + jnp.exp(-h))) * u
    y = ALPHA * x + 0.5 * _dot(a.astype(BF16), wd_ref[...])
    return _layer_norm(y, g_ref[...], b_ref[...])


def _ffn_proj_kernel(x_ref, wgu_ref, wd_ref, g_ref, b_ref, *refs, n_proj):
    w_refs, o_ref, p_refs = refs[:n_proj], refs[n_proj], refs[n_proj + 1:]
    y = _ffn_ln_value(x_ref[...], wgu_ref, wd_ref, g_ref, b_ref)
    o_ref[...] = y
    yb = y.astype(BF16)
    for w_ref, p_ref in zip(w_refs, p_refs):
        p_ref[...] = _dot(yb, w_ref[...]).astype(p_ref.dtype)


def _ffn_proj(x, w_gu, w_down, g, b, projs, tm=512):
    n, d = x.shape
    tm = min(tm, n)
    ws = [w for w, _ in projs]
    return pl.pallas_call(
        functools.partial(_ffn_proj_kernel, n_proj=len(projs)),
        grid=(n // tm,),
        in_specs=[_rows(tm, d), _resident(w_gu.shape), _resident(w_down.shape), _resident((1, d)),
                  _resident((1, d))] + [_resident(w.shape) for w in ws],
        out_specs=[_rows(tm, d)] + [_rows(tm, w.shape[1]) for w in ws],
        out_shape=[jax.ShapeDtypeStruct((n, d), F32)]
        + [jax.ShapeDtypeStruct((n, w.shape[1]), dt) for w, dt in projs],
        compiler_params=_params("parallel"),
        name="ffn_in_proj",
    )(x, w_gu, w_down, g.reshape(1, d), b.reshape(1, d), *ws)


def _proj_ffn_kernel(o_ref, wo_ref, x_ref, g1_ref, b1_ref, wgu_ref, wd_ref, g2_ref, b2_ref, out_ref):
    y = ALPHA * x_ref[...] + _dot(o_ref[...], wo_ref[...])
    x2 = _layer_norm(y, g1_ref[...], b1_ref[...])
    out_ref[...] = _ffn_ln_value(x2, wgu_ref, wd_ref, g2_ref, b2_ref)


def _proj_ffn(o, w_o, x, g1, b1, w_gu, w_down, g2, b2, tm=512):
    n, k = o.shape
    d = w_o.shape[1]
    tm = min(tm, n)
    vec = lambda a: a.reshape(1, d)
    return pl.pallas_call(
        _proj_ffn_kernel,
        grid=(n // tm,),
        in_specs=[_rows(tm, k), _resident(w_o.shape), _rows(tm, d), _resident((1, d)), _resident((1, d)),
                  _resident(w_gu.shape), _resident(w_down.shape), _resident((1, d)), _resident((1, d))],
        out_specs=_rows(tm, d),
        out_shape=jax.ShapeDtypeStruct((n, d), F32),
        compiler_params=_params("parallel"),
        name="out_proj_ffn",
    )(o, w_o, x, vec(g1), vec(b1), w_gu, w_down, vec(g2), vec(b2))


def _swa_kernel(sink_ref, q_ref, kc_ref, kp_ref, vc_ref, vp_ref, o_ref, *, tq):
    i = pl.program_id(1)
    t = SWA_BLOCK
    rep = SWA_HEADS // SWA_KV_HEADS
    kvs = range(SWA_KV_HEADS)
    lo = _lane_lo()
    rel = _tile_rel(t, 2 * t) + t
    col = lax.broadcasted_iota(jnp.int32, (t, 2 * t), 1)
    dist = rel.astype(F32)
    in_band = jnp.logical_and(rel >= 0, rel < SWA_BLOCK)
    slopes = [s * LOG2E for s in _alibi_slopes(SWA_HEADS)]
    lane = lambda g: slice(g * LANES, (g + 1) * LANES)
    for qb in range(tq // t):
        r0 = qb * t
        if qb == 0:
            ks = [jnp.concatenate([kp_ref[0, :, lane(g)], kc_ref[0, 0:t, lane(g)]], axis=0) for g in kvs]
            vs = [jnp.concatenate([vp_ref[0, :, lane(g)], vc_ref[0, 0:t, lane(g)]], axis=0) for g in kvs]
            valid = jnp.logical_and(in_band, col >= jnp.where(i == 0, t, 0))
        else:
            ks = [kc_ref[0, r0 - t:r0 + t, lane(g)] for g in kvs]
            vs = [vc_ref[0, r0 - t:r0 + t, lane(g)] for g in kvs]
            valid = in_band
        qst = []
        for g in kvs:
            parts = []
            for r in range(rep):
                h = g * rep + r
                qp = q_ref[0, r0:r0 + t, lane(h // 2)]
                parts.append(jnp.where(lo if h % 2 == 0 else jnp.logical_not(lo), qp, jnp.zeros_like(qp)))
            qst.append(jnp.concatenate(parts, axis=0))
        ss = [_dot_t(qst[g], ks[g]) for g in kvs]
        ps, dens = [], []
        for g in kvs:
            prow, drow = [], []
            for r in range(rep):
                h = g * rep + r
                s = jnp.where(valid, ss[g][r * t:(r + 1) * t] - slopes[h] * dist, NEG)
                sink = sink_ref[h]
                mx = jnp.maximum(jnp.max(s, axis=-1, keepdims=True), sink)
                p = jnp.exp2(s - mx)
                drow.append(jnp.sum(p, axis=-1, keepdims=True) + jnp.exp2(sink - mx))
                prow.append(p.astype(BF16))
            ps.append(jnp.concatenate(prow, axis=0))
            dens.append(drow)
        outs = [_dot(ps[g], vs[g]) for g in kvs]
        for g in kvs:
            for r in range(0, rep, 2):
                even = outs[g][r * t:(r + 1) * t] / dens[g][r]
                odd = outs[g][(r + 1) * t:(r + 2) * t] / dens[g][r + 1]
                o_ref[0, r0:r0 + t, lane((g * rep + r) // 2)] = jnp.where(lo, even, odd).astype(BF16)


def _swa_attention(qkv, sinks, batch, seq, tq=512):
    t = SWA_BLOCK
    tq = min(tq, seq)
    sub = tq // t
    qw = SWA_HEADS * SWA_HEAD_DIM
    kw = SWA_KV_HEADS * LANES
    qblk, kblk, vblk = 0, qw // kw, qw // kw + 1
    prev = lambda i: jnp.maximum(i * sub - 1, 0)
    return pl.pallas_call(
        functools.partial(_swa_kernel, tq=tq),
        grid=(batch, seq // tq),
        in_specs=[pl.BlockSpec(memory_space=pltpu.SMEM),
                  pl.BlockSpec((1, tq, qw), lambda b, i: (b, i, qblk)),
                  pl.BlockSpec((1, tq, kw), lambda b, i: (b, i, kblk)),
                  pl.BlockSpec((1, t, kw), lambda b, i: (b, prev(i), kblk)),
                  pl.BlockSpec((1, tq, kw), lambda b, i: (b, i, vblk)),
                  pl.BlockSpec((1, t, kw), lambda b, i: (b, prev(i), vblk))],
        out_specs=pl.BlockSpec((1, tq, qw), lambda b, i: (b, i, 0)),
        out_shape=jax.ShapeDtypeStruct((batch, seq, qw), BF16),
        compiler_params=_params("parallel", "arbitrary"),
        name="swa_attention",
    )(sinks, qkv, qkv, qkv, qkv, qkv)


def _mla_up_kernel(c_ref, gq_ref, gkv_ref, wq_ref, wk_ref, wv_ref, cq_ref, s1q_ref, s2q_ref,
                   ck_ref, s1k_ref, s2k_ref, q_ref, k_ref, v_ref):
    c = c_ref[...]
    cq = c[:, :MLA_Q_LORA]
    ckv = c[:, MLA_Q_LORA:MLA_Q_LORA + MLA_KV_LORA]
    kr = c[:, MLA_Q_LORA + MLA_KV_LORA:]
    cq = cq * lax.rsqrt(jnp.mean(cq * cq, axis=-1, keepdims=True) + RMS_EPS) * gq_ref[...]
    ckv = ckv * lax.rsqrt(jnp.mean(ckv * ckv, axis=-1, keepdims=True) + RMS_EPS) * gkv_ref[...]
    ckv_b = ckv.astype(BF16)
    q = _dot(cq.astype(BF16), wq_ref[...])
    kn = _dot(ckv_b, wk_ref[...])
    v_ref[...] = _dot(ckv_b, wv_ref[...]).astype(BF16)

    def rot(xh, cc, s1, s2):
        return xh * cc + pltpu.roll(xh, 16, 1) * s1 + pltpu.roll(xh, LANES - 16, 1) * s2

    kr = rot(kr, ck_ref[...], s1k_ref[...], s2k_ref[...])
    cq_t, s1q, s2q = cq_ref[...], s1q_ref[...], s2q_ref[...]
    for h in range(MLA_HEADS):
        sl = slice(h * LANES, (h + 1) * LANES)
        q_ref[:, sl] = rot(q[:, sl], cq_t, s1q, s2q).astype(BF16)
        k_ref[:, sl] = (kn[:, sl] + kr).astype(BF16)


def _mla_rope_tables(seq, scale):
    half = MLA_ROPE // 2
    inv = ROPE_THETA ** (-jnp.arange(0, MLA_ROPE, 2, dtype=F32) / MLA_ROPE)
    ang = jnp.arange(seq, dtype=F32)[:, None] * inv[None, :]
    cos, sin = jnp.cos(ang), jnp.sin(ang)
    z = lambda w: jnp.zeros((seq, w), F32)
    one = jnp.ones((seq, MLA_NOPE), F32)
    cc = jnp.concatenate([one, cos, cos, z(LANES - MLA_NOPE - MLA_ROPE)], axis=1) * scale
    s1 = jnp.concatenate([z(MLA_NOPE + half), sin, z(LANES - MLA_NOPE - MLA_ROPE)], axis=1) * scale
    s2 = jnp.concatenate([z(MLA_NOPE), -sin, z(LANES - MLA_NOPE - half)], axis=1) * scale
    return cc, s1, s2


def _mla_up(c, gq, gkv, wq, wk, wv, seq, tm=512):
    n, cw = c.shape
    tm = min(tm, seq)
    spt = seq // tm
    scale = LOG2E * (MLA_NOPE + MLA_ROPE) ** -0.5
    tq = _mla_rope_tables(seq, scale)
    tk = _mla_rope_tables(seq, 1.0)
    qk_w = MLA_HEADS * LANES
    v_w = MLA_HEADS * MLA_V
    full = lambda a: _resident(a.shape)
    tab = pl.BlockSpec((tm, LANES), lambda i: (i % spt, 0))
    gq2, gkv2 = gq.reshape(1, -1), gkv.reshape(1, -1)
    return pl.pallas_call(
        _mla_up_kernel,
        grid=(n // tm,),
        in_specs=[pl.BlockSpec((tm, cw), lambda i: (i, 0)), full(gq2), full(gkv2), full(wq), full(wk),
                  full(wv), tab, tab, tab, tab, tab, tab],
        out_specs=[pl.BlockSpec((tm, qk_w), lambda i: (i, 0)),
                   pl.BlockSpec((tm, qk_w), lambda i: (i, 0)),
                   pl.BlockSpec((tm, v_w), lambda i: (i, 0))],
        out_shape=[jax.ShapeDtypeStruct((n, qk_w), BF16), jax.ShapeDtypeStruct((n, qk_w), BF16),
                   jax.ShapeDtypeStruct((n, v_w), BF16)],
        compiler_params=_params("parallel"),
        name="mla_up_proj",
    )(c, gq2, gkv2, wq, wk, wv, *tq, *tk)


def _flash_init(tq):
    one = (jnp.full((tq, 1), NEG, F32), jnp.zeros((tq, 1), F32))
    return (one, one)


def _flash_pair(ss, v, carry, acc_ref):
    hs = range(2)
    ms = [jnp.maximum(carry[hh][0], jnp.max(ss[hh], axis=-1, keepdims=True)) for hh in hs]
    ps = [jnp.exp2(ss[hh] - ms[hh]) for hh in hs]
    al = [jnp.exp2(carry[hh][0] - ms[hh]) for hh in hs]
    ls = [al[hh] * carry[hh][1] + jnp.sum(ps[hh], axis=-1, keepdims=True) for hh in hs]
    for hh in hs:
        acc_ref[hh] = al[hh] * acc_ref[hh] + _dot(ps[hh].astype(BF16), v)
    return tuple((ms[hh], ls[hh]) for hh in hs)


SOFTMAX_ROWS = 32


def _flash_scratch(tq, tk):
    return [pltpu.VMEM((2, tq, tk), F32), pltpu.VMEM((2, tq, tk), BF16), pltpu.VMEM((2, 2, tq, LANES), F32),
            pltpu.VMEM((2, tq, LANES), F32)]


def _flash_reset(st_ref, acc_ref):
    st_ref[:, 0] = jnp.full(st_ref.shape[:1] + st_ref.shape[2:], NEG, F32)
    st_ref[:, 1] = jnp.zeros(st_ref.shape[:1] + st_ref.shape[2:], F32)
    acc_ref[...] = jnp.zeros(acc_ref.shape, F32)


def _flash_update(s_ref, p_ref, st_ref, acc_ref, v, extra=None):
    _, tq, tk = s_ref.shape
    for c in range(tq // SOFTMAX_ROWS):
        rows = slice(c * SOFTMAX_ROWS, (c + 1) * SOFTMAX_ROWS)
        for hh in range(2):
            s = s_ref[hh, rows, :]
            if extra is not None:
                s = s + extra(hh, rows)
            m_old = st_ref[hh, 0, rows, :]
            m_new = jnp.maximum(m_old, jnp.max(s, axis=-1, keepdims=True))
            al = jnp.exp2(m_old - m_new)
            ps = [jnp.exp2(s[:, c0:c0 + LANES] - m_new) for c0 in range(0, tk, LANES)]
            tot = ps[0]
            for p in ps[1:]:
                tot = tot + p
            st_ref[hh, 0, rows, :] = m_new
            st_ref[hh, 1, rows, :] = al * st_ref[hh, 1, rows, :] + jnp.sum(tot, axis=-1, keepdims=True)
            acc_ref[hh, rows, :] = al * acc_ref[hh, rows, :]
            for n, p in enumerate(ps):
                p_ref[hh, rows, n * LANES:(n + 1) * LANES] = p.astype(BF16)
    for hh in range(2):
        acc_ref[hh] += _dot(p_ref[hh], v)


def _tile_rel(tq, tk):
    return (lax.broadcasted_iota(jnp.int32, (tq, tk), 0) - lax.broadcasted_iota(jnp.int32, (tq, tk), 1))


def _mla_attn_kernel(q_ref, k_ref, v_ref, o_ref, s_ref, p_ref, st_ref, acc_ref, *, tq, tk):
    i = pl.program_id(2)
    r = tq // tk
    rel = _tile_rel(tq, tk)
    sls = [slice(hh * LANES, (hh + 1) * LANES) for hh in range(2)]

    def tile(j, diag):
        start = pl.multiple_of(j * tk, tk)
        for hh in range(2):
            s = _dot_t(q_ref[0, :, sls[hh]], k_ref[0, pl.ds(start, tk), sls[hh]])
            if diag is not None:
                s = jnp.where(rel >= diag * tk, s, NEG)
            s_ref[hh] = s
        _flash_update(s_ref, p_ref, st_ref, acc_ref, v_ref[0, pl.ds(start, tk), :])

    _flash_reset(st_ref, acc_ref)
    for d in range(r):
        tile(i * r + d, d)

    @pl.loop(0, i * r)
    def _(j):
        tile(j, None)

    outs = [acc_ref[hh] / st_ref[hh, 1] for hh in range(2)]
    o_ref[0] = jnp.where(_lane_lo(), outs[0], outs[1]).astype(BF16)


def _mla_attention(q, k, v, tq=512, tk=512):
    batch, seq, _ = q.shape
    tq, tk = min(tq, seq), min(tk, seq)
    return pl.pallas_call(
        functools.partial(_mla_attn_kernel, tq=tq, tk=tk),
        grid=(batch, MLA_HEADS // 2, seq // tq),
        in_specs=[pl.BlockSpec((1, tq, 2 * LANES), lambda b, m, i: (b, i, m)),
                  pl.BlockSpec((1, seq, 2 * LANES), lambda b, m, i: (b, 0, m)),
                  pl.BlockSpec((1, seq, LANES), lambda b, m, i: (b, 0, m))],
        out_specs=pl.BlockSpec((1, tq, LANES), lambda b, m, i: (b, i, m)),
        out_shape=jax.ShapeDtypeStruct(v.shape, BF16),
        scratch_shapes=_flash_scratch(tq, tk),
        compiler_params=_params("parallel", "parallel", "arbitrary"),
        name="mla_attention",
    )(q, k, v)


def _order_key(x):
    bits = lax.bitcast_convert_type(x, jnp.int32)
    return bits ^ (lax.shift_right_arithmetic(bits, 31) & 0x7FFFFFFF)


def _dsa_kernel(slope_ref, qi_ref, ki_ref, wi_ref, q_ref, k_ref, v_ref, o_ref, key_ref, bias_ref,
                dist_ref, byte_ref, s_ref, p_ref, st_ref, acc_ref, *, t, topk):
    i = pl.program_id(1)
    m = pl.program_id(2)
    lo = _lane_lo()
    row = lax.broadcasted_iota(jnp.int32, (t, t), 0)
    col = lax.broadcasted_iota(jnp.int32, (t, t), 1)
    rel = row - col

    @pl.when(m == 0)
    def _select():
        wi = wi_ref[0] * (IDX_HEADS ** -0.5)

        def score_tile(j, carry):
            start = pl.multiple_of(j * t, t)
            ki = ki_ref[0, pl.ds(start, t), :]
            sc = jnp.zeros((t, t), F32)
            for p in range(IDX_HEADS // 2):
                qp = qi_ref[0, :, p * LANES:(p + 1) * LANES]
                for hh in range(2):
                    h = 2 * p + hh
                    qm = jnp.where(lo if hh == 0 else jnp.logical_not(lo), qp, jnp.zeros_like(qp))
                    dots = _dot_t(qm, ki)
                    sc = sc + wi[:, h:h + 1] * jnp.maximum(dots, 0.0)
            sc = sc + 0.0
            sc = jnp.where(rel >= (j - i) * t, sc, NEG)
            key_ref[j] = _order_key(sc)
            dist_ref[j] = ((i - j) * t + rel).astype(F32)
            return carry

        lax.fori_loop(0, i + 1, score_tile, 0)

        one_b, zero_b = jnp.ones((), BF16), jnp.zeros((), BF16)

        def count_ge(cand):
            cand_b = cand.astype(BF16)

            def body(j, part):
                ge = jnp.where(byte_ref[j] >= cand_b, one_b, zero_b)
                for c0 in range(0, t, LANES):
                    part = part + ge[:, c0:c0 + LANES]
                return part
            part = lax.fori_loop(0, i + 1, body, jnp.zeros((t, LANES), BF16))
            return jnp.sum(part.astype(F32), axis=-1, keepdims=True)

        prefix = jnp.zeros((t, 1), jnp.int32)
        above = jnp.zeros((t, 1), F32)
        for stage in range(4):
            shift = 24 - 8 * stage

            def fill(j, carry, stage=stage, shift=shift, prefix=prefix):
                key = key_ref[j]
                if stage == 0:
                    byte = lax.shift_right_arithmetic(key, shift) + 128
                else:
                    match = lax.shift_right_arithmetic(key, shift + 8) == prefix
                    byte = jnp.where(match, lax.shift_right_logical(key, shift) & 255, -1)
                byte_ref[j] = byte.astype(F32).astype(BF16)
                return carry

            lax.fori_loop(0, i + 1, fill, 0)
            val = jnp.zeros((t, 1), F32)
            rej = jnp.zeros((t, 1), F32)
            for bit in (128, 64, 32, 16, 8, 4, 2, 1):
                cand = val + float(bit)
                cnt = count_ge(cand)
                ok = above + cnt >= topk
                val = jnp.where(ok, cand, val)
                rej = jnp.where(ok, rej, cnt)
            above = above + rej
            byte_i = val.astype(jnp.int32) - (128 if stage == 0 else 0)
            prefix = byte_i if stage == 0 else lax.shift_left(prefix, 8) | byte_i
        thr = prefix
        need = topk - above
        tri = jnp.where(row < col, 1.0, 0.0).astype(BF16)

        def select_tile(j, seen):
            key = key_ref[j]
            eq = key == thr
            eq_b = jnp.where(eq, 1.0, 0.0).astype(BF16)
            rank = seen + _dot(eq_b, tri)
            sel = jnp.logical_or(key > thr, jnp.logical_and(eq, rank < need))
            sel = jnp.logical_and(sel, rel >= (j - i) * t)
            bias_ref[j] = jnp.where(sel, 0.0, NEG)
            return seen + jnp.sum(eq_b.astype(F32), axis=-1, keepdims=True)

        lax.fori_loop(0, i + 1, select_tile, jnp.zeros((t, 1), F32))

    qp = q_ref[0]
    qs = [jnp.where(lo if hh == 0 else jnp.logical_not(lo), qp, jnp.zeros_like(qp)) for hh in range(2)]
    slopes = [slope_ref[2 * m + hh] for hh in range(2)]

    def tile(j):
        start = pl.multiple_of(j * t, t)
        k = k_ref[0, pl.ds(start, t), :]
        for hh in range(2):
            s_ref[hh] = _dot_t(qs[hh], k)
        extra = lambda hh, rows: bias_ref[j, rows, :] - slopes[hh] * dist_ref[j, rows, :]
        _flash_update(s_ref, p_ref, st_ref, acc_ref, v_ref[0, pl.ds(start, t), :], extra)

    _flash_reset(st_ref, acc_ref)
    tile(i)
    pl.loop(0, i)(tile)
    outs = [acc_ref[hh] / st_ref[hh, 1] for hh in range(2)]
    o_ref[0] = jnp.where(lo, outs[0], outs[1]).astype(BF16)


def _dsa_attention(proj, wi, batch, seq, t=512):
    t = min(t, seq)
    nt = seq // t
    topk = min(IDX_TOPK, seq // 4)
    qw = DSA_HEADS * DSA_HEAD_DIM
    q0 = 0
    k0 = qw // LANES
    v0 = k0 + DSA_KV_HEADS
    qi0 = (v0 + DSA_KV_HEADS) * LANES // (IDX_HEADS * IDX_DIM)
    ki0 = v0 + DSA_KV_HEADS + IDX_HEADS * IDX_DIM // LANES
    pairs = DSA_HEADS // 2
    per_kv = pairs // DSA_KV_HEADS
    return pl.pallas_call(
        functools.partial(_dsa_kernel, t=t, topk=topk),
        grid=(batch, nt, pairs),
        in_specs=[pl.BlockSpec(memory_space=pltpu.SMEM),
                  pl.BlockSpec((1, t, IDX_HEADS * IDX_DIM), lambda b, i, m: (b, i, qi0)),
                  pl.BlockSpec((1, seq, LANES), lambda b, i, m: (b, 0, ki0)),
                  pl.BlockSpec((1, t, LANES), lambda b, i, m: (b, i, 0)),
                  pl.BlockSpec((1, t, LANES), lambda b, i, m: (b, i, q0 + m)),
                  pl.BlockSpec((1, seq, LANES), lambda b, i, m: (b, 0, k0 + m // per_kv)),
                  pl.BlockSpec((1, seq, LANES), lambda b, i, m: (b, 0, v0 + m // per_kv))],
        out_specs=pl.BlockSpec((1, t, LANES), lambda b, i, m: (b, i, m)),
        out_shape=jax.ShapeDtypeStruct((batch, seq, qw), BF16),
        scratch_shapes=[pltpu.VMEM((nt, t, t), jnp.int32), pltpu.VMEM((nt, t, t), F32),
                        pltpu.VMEM((nt, t, t), F32), pltpu.VMEM((nt, t, t), BF16)] + _flash_scratch(t, t),
        compiler_params=_params("parallel", "arbitrary", "arbitrary"),
        name="dsa_attention",
    )(jnp.asarray([s * LOG2E for s in _alibi_slopes(DSA_HEADS)], F32), proj, proj, wi, proj, proj, proj)


SB_ROWS = 64


def _sb_kernel(q_ref, k_ref, v_ref, o_ref, z_ref, lb_ref, run_ref, p_ref, later_ref, acc_ref, *, tq, tk):
    i = pl.program_id(2)
    r = tq // tk
    lo = _lane_lo()

    def rel_of(rows, cols):
        return _tile_rel(rows.stop - rows.start, cols.stop - cols.start) + (rows.start - cols.start)

    suffix = jnp.where(_tile_rel(tk, tk) >= 0, 1.0, 0.0).astype(BF16)
    qp = q_ref[0]
    qs = [jnp.where(lo if hh == 0 else jnp.logical_not(lo), qp, jnp.zeros_like(qp)) for hh in range(2)]

    chunks = [slice(c, c + SB_ROWS) for c in range(0, tq, SB_ROWS)]
    lanes = [slice(c, c + LANES) for c in range(0, tk, LANES)]

    def tile(j, diag):
        start = pl.multiple_of(j * tk, tk)
        k = k_ref[0, pl.ds(start, tk), :]
        hs = range(2)
        for hh in hs:
            z_ref[hh] = _dot_t(qs[hh], k)
        for rows in chunks:
            for hh in hs:
                z = z_ref[hh, rows, :]
                nz = -z
                lg = jnp.minimum(nz, 0.0) - jnp.log2(1.0 + jnp.exp2(jnp.minimum(z, nz)))
                if diag is not None:
                    lg = jnp.where(rel_of(rows, slice(0, tk)) > diag * tk, lg, 0.0)
                lb_ref[hh, rows, :] = lg.astype(BF16)
        for hh in hs:
            run_ref[hh] = _dot(lb_ref[hh], suffix)
        for rows in chunks:
            for hh in hs:
                later = later_ref[hh, rows, :]
                for ln in lanes:
                    a = jnp.exp2(z_ref[hh, rows, ln] + run_ref[hh, rows, ln] + later)
                    if diag is not None:
                        a = jnp.where(rel_of(rows, ln) > diag * tk, a, 0.0)
                    p_ref[hh, rows, ln] = a.astype(BF16)
                later_ref[hh, rows, :] = later + jnp.broadcast_to(run_ref[hh, rows, 0:1], later.shape)
        v = v_ref[0, pl.ds(start, tk), :]
        for hh in hs:
            acc_ref[hh] += _dot(p_ref[hh], v)

    later_ref[...] = jnp.zeros(later_ref.shape, F32)
    acc_ref[...] = jnp.zeros(acc_ref.shape, F32)
    for d in reversed(range(r)):
        tile(i * r + d, d)

    @pl.loop(0, i * r)
    def _(jj):
        tile(i * r - 1 - jj, None)

    o_ref[0] = jnp.where(lo, acc_ref[0], acc_ref[1]).astype(BF16)


def _sb_attention(qkv, batch, seq, tq=512, tk=256):
    tq, tk = min(tq, seq), min(tk, seq)
    w = SB_HEADS * SB_HEAD_DIM
    pairs = SB_HEADS // 2
    return pl.pallas_call(
        functools.partial(_sb_kernel, tq=tq, tk=tk),
        grid=(batch, pairs, seq // tq),
        in_specs=[pl.BlockSpec((1, tq, LANES), lambda b, m, i: (b, i, m)),
                  pl.BlockSpec((1, seq, LANES), lambda b, m, i: (b, 0, pairs + m)),
                  pl.BlockSpec((1, seq, LANES), lambda b, m, i: (b, 0, 2 * pairs + m))],
        out_specs=pl.BlockSpec((1, tq, LANES), lambda b, m, i: (b, i, m)),
        out_shape=jax.ShapeDtypeStruct((batch, seq, w), BF16),
        scratch_shapes=[pltpu.VMEM((2, tq, tk), F32), pltpu.VMEM((2, tq, tk), BF16),
                        pltpu.VMEM((2, tq, tk), F32), pltpu.VMEM((2, tq, tk), BF16),
                        pltpu.VMEM((2, tq, LANES), F32), pltpu.VMEM((2, tq, LANES), F32)],
        compiler_params=_params("parallel", "parallel", "arbitrary"),
        name="sb_attention",
    )(qkv, qkv, qkv)


def _dup_heads(w, n_heads, dim):
    d = w.shape[0]
    w = w.reshape(d, n_heads, 1, dim)
    return jnp.broadcast_to(w, (d, n_heads, 2, dim)).reshape(d, n_heads * 2 * dim)


def _pad_heads(w, n_heads, dim, to):
    d = w.shape[0]
    w = w.reshape(d, n_heads, dim)
    return jnp.pad(w, ((0, 0), (0, 0), (0, to - dim))).reshape(d, n_heads * to)


def _swa_w_in(w):
    qw = SWA_HEADS * SWA_HEAD_DIM
    kw = SWA_KV_HEADS * SWA_HEAD_DIM
    q, k, v = w[:, :qw], w[:, qw:qw + kw], w[:, qw + kw:]
    q = q * (LOG2E * SWA_HEAD_DIM ** -0.5)
    return jnp.concatenate([q, _dup_heads(k, SWA_KV_HEADS, SWA_HEAD_DIM),
                            _dup_heads(v, SWA_KV_HEADS, SWA_HEAD_DIM)], axis=1).astype(BF16)


def _mla_w_in(w):
    d = w.shape[0]
    lat = MLA_Q_LORA + MLA_KV_LORA
    z = lambda c: jnp.zeros((d, c), w.dtype)
    return jnp.concatenate([w[:, :lat], z(MLA_NOPE), w[:, lat:], z(LANES - MLA_NOPE - MLA_ROPE)],
                           axis=1).astype(BF16)


def _sb_w_in(w):
    qw = SB_HEADS * SB_HEAD_DIM
    return jnp.concatenate([w[:, :qw] * (LOG2E * SB_HEAD_DIM ** -0.5), w[:, qw:]], axis=1).astype(BF16)


def _dsa_w_in(w):
    qw = DSA_HEADS * DSA_HEAD_DIM
    kw = DSA_KV_HEADS * DSA_HEAD_DIM
    iw = IDX_HEADS * IDX_DIM
    c = np.cumsum([qw, kw, kw, iw, IDX_DIM]).tolist()
    q, k, v, qi, ki, wi = (w[:, :c[0]], w[:, c[0]:c[1]], w[:, c[1]:c[2]], w[:, c[2]:c[3]],
                           w[:, c[3]:c[4]], w[:, c[4]:])
    q = q * (LOG2E * DSA_HEAD_DIM ** -0.5)
    qi = qi * (IDX_DIM ** -0.5)
    main = jnp.concatenate([q, _dup_heads(k, DSA_KV_HEADS, DSA_HEAD_DIM),
                            _dup_heads(v, DSA_KV_HEADS, DSA_HEAD_DIM), qi, _dup_heads(ki, 1, IDX_DIM)],
                           axis=1).astype(BF16)
    wi = jnp.pad(wi, ((0, 0), (0, LANES - IDX_HEADS))).astype(BF16)
    return main, wi


def kernel(x, ffn1_w_gu, ffn1_w_down, ffn2_w_gu, ffn2_w_down, ln_g, ln_b, a_w_in, a_sinks, a_w_o,
           b_w_in, b_q_norm, b_w_uq, b_kv_norm, b_w_ukv, b_w_o, c_w_in, c_w_o, d_w_in, d_w_o):
    batch, seq, d = x.shape
    n = batch * seq
    x = x.reshape(n, d)
    for i in range(DEPTH):
        mixer, j = i % N_MIXERS, i // N_MIXERS

        def ffn1(projs):
            return _ffn_proj(x, ffn1_w_gu[i].astype(BF16), ffn1_w_down[i].astype(BF16), ln_g[i, 0],
                             ln_b[i, 0], projs)

        if mixer == 0:
            x, qkv = ffn1([(_swa_w_in(a_w_in[j]), BF16)])
            o = _swa_attention(qkv.reshape(batch, seq, -1), a_sinks[j] * LOG2E, batch, seq)
            w_o = a_w_o[j]
        elif mixer == 1:
            x, c = ffn1([(_mla_w_in(b_w_in[j]), F32)])
            hd = MLA_NOPE + MLA_ROPE
            wq = _pad_heads(b_w_uq[j], MLA_HEADS, hd, LANES).astype(BF16)
            wkv = b_w_ukv[j].reshape(MLA_KV_LORA, MLA_HEADS, MLA_NOPE + MLA_V)
            wk = _pad_heads(wkv[:, :, :MLA_NOPE].reshape(MLA_KV_LORA, -1), MLA_HEADS, MLA_NOPE,
                            LANES).astype(BF16)
            wv = wkv[:, :, MLA_NOPE:].reshape(MLA_KV_LORA, -1).astype(BF16)
            q, k, v = _mla_up(c, b_q_norm[j], b_kv_norm[j], wq, wk, wv, seq)
            o = _mla_attention(q.reshape(batch, seq, -1), k.reshape(batch, seq, -1),
                               v.reshape(batch, seq, -1))
            w_o = b_w_o[j]
        elif mixer == 2:
            w_main, w_wi = _dsa_w_in(c_w_in[j])
            x, proj, wi = ffn1([(w_main, BF16), (w_wi, F32)])
            o = _dsa_attention(proj.reshape(batch, seq, -1), wi.reshape(batch, seq, -1), batch, seq)
            w_o = c_w_o[j]
        else:
            x, qkv = ffn1([(_sb_w_in(d_w_in[j]), BF16)])
            o = _sb_attention(qkv.reshape(batch, seq, -1), batch, seq)
            w_o = d_w_o[j]
        x = _proj_ffn(o.reshape(n, -1), w_o.astype(BF16), x, ln_g[i, 1], ln_b[i, 1],
                      ffn2_w_gu[i].astype(BF16), ffn2_w_down[i].astype(BF16), ln_g[i, 2], ln_b[i, 2])
    return x.reshape(batch, seq, d)
```

```python
import functools
import math

import numpy as np
import jax
import jax.numpy as jnp
from jax import lax
from jax.experimental import pallas as pl
from jax.experimental.pallas import tpu as pltpu

D_MODEL = 1024
DEPTH = 4
N_MIXERS = 4
D_FF = 2816
LN_EPS = 1e-5
RMS_EPS = 1e-6
NEG = -1e30

SWA_HEADS, SWA_KV_HEADS, SWA_HEAD_DIM, SWA_BLOCK = 16, 4, 64, 128
MLA_HEADS, MLA_Q_LORA, MLA_KV_LORA, MLA_NOPE, MLA_ROPE, MLA_V = 16, 768, 256, 64, 32, 64
ROPE_THETA = 10000.0
DSA_HEADS, DSA_KV_HEADS, DSA_HEAD_DIM = 16, 4, 64
IDX_HEADS, IDX_DIM, IDX_TOPK = 8, 64, 256
SB_HEADS, SB_HEAD_DIM = 16, 64

ALPHA = (2.0 * DEPTH) ** 0.25
LOG2E = math.log2(math.e)

LANES = 128
HALF = 64
VMEM_LIMIT = 56 * 1024 * 1024
BF16 = jnp.bfloat16
F32 = jnp.float32


def _params(*sem, flags=None):
    return pltpu.CompilerParams(dimension_semantics=sem, vmem_limit_bytes=VMEM_LIMIT, flags=flags)


def _dot(a, b):
    return jnp.dot(a, b, preferred_element_type=F32)


def _dot_t(a, b):
    return lax.dot_general(a, b, (((1,), (1,)), ((), ())), preferred_element_type=F32)


def _alibi_slopes(n_heads):
    return [2.0 ** (-8.0 * (i + 1) / n_heads) for i in range(n_heads)]


def _layer_norm(y, g, b):
    mu = jnp.mean(y, axis=-1, keepdims=True)
    yc = y - mu
    var = jnp.mean(yc * yc, axis=-1, keepdims=True)
    return yc * lax.rsqrt(var + LN_EPS) * g + b


def _lane_lo():
    return lax.broadcasted_iota(jnp.int32, (1, LANES), 1) < HALF


def _resident(shape):
    return pl.BlockSpec(shape, lambda i: (0,) * len(shape), pipeline_mode=pl.Buffered(1))


def _rows(tm, width):
    return pl.BlockSpec((tm, width), lambda i: (i, 0))


def _ffn_ln_value(x, wgu_ref, wd_ref, g_ref, b_ref):
    f = wd_ref.shape[0]
    xb = x.astype(BF16)
    h = _dot(xb, wgu_ref[:, :f])
    u = _dot(xb, wgu_ref[:, f:])
    a = h * (1.0 / (1.0 + jnp.exp(-h))) * u
    y = ALPHA * x + 0.5 * _dot(a.astype(BF16), wd_ref[...])
    return _layer_norm(y, g_ref[...], b_ref[...])


def _ffn_proj_kernel(x_ref, wgu_ref, wd_ref, g_ref, b_ref, *refs, n_proj):
    w_refs, o_ref, p_refs = refs[:n_proj], refs[n_proj], refs[n_proj + 1:]
    y = _ffn_ln_value(x_ref[...], wgu_ref, wd_ref, g_ref, b_ref)
    o_ref[...] = y
    yb = y.astype(BF16)
    for w_ref, p_ref in zip(w_refs, p_refs):
        p_ref[...] = _dot(yb, w_ref[...]).astype(p_ref.dtype)


def _ffn_proj(x, w_gu, w_down, g, b, projs, tm=512):
    n, d = x.shape
    tm = min(tm, n)
    ws = [w for w, _ in projs]
    return pl.pallas_call(
        functools.partial(_ffn_proj_kernel, n_proj=len(projs)),
        grid=(n // tm,),
        in_specs=[_rows(tm, d), _resident(w_gu.shape), _resident(w_down.shape), _resident((1, d)),
                  _resident((1, d))] + [_resident(w.shape) for w in ws],
        out_specs=[_rows(tm, d)] + [_rows(tm, w.shape[1]) for w in ws],
        out_shape=[jax.ShapeDtypeStruct((n, d), F32)]
        + [jax.ShapeDtypeStruct((n, w.shape[1]), dt) for w, dt in projs],
        compiler_params=_params("parallel"),
        name="ffn_in_proj",
    )(x, w_gu, w_down, g.reshape(1, d), b.reshape(1, d), *ws)


def _proj_ffn_kernel(o_ref, wo_ref, x_ref, g1_ref, b1_ref, wgu_ref, wd_ref, g2_ref, b2_ref, out_ref):
    y = ALPHA * x_ref[...] + _dot(o_ref[...], wo_ref[...])
    x2 = _layer_norm(y, g1_ref[...], b1_ref[...])
    out_ref[...] = _ffn_ln_value(x2, wgu_ref, wd_ref, g2_ref, b2_ref)


def _proj_ffn(o, w_o, x, g1, b1, w_gu, w_down, g2, b2, tm=512):
    n, k = o.shape
    d = w_o.shape[1]
    tm = min(tm, n)
    vec = lambda a: a.reshape(1, d)
    return pl.pallas_call(
        _proj_ffn_kernel,
        grid=(n // tm,),
        in_specs=[_rows(tm, k), _resident(w_o.shape), _rows(tm, d), _resident((1, d)), _resident((1, d)),
                  _resident(w_gu.shape), _resident(w_down.shape), _resident((1, d)), _resident((1, d))],
        out_specs=_rows(tm, d),
        out_shape=jax.ShapeDtypeStruct((n, d), F32),
        compiler_params=_params("parallel"),
        name="out_proj_ffn",
    )(o, w_o, x, vec(g1), vec(b1), w_gu, w_down, vec(g2), vec(b2))


def _swa_kernel(sink_ref, q_ref, kc_ref, kp_ref, vc_ref, vp_ref, o_ref, *, tq):
    i = pl.program_id(1)
    t = SWA_BLOCK
    rep = SWA_HEADS // SWA_KV_HEADS
    kvs = range(SWA_KV_HEADS)
    lo = _lane_lo()
    rel = _tile_rel(t, 2 * t) + t
    col = lax.broadcasted_iota(jnp.int32, (t, 2 * t), 1)
    dist = rel.astype(F32)
    in_band = jnp.logical_and(rel >= 0, rel < SWA_BLOCK)
    slopes = [s * LOG2E for s in _alibi_slopes(SWA_HEADS)]
    lane = lambda g: slice(g * LANES, (g + 1) * LANES)
    for qb in range(tq // t):
        r0 = qb * t
        if qb == 0:
            ks = [jnp.concatenate([kp_ref[0, :, lane(g)], kc_ref[0, 0:t, lane(g)]], axis=0) for g in kvs]
            vs = [jnp.concatenate([vp_ref[0, :, lane(g)], vc_ref[0, 0:t, lane(g)]], axis=0) for g in kvs]
            valid = jnp.logical_and(in_band, col >= jnp.where(i == 0, t, 0))
        else:
            ks = [kc_ref[0, r0 - t:r0 + t, lane(g)] for g in kvs]
            vs = [vc_ref[0, r0 - t:r0 + t, lane(g)] for g in kvs]
            valid = in_band
        qst = []
        for g in kvs:
            parts = []
            for r in range(rep):
                h = g * rep + r
                qp = q_ref[0, r0:r0 + t, lane(h // 2)]
                parts.append(jnp.where(lo if h % 2 == 0 else jnp.logical_not(lo), qp, jnp.zeros_like(qp)))
            qst.append(jnp.concatenate(parts, axis=0))
        ss = [_dot_t(qst[g], ks[g]) for g in kvs]
        ps, dens = [], []
        for g in kvs:
            prow, drow = [], []
            for r in range(rep):
                h = g * rep + r
                s = jnp.where(valid, ss[g][r * t:(r + 1) * t] - slopes[h] * dist, NEG)
                sink = sink_ref[h]
                mx = jnp.maximum(jnp.max(s, axis=-1, keepdims=True), sink)
                p = jnp.exp2(s - mx)
                drow.append(jnp.sum(p, axis=-1, keepdims=True) + jnp.exp2(sink - mx))
                prow.append(p.astype(BF16))
            ps.append(jnp.concatenate(prow, axis=0))
            dens.append(drow)
        outs = [_dot(ps[g], vs[g]) for g in kvs]
        for g in kvs:
            for r in range(0, rep, 2):
                even = outs[g][r * t:(r + 1) * t] / dens[g][r]
                odd = outs[g][(r + 1) * t:(r + 2) * t] / dens[g][r + 1]
                o_ref[0, r0:r0 + t, lane((g * rep + r) // 2)] = jnp.where(lo, even, odd).astype(BF16)


def _swa_attention(qkv, sinks, batch, seq, tq=512):
    t = SWA_BLOCK
    tq = min(tq, seq)
    sub = tq // t
    qw = SWA_HEADS * SWA_HEAD_DIM
    kw = SWA_KV_HEADS * LANES
    qblk, kblk, vblk = 0, qw // kw, qw // kw + 1
    prev = lambda i: jnp.maximum(i * sub - 1, 0)
    return pl.pallas_call(
        functools.partial(_swa_kernel, tq=tq),
        grid=(batch, seq // tq),
        in_specs=[pl.BlockSpec(memory_space=pltpu.SMEM),
                  pl.BlockSpec((1, tq, qw), lambda b, i: (b, i, qblk)),
                  pl.BlockSpec((1, tq, kw), lambda b, i: (b, i, kblk)),
                  pl.BlockSpec((1, t, kw), lambda b, i: (b, prev(i), kblk)),
                  pl.BlockSpec((1, tq, kw), lambda b, i: (b, i, vblk)),
                  pl.BlockSpec((1, t, kw), lambda b, i: (b, prev(i), vblk))],
        out_specs=pl.BlockSpec((1, tq, qw), lambda b, i: (b, i, 0)),
        out_shape=jax.ShapeDtypeStruct((batch, seq, qw), BF16),
        compiler_params=_params("parallel", "arbitrary"),
        name="swa_attention",
    )(sinks, qkv, qkv, qkv, qkv, qkv)


def _mla_up_kernel(c_ref, gq_ref, gkv_ref, wq_ref, wk_ref, wv_ref, cq_ref, s1q_ref, s2q_ref,
                   ck_ref, s1k_ref, s2k_ref, q_ref, k_ref, v_ref):
    c = c_ref[...]
    cq = c[:, :MLA_Q_LORA]
    ckv = c[:, MLA_Q_LORA:MLA_Q_LORA + MLA_KV_LORA]
    kr = c[:, MLA_Q_LORA + MLA_KV_LORA:]
    cq = cq * lax.rsqrt(jnp.mean(cq * cq, axis=-1, keepdims=True) + RMS_EPS) * gq_ref[...]
    ckv = ckv * lax.rsqrt(jnp.mean(ckv * ckv, axis=-1, keepdims=True) + RMS_EPS) * gkv_ref[...]
    ckv_b = ckv.astype(BF16)
    q = _dot(cq.astype(BF16), wq_ref[...])
    kn = _dot(ckv_b, wk_ref[...])
    v_ref[...] = _dot(ckv_b, wv_ref[...]).astype(BF16)

    def rot(xh, cc, s1, s2):
        return xh * cc + pltpu.roll(xh, 16, 1) * s1 + pltpu.roll(xh, LANES - 16, 1) * s2

    kr = rot(kr, ck_ref[...], s1k_ref[...], s2k_ref[...])
    cq_t, s1q, s2q = cq_ref[...], s1q_ref[...], s2q_ref[...]
    for h in range(MLA_HEADS):
        sl = slice(h * LANES, (h + 1) * LANES)
        q_ref[:, sl] = rot(q[:, sl], cq_t, s1q, s2q).astype(BF16)
        k_ref[:, sl] = (kn[:, sl] + kr).astype(BF16)


def _mla_rope_tables(seq, scale):
    half = MLA_ROPE // 2
    inv = ROPE_THETA ** (-jnp.arange(0, MLA_ROPE, 2, dtype=F32) / MLA_ROPE)
    ang = jnp.arange(seq, dtype=F32)[:, None] * inv[None, :]
    cos, sin = jnp.cos(ang), jnp.sin(ang)
    z = lambda w: jnp.zeros((seq, w), F32)
    one = jnp.ones((seq, MLA_NOPE), F32)
    cc = jnp.concatenate([one, cos, cos, z(LANES - MLA_NOPE - MLA_ROPE)], axis=1) * scale
    s1 = jnp.concatenate([z(MLA_NOPE + half), sin, z(LANES - MLA_NOPE - MLA_ROPE)], axis=1) * scale
    s2 = jnp.concatenate([z(MLA_NOPE), -sin, z(LANES - MLA_NOPE - half)], axis=1) * scale
    return cc, s1, s2


def _mla_up(c, gq, gkv, wq, wk, wv, seq, tm=512):
    n, cw = c.shape
    tm = min(tm, seq)
    spt = seq // tm
    scale = LOG2E * (MLA_NOPE + MLA_ROPE) ** -0.5
    tq = _mla_rope_tables(seq, scale)
    tk = _mla_rope_tables(seq, 1.0)
    qk_w = MLA_HEADS * LANES
    v_w = MLA_HEADS * MLA_V
    full = lambda a: _resident(a.shape)
    tab = pl.BlockSpec((tm, LANES), lambda i: (i % spt, 0))
    gq2, gkv2 = gq.reshape(1, -1), gkv.reshape(1, -1)
    return pl.pallas_call(
        _mla_up_kernel,
        grid=(n // tm,),
        in_specs=[pl.BlockSpec((tm, cw), lambda i: (i, 0)), full(gq2), full(gkv2), full(wq), full(wk),
                  full(wv), tab, tab, tab, tab, tab, tab],
        out_specs=[pl.BlockSpec((tm, qk_w), lambda i: (i, 0)),
                   pl.BlockSpec((tm, qk_w), lambda i: (i, 0)),
                   pl.BlockSpec((tm, v_w), lambda i: (i, 0))],
        out_shape=[jax.ShapeDtypeStruct((n, qk_w), BF16), jax.ShapeDtypeStruct((n, qk_w), BF16),
                   jax.ShapeDtypeStruct((n, v_w), BF16)],
        compiler_params=_params("parallel"),
        name="mla_up_proj",
    )(c, gq2, gkv2, wq, wk, wv, *tq, *tk)


def _flash_init(tq):
    one = (jnp.full((tq, 1), NEG, F32), jnp.zeros((tq, 1), F32))
    return (one, one)


def _flash_pair(ss, v, carry, acc_ref):
    hs = range(2)
    ms = [jnp.maximum(carry[hh][0], jnp.max(ss[hh], axis=-1, keepdims=True)) for hh in hs]
    ps = [jnp.exp2(ss[hh] - ms[hh]) for hh in hs]
    al = [jnp.exp2(carry[hh][0] - ms[hh]) for hh in hs]
    ls = [al[hh] * carry[hh][1] + jnp.sum(ps[hh], axis=-1, keepdims=True) for hh in hs]
    for hh in hs:
        acc_ref[hh] = al[hh] * acc_ref[hh] + _dot(ps[hh].astype(BF16), v)
    return tuple((ms[hh], ls[hh]) for hh in hs)


SOFTMAX_ROWS = 32


def _flash_scratch(tq, tk):
    return [pltpu.VMEM((2, tq, tk), F32), pltpu.VMEM((2, tq, tk), BF16), pltpu.VMEM((2, 2, tq, LANES), F32),
            pltpu.VMEM((2, tq, LANES), F32)]


def _flash_reset(st_ref, acc_ref):
    st_ref[:, 0] = jnp.full(st_ref.shape[:1] + st_ref.shape[2:], NEG, F32)
    st_ref[:, 1] = jnp.zeros(st_ref.shape[:1] + st_ref.shape[2:], F32)
    acc_ref[...] = jnp.zeros(acc_ref.shape, F32)


def _flash_update(s_ref, p_ref, st_ref, acc_ref, v, extra=None):
    _, tq, tk = s_ref.shape
    for c in range(tq // SOFTMAX_ROWS):
        rows = slice(c * SOFTMAX_ROWS, (c + 1) * SOFTMAX_ROWS)
        for hh in range(2):
            s = s_ref[hh, rows, :]
            if extra is not None:
                s = s + extra(hh, rows)
            m_old = st_ref[hh, 0, rows, :]
            m_new = jnp.maximum(m_old, jnp.max(s, axis=-1, keepdims=True))
            al = jnp.exp2(m_old - m_new)
            ps = [jnp.exp2(s[:, c0:c0 + LANES] - m_new) for c0 in range(0, tk, LANES)]
            tot = ps[0]
            for p in ps[1:]:
                tot = tot + p
            st_ref[hh, 0, rows, :] = m_new
            st_ref[hh, 1, rows, :] = al * st_ref[hh, 1, rows, :] + jnp.sum(tot, axis=-1, keepdims=True)
            acc_ref[hh, rows, :] = al * acc_ref[hh, rows, :]
            for n, p in enumerate(ps):
                p_ref[hh, rows, n * LANES:(n + 1) * LANES] = p.astype(BF16)
    for hh in range(2):
        acc_ref[hh] += _dot(p_ref[hh], v)


def _tile_rel(tq, tk):
    return (lax.broadcasted_iota(jnp.int32, (tq, tk), 0) - lax.broadcasted_iota(jnp.int32, (tq, tk), 1))


def _mla_attn_kernel(q_ref, k_ref, v_ref, o_ref, s_ref, p_ref, st_ref, acc_ref, *, tq, tk):
    i = pl.program_id(2)
    r = tq // tk
    rel = _tile_rel(tq, tk)
    sls = [slice(hh * LANES, (hh + 1) * LANES) for hh in range(2)]

    def tile(j, diag):
        start = pl.multiple_of(j * tk, tk)
        for hh in range(2):
            s = _dot_t(q_ref[0, :, sls[hh]], k_ref[0, pl.ds(start, tk), sls[hh]])
            if diag is not None:
                s = jnp.where(rel >= diag * tk, s, NEG)
            s_ref[hh] = s
        _flash_update(s_ref, p_ref, st_ref, acc_ref, v_ref[0, pl.ds(start, tk), :])

    _flash_reset(st_ref, acc_ref)
    for d in range(r):
        tile(i * r + d, d)

    @pl.loop(0, i * r)
    def _(j):
        tile(j, None)

    outs = [acc_ref[hh] / st_ref[hh, 1] for hh in range(2)]
    o_ref[0] = jnp.where(_lane_lo(), outs[0], outs[1]).astype(BF16)


def _mla_attention(q, k, v, tq=512, tk=512):
    batch, seq, _ = q.shape
    tq, tk = min(tq, seq), min(tk, seq)
    return pl.pallas_call(
        functools.partial(_mla_attn_kernel, tq=tq, tk=tk),
        grid=(batch, MLA_HEADS // 2, seq // tq),
        in_specs=[pl.BlockSpec((1, tq, 2 * LANES), lambda b, m, i: (b, i, m)),
                  pl.BlockSpec((1, seq, 2 * LANES), lambda b, m, i: (b, 0, m)),
                  pl.BlockSpec((1, seq, LANES), lambda b, m, i: (b, 0, m))],
        out_specs=pl.BlockSpec((1, tq, LANES), lambda b, m, i: (b, i, m)),
        out_shape=jax.ShapeDtypeStruct(v.shape, BF16),
        scratch_shapes=_flash_scratch(tq, tk),
        compiler_params=_params("parallel", "parallel", "arbitrary"),
        name="mla_attention",
    )(q, k, v)


def _order_key(x):
    bits = lax.bitcast_convert_type(x, jnp.int32)
    return bits ^ (lax.shift_right_arithmetic(bits, 31) & 0x7FFFFFFF)


def _dsa_kernel(slope_ref, qi_ref, ki_ref, wi_ref, q_ref, k_ref, v_ref, o_ref, key_ref, bias_ref,
                dist_ref, byte_ref, acc_ref, *, t, topk):
    i = pl.program_id(1)
    m = pl.program_id(2)
    lo = _lane_lo()
    row = lax.broadcasted_iota(jnp.int32, (t, t), 0)
    col = lax.broadcasted_iota(jnp.int32, (t, t), 1)
    rel = row - col

    @pl.when(m == 0)
    def _select():
        wi_t = jnp.transpose(wi_ref[0] * (IDX_HEADS ** -0.5))
        rel_t = col - row
        pack = 16

        def score_tile(j, carry):
            start = pl.multiple_of(j * t, t)
            ki = ki_ref[0, pl.ds(start, t), :]
            sc = jnp.zeros((t, t), F32)
            for p in range(IDX_HEADS // 2):
                qp = qi_ref[0, :, p * LANES:(p + 1) * LANES]
                for hh in range(2):
                    h = 2 * p + hh
                    qm = jnp.where(lo if hh == 0 else jnp.logical_not(lo), qp, jnp.zeros_like(qp))
                    dots = _dot_t(ki, qm)
                    sc = sc + wi_t[h:h + 1, :] * jnp.maximum(dots, 0.0)
            sc = sc + 0.0
            sc = jnp.where(rel_t >= (j - i) * t, sc, NEG)
            key_ref[j] = _order_key(sc)
            dist_ref[j] = ((i - j) * t + rel).astype(F32)
            return carry

        lax.fori_loop(0, i + 1, score_tile, 0)

        one_b, zero_b = jnp.ones((), BF16), jnp.zeros((), BF16)

        def count_ge(cand):
            cand_b = cand.astype(BF16)

            def body(j, part):
                ge = jnp.where(byte_ref[j] >= cand_b, one_b, zero_b)
                slabs = [ge[r0:r0 + pack] for r0 in range(0, t, pack)]
                while len(slabs) > 1:
                    slabs = [a + b for a, b in zip(slabs[0::2], slabs[1::2])]
                return part + slabs[0]
            part = lax.fori_loop(0, i + 1, body, jnp.zeros((pack, t), BF16))
            return jnp.sum(part.astype(F32), axis=0, keepdims=True)

        prefix = jnp.zeros((1, t), jnp.int32)
        above = jnp.zeros((1, t), F32)
        for stage in range(4):
            shift = 24 - 8 * stage

            def fill(j, carry, stage=stage, shift=shift, prefix=prefix):
                key = key_ref[j]
                if stage == 0:
                    byte = lax.shift_right_arithmetic(key, shift) + 128
                else:
                    match = lax.shift_right_arithmetic(key, shift + 8) == prefix
                    byte = jnp.where(match, lax.shift_right_logical(key, shift) & 255, -1)
                byte_ref[j] = byte.astype(F32).astype(BF16)
                return carry

            lax.fori_loop(0, i + 1, fill, 0)
            val = jnp.zeros((1, t), F32)
            rej = jnp.zeros((1, t), F32)
            for bit in (128, 64, 32, 16, 8, 4, 2, 1):
                cand = val + float(bit)
                cnt = count_ge(cand)
                ok = above + cnt >= topk
                val = jnp.where(ok, cand, val)
                rej = jnp.where(ok, rej, cnt)
            above = above + rej
            byte_i = val.astype(jnp.int32) - (128 if stage == 0 else 0)
            prefix = byte_i if stage == 0 else lax.shift_left(prefix, 8) | byte_i
        thr = prefix
        need = topk - above
        before = jnp.where(row > col, 1.0, 0.0).astype(BF16)

        def select_tile(j, seen):
            key = key_ref[j]
            eq = key == thr
            eq_b = jnp.where(eq, 1.0, 0.0).astype(BF16)
            rank = seen + _dot(before, eq_b)
            sel = jnp.logical_or(key > thr, jnp.logical_and(eq, rank < need))
            sel = jnp.logical_and(sel, rel_t >= (j - i) * t)
            bias_ref[j] = jnp.transpose(jnp.where(sel, 0.0, NEG))
            return seen + jnp.sum(eq_b.astype(F32), axis=0, keepdims=True)

        lax.fori_loop(0, i + 1, select_tile, jnp.zeros((1, t), F32))

    qp = q_ref[0]
    qs = [jnp.where(lo if hh == 0 else jnp.logical_not(lo), qp, jnp.zeros_like(qp)) for hh in range(2)]
    slopes = [slope_ref[2 * m + hh] for hh in range(2)]

    def tile(j, carry):
        start = pl.multiple_of(j * t, t)
        k = k_ref[0, pl.ds(start, t), :]
        v = v_ref[0, pl.ds(start, t), :]
        ss = [_dot_t(qs[hh], k) for hh in range(2)]
        dist, bias = dist_ref[j], bias_ref[j]
        ss = [ss[hh] - slopes[hh] * dist + bias for hh in range(2)]
        return _flash_pair(ss, v, carry, acc_ref)

    acc_ref[...] = jnp.zeros(acc_ref.shape, F32)
    carry = tile(i, _flash_init(t))
    carry = lax.fori_loop(0, i, tile, carry)
    outs = [acc_ref[hh] / carry[hh][1] for hh in range(2)]
    o_ref[0] = jnp.where(lo, outs[0], outs[1]).astype(BF16)


def _dsa_attention(proj, wi, batch, seq, t=512):
    t = min(t, seq)
    nt = seq // t
    topk = min(IDX_TOPK, seq // 4)
    qw = DSA_HEADS * DSA_HEAD_DIM
    q0 = 0
    k0 = qw // LANES
    v0 = k0 + DSA_KV_HEADS
    qi0 = (v0 + DSA_KV_HEADS) * LANES // (IDX_HEADS * IDX_DIM)
    ki0 = v0 + DSA_KV_HEADS + IDX_HEADS * IDX_DIM // LANES
    pairs = DSA_HEADS // 2
    per_kv = pairs // DSA_KV_HEADS
    return pl.pallas_call(
        functools.partial(_dsa_kernel, t=t, topk=topk),
        grid=(batch, nt, pairs),
        in_specs=[pl.BlockSpec(memory_space=pltpu.SMEM),
                  pl.BlockSpec((1, t, IDX_HEADS * IDX_DIM), lambda b, i, m: (b, i, qi0)),
                  pl.BlockSpec((1, seq, LANES), lambda b, i, m: (b, 0, ki0)),
                  pl.BlockSpec((1, t, LANES), lambda b, i, m: (b, i, 0)),
                  pl.BlockSpec((1, t, LANES), lambda b, i, m: (b, i, q0 + m)),
                  pl.BlockSpec((1, seq, LANES), lambda b, i, m: (b, 0, k0 + m // per_kv)),
                  pl.BlockSpec((1, seq, LANES), lambda b, i, m: (b, 0, v0 + m // per_kv))],
        out_specs=pl.BlockSpec((1, t, LANES), lambda b, i, m: (b, i, m)),
        out_shape=jax.ShapeDtypeStruct((batch, seq, qw), BF16),
        scratch_shapes=[pltpu.VMEM((nt, t, t), jnp.int32), pltpu.VMEM((nt, t, t), F32),
                        pltpu.VMEM((nt, t, t), F32), pltpu.VMEM((nt, t, t), BF16),
                        pltpu.VMEM((2, t, LANES), F32)],
        compiler_params=_params("parallel", "arbitrary", "arbitrary"),
        name="dsa_attention",
    )(jnp.asarray([s * LOG2E for s in _alibi_slopes(DSA_HEADS)], F32), proj, proj, wi, proj, proj, proj)


def _sb_kernel(q_ref, k_ref, v_ref, o_ref, later_ref, acc_ref, *, tq, tk):
    i = pl.program_id(2)
    r = tq // tk
    lo = _lane_lo()
    rel = _tile_rel(tq, tk)
    suffix = jnp.where(_tile_rel(tk, tk) >= 0, 1.0, 0.0).astype(BF16)
    qp = q_ref[0]
    qs = [jnp.where(lo if hh == 0 else jnp.logical_not(lo), qp, jnp.zeros_like(qp)) for hh in range(2)]

    def tile(j, diag):
        start = pl.multiple_of(j * tk, tk)
        k = k_ref[0, pl.ds(start, tk), :]
        v = v_ref[0, pl.ds(start, tk), :]
        hs = range(2)
        zs = [_dot_t(qs[hh], k) for hh in hs]
        lgs = []
        for z in zs:
            nz = -z
            lgs.append(jnp.minimum(nz, 0.0) - jnp.log2(1.0 + jnp.exp2(jnp.minimum(z, nz))))
        if diag is not None:
            strict = rel > diag * tk
            lgs = [jnp.where(strict, lg, 0.0) for lg in lgs]
        lbs = [lg.astype(BF16) for lg in lgs]
        runs = [_dot(lb, suffix) for lb in lbs]
        laters = [later_ref[hh] for hh in hs]
        a_s = [jnp.exp2(zs[hh] + runs[hh] + laters[hh]) for hh in hs]
        if diag is not None:
            a_s = [jnp.where(strict, a, 0.0) for a in a_s]
        for hh in hs:
            later_ref[hh] = laters[hh] + runs[hh][:, 0:1]
        for hh in hs:
            acc_ref[hh] += _dot(a_s[hh].astype(BF16), v)

    later_ref[...] = jnp.zeros(later_ref.shape, F32)
    acc_ref[...] = jnp.zeros(acc_ref.shape, F32)
    for d in reversed(range(r)):
        tile(i * r + d, d)

    @pl.loop(0, i * r)
    def _(jj):
        tile(i * r - 1 - jj, None)

    o_ref[0] = jnp.where(lo, acc_ref[0], acc_ref[1]).astype(BF16)


def _sb_attention(qkv, batch, seq, tq=512, tk=256):
    tq, tk = min(tq, seq), min(tk, seq)
    w = SB_HEADS * SB_HEAD_DIM
    pairs = SB_HEADS // 2
    return pl.pallas_call(
        functools.partial(_sb_kernel, tq=tq, tk=tk),
        grid=(batch, pairs, seq // tq),
        in_specs=[pl.BlockSpec((1, tq, LANES), lambda b, m, i: (b, i, m)),
                  pl.BlockSpec((1, seq, LANES), lambda b, m, i: (b, 0, pairs + m)),
                  pl.BlockSpec((1, seq, LANES), lambda b, m, i: (b, 0, 2 * pairs + m))],
        out_specs=pl.BlockSpec((1, tq, LANES), lambda b, m, i: (b, i, m)),
        out_shape=jax.ShapeDtypeStruct((batch, seq, w), BF16),
        scratch_shapes=[pltpu.VMEM((2, tq, 1), F32), pltpu.VMEM((2, tq, LANES), F32)],
        compiler_params=_params("parallel", "parallel", "arbitrary"),
        name="sb_attention",
    )(qkv, qkv, qkv)


def _dup_heads(w, n_heads, dim):
    d = w.shape[0]
    w = w.reshape(d, n_heads, 1, dim)
    return jnp.broadcast_to(w, (d, n_heads, 2, dim)).reshape(d, n_heads * 2 * dim)


def _pad_heads(w, n_heads, dim, to):
    d = w.shape[0]
    w = w.reshape(d, n_heads, dim)
    return jnp.pad(w, ((0, 0), (0, 0), (0, to - dim))).reshape(d, n_heads * to)


def _swa_w_in(w):
    qw = SWA_HEADS * SWA_HEAD_DIM
    kw = SWA_KV_HEADS * SWA_HEAD_DIM
    q, k, v = w[:, :qw], w[:, qw:qw + kw], w[:, qw + kw:]
    q = q * (LOG2E * SWA_HEAD_DIM ** -0.5)
    return jnp.concatenate([q, _dup_heads(k, SWA_KV_HEADS, SWA_HEAD_DIM),
                            _dup_heads(v, SWA_KV_HEADS, SWA_HEAD_DIM)], axis=1).astype(BF16)


def _mla_w_in(w):
    d = w.shape[0]
    lat = MLA_Q_LORA + MLA_KV_LORA
    z = lambda c: jnp.zeros((d, c), w.dtype)
    return jnp.concatenate([w[:, :lat], z(MLA_NOPE), w[:, lat:], z(LANES - MLA_NOPE - MLA_ROPE)],
                           axis=1).astype(BF16)


def _sb_w_in(w):
    qw = SB_HEADS * SB_HEAD_DIM
    return jnp.concatenate([w[:, :qw] * (LOG2E * SB_HEAD_DIM ** -0.5), w[:, qw:]], axis=1).astype(BF16)


def _dsa_w_in(w):
    qw = DSA_HEADS * DSA_HEAD_DIM
    kw = DSA_KV_HEADS * DSA_HEAD_DIM
    iw = IDX_HEADS * IDX_DIM
    c = np.cumsum([qw, kw, kw, iw, IDX_DIM]).tolist()
    q, k, v, qi, ki, wi = (w[:, :c[0]], w[:, c[0]:c[1]], w[:, c[1]:c[2]], w[:, c[2]:c[3]],
                           w[:, c[3]:c[4]], w[:, c[4]:])
    q = q * (LOG2E * DSA_HEAD_DIM ** -0.5)
    qi = qi * (IDX_DIM ** -0.5)
    main = jnp.concatenate([q, _dup_heads(k, DSA_KV_HEADS, DSA_HEAD_DIM),
                            _dup_heads(v, DSA_KV_HEADS, DSA_HEAD_DIM), qi, _dup_heads(ki, 1, IDX_DIM)],
                           axis=1).astype(BF16)
    wi = jnp.pad(wi, ((0, 0), (0, LANES - IDX_HEADS))).astype(BF16)
    return main, wi


def kernel(x, ffn1_w_gu, ffn1_w_down, ffn2_w_gu, ffn2_w_down, ln_g, ln_b, a_w_in, a_sinks, a_w_o,
           b_w_in, b_q_norm, b_w_uq, b_kv_norm, b_w_ukv, b_w_o, c_w_in, c_w_o, d_w_in, d_w_o):
    batch, seq, d = x.shape
    n = batch * seq
    x = x.reshape(n, d)
    for i in range(DEPTH):
        mixer, j = i % N_MIXERS, i // N_MIXERS

        def ffn1(projs):
            return _ffn_proj(x, ffn1_w_gu[i].astype(BF16), ffn1_w_down[i].astype(BF16), ln_g[i, 0],
                             ln_b[i, 0], projs)

        if mixer == 0:
            x, qkv = ffn1([(_swa_w_in(a_w_in[j]), BF16)])
            o = _swa_attention(qkv.reshape(batch, seq, -1), a_sinks[j] * LOG2E, batch, seq)
            w_o = a_w_o[j]
        elif mixer == 1:
            x, c = ffn1([(_mla_w_in(b_w_in[j]), F32)])
            hd = MLA_NOPE + MLA_ROPE
            wq = _pad_heads(b_w_uq[j], MLA_HEADS, hd, LANES).astype(BF16)
            wkv = b_w_ukv[j].reshape(MLA_KV_LORA, MLA_HEADS, MLA_NOPE + MLA_V)
            wk = _pad_heads(wkv[:, :, :MLA_NOPE].reshape(MLA_KV_LORA, -1), MLA_HEADS, MLA_NOPE,
                            LANES).astype(BF16)
            wv = wkv[:, :, MLA_NOPE:].reshape(MLA_KV_LORA, -1).astype(BF16)
            q, k, v = _mla_up(c, b_q_norm[j], b_kv_norm[j], wq, wk, wv, seq)
            o = _mla_attention(q.reshape(batch, seq, -1), k.reshape(batch, seq, -1),
                               v.reshape(batch, seq, -1))
            w_o = b_w_o[j]
        elif mixer == 2:
            w_main, w_wi = _dsa_w_in(c_w_in[j])
            x, proj, wi = ffn1([(w_main, BF16), (w_wi, F32)])
            o = _dsa_attention(proj.reshape(batch, seq, -1), wi.reshape(batch, seq, -1), batch, seq)
            w_o = c_w_o[j]
        else:
            x, qkv = ffn1([(_sb_w_in(d_w_in[j]), BF16)])
            o = _sb_attention(qkv.reshape(batch, seq, -1), batch, seq)
            w_o = d_w_o[j]
        x = _proj_ffn(o.reshape(n, -1), w_o.astype(BF16), x, ln_g[i, 1], ln_b[i, 1],
                      ffn2_w_gu[i].astype(BF16), ffn2_w_down[i].astype(BF16), ln_g[i, 2], ln_b[i, 2])
    return x.reshape(batch, seq, d)
```

```python
import functools
import math

import numpy as np
import jax
import jax.numpy as jnp
from jax import lax
from jax.experimental import pallas as pl
from jax.experimental.pallas import tpu as pltpu

D_MODEL = 1024
DEPTH = 4
N_MIXERS = 4
D_FF = 2816
LN_EPS = 1e-5
RMS_EPS = 1e-6
NEG = -1e30

SWA_HEADS, SWA_KV_HEADS, SWA_HEAD_DIM, SWA_BLOCK = 16, 4, 64, 128
MLA_HEADS, MLA_Q_LORA, MLA_KV_LORA, MLA_NOPE, MLA_ROPE, MLA_V = 16, 768, 256, 64, 32, 64
ROPE_THETA = 10000.0
DSA_HEADS, DSA_KV_HEADS, DSA_HEAD_DIM = 16, 4, 64
IDX_HEADS, IDX_DIM, IDX_TOPK = 8, 64, 256
SB_HEADS, SB_HEAD_DIM = 16, 64

ALPHA = (2.0 * DEPTH) ** 0.25
LOG2E = math.log2(math.e)

LANES = 128
HALF = 64
VMEM_LIMIT = 56 * 1024 * 1024
BF16 = jnp.bfloat16
F32 = jnp.float32


def _params(*sem, flags=None):
    return pltpu.CompilerParams(dimension_semantics=sem, vmem_limit_bytes=VMEM_LIMIT, flags=flags)


def _dot(a, b):
    return jnp.dot(a, b, preferred_element_type=F32)


def _dot_t(a, b):
    return lax.dot_general(a, b, (((1,), (1,)), ((), ())), preferred_element_type=F32)


def _alibi_slopes(n_heads):
    return [2.0 ** (-8.0 * (i + 1) / n_heads) for i in range(n_heads)]


def _layer_norm(y, g, b):
    mu = jnp.mean(y, axis=-1, keepdims=True)
    yc = y - mu
    var = jnp.mean(yc * yc, axis=-1, keepdims=True)
    return yc * lax.rsqrt(var + LN_EPS) * g + b


def _lane_lo():
    return lax.broadcasted_iota(jnp.int32, (1, LANES), 1) < HALF


def _resident(shape):
    return pl.BlockSpec(shape, lambda i: (0,) * len(shape), pipeline_mode=pl.Buffered(1))


def _rows(tm, width):
    return pl.BlockSpec((tm, width), lambda i: (i, 0))


def _ffn_ln_value(x, wgu_ref, wd_ref, g_ref, b_ref):
    f = wd_ref.shape[0]
    xb = x.astype(BF16)
    h = _dot(xb, wgu_ref[:, :f])
    u = _dot(xb, wgu_ref[:, f:])
    a = h * (1.0 / (1.0 + jnp.exp(-h))) * u
    y = ALPHA * x + 0.5 * _dot(a.astype(BF16), wd_ref[...])
    return _layer_norm(y, g_ref[...], b_ref[...])


def _ffn_proj_kernel(x_ref, wgu_ref, wd_ref, g_ref, b_ref, *refs, n_proj):
    w_refs, o_ref, p_refs = refs[:n_proj], refs[n_proj], refs[n_proj + 1:]
    y = _ffn_ln_value(x_ref[...], wgu_ref, wd_ref, g_ref, b_ref)
    o_ref[...] = y
    yb = y.astype(BF16)
    for w_ref, p_ref in zip(w_refs, p_refs):
        p_ref[...] = _dot(yb, w_ref[...]).astype(p_ref.dtype)


def _ffn_proj(x, w_gu, w_down, g, b, projs, tm=512):
    n, d = x.shape
    tm = min(tm, n)
    ws = [w for w, _ in projs]
    return pl.pallas_call(
        functools.partial(_ffn_proj_kernel, n_proj=len(projs)),
        grid=(n // tm,),
        in_specs=[_rows(tm, d), _resident(w_gu.shape), _resident(w_down.shape), _resident((1, d)),
                  _resident((1, d))] + [_resident(w.shape) for w in ws],
        out_specs=[_rows(tm, d)] + [_rows(tm, w.shape[1]) for w in ws],
        out_shape=[jax.ShapeDtypeStruct((n, d), F32)]
        + [jax.ShapeDtypeStruct((n, w.shape[1]), dt) for w, dt in projs],
        compiler_params=_params("parallel"),
        name="ffn_in_proj",
    )(x, w_gu, w_down, g.reshape(1, d), b.reshape(1, d), *ws)


def _proj_ffn_kernel(o_ref, wo_ref, x_ref, g1_ref, b1_ref, wgu_ref, wd_ref, g2_ref, b2_ref, out_ref):
    y = ALPHA * x_ref[...] + _dot(o_ref[...], wo_ref[...])
    x2 = _layer_norm(y, g1_ref[...], b1_ref[...])
    out_ref[...] = _ffn_ln_value(x2, wgu_ref, wd_ref, g2_ref, b2_ref)


def _proj_ffn(o, w_o, x, g1, b1, w_gu, w_down, g2, b2, tm=512):
    n, k = o.shape
    d = w_o.shape[1]
    tm = min(tm, n)
    vec = lambda a: a.reshape(1, d)
    return pl.pallas_call(
        _proj_ffn_kernel,
        grid=(n // tm,),
        in_specs=[_rows(tm, k), _resident(w_o.shape), _rows(tm, d), _resident((1, d)), _resident((1, d)),
                  _resident(w_gu.shape), _resident(w_down.shape), _resident((1, d)), _resident((1, d))],
        out_specs=_rows(tm, d),
        out_shape=jax.ShapeDtypeStruct((n, d), F32),
        compiler_params=_params("parallel"),
        name="out_proj_ffn",
    )(o, w_o, x, vec(g1), vec(b1), w_gu, w_down, vec(g2), vec(b2))


def _swa_kernel(sink_ref, q_ref, kc_ref, kp_ref, vc_ref, vp_ref, o_ref, *, tq):
    i = pl.program_id(1)
    t = SWA_BLOCK
    rep = SWA_HEADS // SWA_KV_HEADS
    kvs = range(SWA_KV_HEADS)
    lo = _lane_lo()
    rel = _tile_rel(t, 2 * t) + t
    col = lax.broadcasted_iota(jnp.int32, (t, 2 * t), 1)
    dist = rel.astype(F32)
    in_band = jnp.logical_and(rel >= 0, rel < SWA_BLOCK)
    slopes = [s * LOG2E for s in _alibi_slopes(SWA_HEADS)]
    lane = lambda g: slice(g * LANES, (g + 1) * LANES)
    for qb in range(tq // t):
        r0 = qb * t
        if qb == 0:
            ks = [jnp.concatenate([kp_ref[0, :, lane(g)], kc_ref[0, 0:t, lane(g)]], axis=0) for g in kvs]
            vs = [jnp.concatenate([vp_ref[0, :, lane(g)], vc_ref[0, 0:t, lane(g)]], axis=0) for g in kvs]
            valid = jnp.logical_and(in_band, col >= jnp.where(i == 0, t, 0))
        else:
            ks = [kc_ref[0, r0 - t:r0 + t, lane(g)] for g in kvs]
            vs = [vc_ref[0, r0 - t:r0 + t, lane(g)] for g in kvs]
            valid = in_band
        qst = []
        for g in kvs:
            parts = []
            for r in range(rep):
                h = g * rep + r
                qp = q_ref[0, r0:r0 + t, lane(h // 2)]
                parts.append(jnp.where(lo if h % 2 == 0 else jnp.logical_not(lo), qp, jnp.zeros_like(qp)))
            qst.append(jnp.concatenate(parts, axis=0))
        ss = [_dot_t(qst[g], ks[g]) for g in kvs]
        ps, dens = [], []
        for g in kvs:
            prow, drow = [], []
            for r in range(rep):
                h = g * rep + r
                s = jnp.where(valid, ss[g][r * t:(r + 1) * t] - slopes[h] * dist, NEG)
                sink = sink_ref[h]
                mx = jnp.maximum(jnp.max(s, axis=-1, keepdims=True), sink)
                p = jnp.exp2(s - mx)
                drow.append(jnp.sum(p, axis=-1, keepdims=True) + jnp.exp2(sink - mx))
                prow.append(p.astype(BF16))
            ps.append(jnp.concatenate(prow, axis=0))
            dens.append(drow)
        outs = [_dot(ps[g], vs[g]) for g in kvs]
        for g in kvs:
            for r in range(0, rep, 2):
                even = outs[g][r * t:(r + 1) * t] / dens[g][r]
                odd = outs[g][(r + 1) * t:(r + 2) * t] / dens[g][r + 1]
                o_ref[0, r0:r0 + t, lane((g * rep + r) // 2)] = jnp.where(lo, even, odd).astype(BF16)


def _swa_attention(qkv, sinks, batch, seq, tq=512):
    t = SWA_BLOCK
    tq = min(tq, seq)
    sub = tq // t
    qw = SWA_HEADS * SWA_HEAD_DIM
    kw = SWA_KV_HEADS * LANES
    qblk, kblk, vblk = 0, qw // kw, qw // kw + 1
    prev = lambda i: jnp.maximum(i * sub - 1, 0)
    return pl.pallas_call(
        functools.partial(_swa_kernel, tq=tq),
        grid=(batch, seq // tq),
        in_specs=[pl.BlockSpec(memory_space=pltpu.SMEM),
                  pl.BlockSpec((1, tq, qw), lambda b, i: (b, i, qblk)),
                  pl.BlockSpec((1, tq, kw), lambda b, i: (b, i, kblk)),
                  pl.BlockSpec((1, t, kw), lambda b, i: (b, prev(i), kblk)),
                  pl.BlockSpec((1, tq, kw), lambda b, i: (b, i, vblk)),
                  pl.BlockSpec((1, t, kw), lambda b, i: (b, prev(i), vblk))],
        out_specs=pl.BlockSpec((1, tq, qw), lambda b, i: (b, i, 0)),
        out_shape=jax.ShapeDtypeStruct((batch, seq, qw), BF16),
        compiler_params=_params("parallel", "arbitrary"),
        name="swa_attention",
    )(sinks, qkv, qkv, qkv, qkv, qkv)


def _mla_up_kernel(c_ref, gq_ref, gkv_ref, wq_ref, wk_ref, wv_ref, cq_ref, s1q_ref, s2q_ref,
                   ck_ref, s1k_ref, s2k_ref, q_ref, k_ref, v_ref):
    c = c_ref[...]
    cq = c[:, :MLA_Q_LORA]
    ckv = c[:, MLA_Q_LORA:MLA_Q_LORA + MLA_KV_LORA]
    kr = c[:, MLA_Q_LORA + MLA_KV_LORA:]
    cq = cq * lax.rsqrt(jnp.mean(cq * cq, axis=-1, keepdims=True) + RMS_EPS) * gq_ref[...]
    ckv = ckv * lax.rsqrt(jnp.mean(ckv * ckv, axis=-1, keepdims=True) + RMS_EPS) * gkv_ref[...]
    ckv_b = ckv.astype(BF16)
    q = _dot(cq.astype(BF16), wq_ref[...])
    kn = _dot(ckv_b, wk_ref[...])
    v_ref[...] = _dot(ckv_b, wv_ref[...]).astype(BF16)

    def rot(xh, cc, s1, s2):
        return xh * cc + pltpu.roll(xh, 16, 1) * s1 + pltpu.roll(xh, LANES - 16, 1) * s2

    kr = rot(kr, ck_ref[...], s1k_ref[...], s2k_ref[...])
    cq_t, s1q, s2q = cq_ref[...], s1q_ref[...], s2q_ref[...]
    for h in range(MLA_HEADS):
        sl = slice(h * LANES, (h + 1) * LANES)
        q_ref[:, sl] = rot(q[:, sl], cq_t, s1q, s2q).astype(BF16)
        k_ref[:, sl] = (kn[:, sl] + kr).astype(BF16)


def _mla_rope_tables(seq, scale):
    half = MLA_ROPE // 2
    inv = ROPE_THETA ** (-jnp.arange(0, MLA_ROPE, 2, dtype=F32) / MLA_ROPE)
    ang = jnp.arange(seq, dtype=F32)[:, None] * inv[None, :]
    cos, sin = jnp.cos(ang), jnp.sin(ang)
    z = lambda w: jnp.zeros((seq, w), F32)
    one = jnp.ones((seq, MLA_NOPE), F32)
    cc = jnp.concatenate([one, cos, cos, z(LANES - MLA_NOPE - MLA_ROPE)], axis=1) * scale
    s1 = jnp.concatenate([z(MLA_NOPE + half), sin, z(LANES - MLA_NOPE - MLA_ROPE)], axis=1) * scale
    s2 = jnp.concatenate([z(MLA_NOPE), -sin, z(LANES - MLA_NOPE - half)], axis=1) * scale
    return cc, s1, s2


def _mla_up(c, gq, gkv, wq, wk, wv, seq, tm=512):
    n, cw = c.shape
    tm = min(tm, seq)
    spt = seq // tm
    scale = LOG2E * (MLA_NOPE + MLA_ROPE) ** -0.5
    tq = _mla_rope_tables(seq, scale)
    tk = _mla_rope_tables(seq, 1.0)
    qk_w = MLA_HEADS * LANES
    v_w = MLA_HEADS * MLA_V
    full = lambda a: _resident(a.shape)
    tab = pl.BlockSpec((tm, LANES), lambda i: (i % spt, 0))
    gq2, gkv2 = gq.reshape(1, -1), gkv.reshape(1, -1)
    return pl.pallas_call(
        _mla_up_kernel,
        grid=(n // tm,),
        in_specs=[pl.BlockSpec((tm, cw), lambda i: (i, 0)), full(gq2), full(gkv2), full(wq), full(wk),
                  full(wv), tab, tab, tab, tab, tab, tab],
        out_specs=[pl.BlockSpec((tm, qk_w), lambda i: (i, 0)),
                   pl.BlockSpec((tm, qk_w), lambda i: (i, 0)),
                   pl.BlockSpec((tm, v_w), lambda i: (i, 0))],
        out_shape=[jax.ShapeDtypeStruct((n, qk_w), BF16), jax.ShapeDtypeStruct((n, qk_w), BF16),
                   jax.ShapeDtypeStruct((n, v_w), BF16)],
        compiler_params=_params("parallel"),
        name="mla_up_proj",
    )(c, gq2, gkv2, wq, wk, wv, *tq, *tk)


SUM_LANE = (HALF, 0)


def _with_sum_column(v):
    lane = lax.broadcasted_iota(jnp.int32, (1, LANES), 1)
    unit = lambda at: jnp.where(lane == at, 1.0, 0.0).astype(v.dtype)
    low = lane < HALF
    return [jnp.where(low, v, unit(SUM_LANE[0])), jnp.where(low, unit(SUM_LANE[1]), v)]


def _flash_finish(acc_ref):
    outs = [acc_ref[hh] / acc_ref[hh][:, SUM_LANE[hh]:SUM_LANE[hh] + 1] for hh in range(2)]
    return jnp.where(_lane_lo(), outs[0], outs[1])


def _row_max(s):
    mx = s[:, :LANES]
    for c0 in range(LANES, s.shape[1], LANES):
        mx = jnp.maximum(mx, s[:, c0:c0 + LANES])
    return jnp.max(mx, axis=-1, keepdims=True)


def _flash_init(tq):
    m = jnp.full((tq, LANES), NEG, F32)
    return (m, m)


def _flash_pair(ss, v, m_old, acc_ref):
    hs = range(2)
    vs = _with_sum_column(v)
    cols = [slice(c0, c0 + LANES) for c0 in range(0, ss[0].shape[1], LANES)]
    ms = [jnp.maximum(m_old[hh], _row_max(ss[hh])) for hh in hs]
    ps = [jnp.concatenate([jnp.exp2(ss[hh][:, c] - ms[hh]) for c in cols], axis=1).astype(BF16) for hh in hs]
    al = [jnp.exp2(m_old[hh] - ms[hh]) for hh in hs]
    for hh in hs:
        acc_ref[hh] = al[hh] * acc_ref[hh] + _dot(ps[hh], vs[hh])
    return tuple(ms)


SOFTMAX_ROWS = 32


def _flash_scratch(tq, tk):
    return [pltpu.VMEM((2, tq, tk), F32), pltpu.VMEM((2, tq, tk), BF16), pltpu.VMEM((2, tq, LANES), F32),
            pltpu.VMEM((2, tq, LANES), F32)]


def _flash_reset(m_ref, acc_ref):
    m_ref[...] = jnp.full(m_ref.shape, NEG, F32)
    acc_ref[...] = jnp.zeros(acc_ref.shape, F32)


def _flash_update(s_ref, p_ref, m_ref, acc_ref, v):
    _, tq, tk = s_ref.shape
    vs = _with_sum_column(v)
    for c in range(tq // SOFTMAX_ROWS):
        rows = slice(c * SOFTMAX_ROWS, (c + 1) * SOFTMAX_ROWS)
        for hh in range(2):
            s = s_ref[hh, rows, :]
            m_old = m_ref[hh, rows, :]
            m_new = jnp.maximum(m_old, _row_max(s))
            m_ref[hh, rows, :] = m_new
            acc_ref[hh, rows, :] = jnp.exp2(m_old - m_new) * acc_ref[hh, rows, :]
            for c0 in range(0, tk, LANES):
                p_ref[hh, rows, c0:c0 + LANES] = jnp.exp2(s[:, c0:c0 + LANES] - m_new).astype(BF16)
    for hh in range(2):
        acc_ref[hh] += _dot(p_ref[hh], vs[hh])


def _tile_rel(tq, tk):
    return (lax.broadcasted_iota(jnp.int32, (tq, tk), 0) - lax.broadcasted_iota(jnp.int32, (tq, tk), 1))


def _mla_attn_kernel(q_ref, k_ref, v_ref, o_ref, s_ref, p_ref, m_ref, acc_ref, *, tq, tk):
    i = pl.program_id(2)
    r = tq // tk
    rel = _tile_rel(tq, tk)
    sls = [slice(hh * LANES, (hh + 1) * LANES) for hh in range(2)]

    def tile(j, diag):
        start = pl.multiple_of(j * tk, tk)
        for hh in range(2):
            s = _dot_t(q_ref[0, :, sls[hh]], k_ref[0, pl.ds(start, tk), sls[hh]])
            if diag is not None:
                s = jnp.where(rel >= diag * tk, s, NEG)
            s_ref[hh] = s
        _flash_update(s_ref, p_ref, m_ref, acc_ref, v_ref[0, pl.ds(start, tk), :])

    _flash_reset(m_ref, acc_ref)
    for d in range(r):
        tile(i * r + d, d)

    @pl.loop(0, i * r)
    def _(j):
        tile(j, None)

    o_ref[0] = _flash_finish(acc_ref).astype(BF16)


def _mla_attention(q, k, v, tq=512, tk=512):
    batch, seq, _ = q.shape
    tq, tk = min(tq, seq), min(tk, seq)
    return pl.pallas_call(
        functools.partial(_mla_attn_kernel, tq=tq, tk=tk),
        grid=(batch, MLA_HEADS // 2, seq // tq),
        in_specs=[pl.BlockSpec((1, tq, 2 * LANES), lambda b, m, i: (b, i, m)),
                  pl.BlockSpec((1, seq, 2 * LANES), lambda b, m, i: (b, 0, m)),
                  pl.BlockSpec((1, seq, LANES), lambda b, m, i: (b, 0, m))],
        out_specs=pl.BlockSpec((1, tq, LANES), lambda b, m, i: (b, i, m)),
        out_shape=jax.ShapeDtypeStruct(v.shape, BF16),
        scratch_shapes=_flash_scratch(tq, tk),
        compiler_params=_params("parallel", "parallel", "arbitrary"),
        name="mla_attention",
    )(q, k, v)


def _order_key(x):
    bits = lax.bitcast_convert_type(x, jnp.int32)
    return bits ^ (lax.shift_right_arithmetic(bits, 31) & 0x7FFFFFFF)


def _dsa_kernel(slope_ref, qi_ref, ki_ref, wi_ref, q_ref, k_ref, v_ref, o_ref, key_ref, bias_ref,
                dist_ref, byte_ref, acc_ref, *, t, topk):
    i = pl.program_id(1)
    m = pl.program_id(2)
    lo = _lane_lo()
    row = lax.broadcasted_iota(jnp.int32, (t, t), 0)
    col = lax.broadcasted_iota(jnp.int32, (t, t), 1)
    rel = row - col

    @pl.when(m == 0)
    def _select():
        wi_t = jnp.transpose(wi_ref[0] * (IDX_HEADS ** -0.5))
        rel_t = col - row
        pack = 16

        def score_tile(j, carry):
            start = pl.multiple_of(j * t, t)
            ki = ki_ref[0, pl.ds(start, t), :]
            sc = jnp.zeros((t, t), F32)
            for p in range(IDX_HEADS // 2):
                qp = qi_ref[0, :, p * LANES:(p + 1) * LANES]
                for hh in range(2):
                    h = 2 * p + hh
                    qm = jnp.where(lo if hh == 0 else jnp.logical_not(lo), qp, jnp.zeros_like(qp))
                    dots = _dot_t(ki, qm)
                    sc = sc + wi_t[h:h + 1, :] * jnp.maximum(dots, 0.0)
            sc = sc + 0.0
            sc = jnp.where(rel_t >= (j - i) * t, sc, NEG)
            key_ref[j] = _order_key(sc)
            dist_ref[j] = ((i - j) * t + rel).astype(F32)
            return carry

        lax.fori_loop(0, i + 1, score_tile, 0)

        one_b, zero_b = jnp.ones((), BF16), jnp.zeros((), BF16)

        def count_ge(cand):
            cand_b = cand.astype(BF16)

            def body(j, part):
                ge = jnp.where(byte_ref[j] >= cand_b, one_b, zero_b)
                slabs = [ge[r0:r0 + pack] for r0 in range(0, t, pack)]
                while len(slabs) > 1:
                    slabs = [a + b for a, b in zip(slabs[0::2], slabs[1::2])]
                return part + slabs[0]
            part = lax.fori_loop(0, i + 1, body, jnp.zeros((pack, t), BF16))
            return jnp.sum(part.astype(F32), axis=0, keepdims=True)

        prefix = jnp.zeros((1, t), jnp.int32)
        above = jnp.zeros((1, t), F32)
        for stage in range(4):
            shift = 24 - 8 * stage

            def fill(j, carry, stage=stage, shift=shift, prefix=prefix):
                key = key_ref[j]
                if stage == 0:
                    byte = lax.shift_right_arithmetic(key, shift) + 128
                else:
                    match = lax.shift_right_arithmetic(key, shift + 8) == prefix
                    byte = jnp.where(match, lax.shift_right_logical(key, shift) & 255, -1)
                byte_ref[j] = byte.astype(F32).astype(BF16)
                return carry

            lax.fori_loop(0, i + 1, fill, 0)
            val = jnp.zeros((1, t), F32)
            rej = jnp.zeros((1, t), F32)
            for bit in (128, 64, 32, 16, 8, 4, 2, 1):
                cand = val + float(bit)
                cnt = count_ge(cand)
                ok = above + cnt >= topk
                val = jnp.where(ok, cand, val)
                rej = jnp.where(ok, rej, cnt)
            above = above + rej
            byte_i = val.astype(jnp.int32) - (128 if stage == 0 else 0)
            prefix = byte_i if stage == 0 else lax.shift_left(prefix, 8) | byte_i
        thr = prefix
        need = topk - above
        before = jnp.where(row > col, 1.0, 0.0).astype(BF16)

        def select_tile(j, seen):
            key = key_ref[j]
            eq = key == thr
            eq_b = jnp.where(eq, 1.0, 0.0).astype(BF16)
            rank = seen + _dot(before, eq_b)
            sel = jnp.logical_or(key > thr, jnp.logical_and(eq, rank < need))
            sel = jnp.logical_and(sel, rel_t >= (j - i) * t)
            bias_ref[j] = jnp.transpose(jnp.where(sel, 0.0, NEG))
            return seen + jnp.sum(eq_b.astype(F32), axis=0, keepdims=True)

        lax.fori_loop(0, i + 1, select_tile, jnp.zeros((1, t), F32))

    qp = q_ref[0]
    qs = [jnp.where(lo if hh == 0 else jnp.logical_not(lo), qp, jnp.zeros_like(qp)) for hh in range(2)]
    slopes = [slope_ref[2 * m + hh] for hh in range(2)]

    def tile(j, carry):
        start = pl.multiple_of(j * t, t)
        k = k_ref[0, pl.ds(start, t), :]
        v = v_ref[0, pl.ds(start, t), :]
        ss = [_dot_t(qs[hh], k) for hh in range(2)]
        dist, bias = dist_ref[j], bias_ref[j]
        ss = [ss[hh] - slopes[hh] * dist + bias for hh in range(2)]
        return _flash_pair(ss, v, carry, acc_ref)

    acc_ref[...] = jnp.zeros(acc_ref.shape, F32)
    carry = tile(i, _flash_init(t))
    lax.fori_loop(0, i, tile, carry)
    o_ref[0] = _flash_finish(acc_ref).astype(BF16)


def _dsa_attention(proj, wi, batch, seq, t=512):
    t = min(t, seq)
    nt = seq // t
    topk = min(IDX_TOPK, seq // 4)
    qw = DSA_HEADS * DSA_HEAD_DIM
    q0 = 0
    k0 = qw // LANES
    v0 = k0 + DSA_KV_HEADS
    qi0 = (v0 + DSA_KV_HEADS) * LANES // (IDX_HEADS * IDX_DIM)
    ki0 = v0 + DSA_KV_HEADS + IDX_HEADS * IDX_DIM // LANES
    pairs = DSA_HEADS // 2
    per_kv = pairs // DSA_KV_HEADS
    return pl.pallas_call(
        functools.partial(_dsa_kernel, t=t, topk=topk),
        grid=(batch, nt, pairs),
        in_specs=[pl.BlockSpec(memory_space=pltpu.SMEM),
                  pl.BlockSpec((1, t, IDX_HEADS * IDX_DIM), lambda b, i, m: (b, i, qi0)),
                  pl.BlockSpec((1, seq, LANES), lambda b, i, m: (b, 0, ki0)),
                  pl.BlockSpec((1, t, LANES), lambda b, i, m: (b, i, 0)),
                  pl.BlockSpec((1, t, LANES), lambda b, i, m: (b, i, q0 + m)),
                  pl.BlockSpec((1, seq, LANES), lambda b, i, m: (b, 0, k0 + m // per_kv)),
                  pl.BlockSpec((1, seq, LANES), lambda b, i, m: (b, 0, v0 + m // per_kv))],
        out_specs=pl.BlockSpec((1, t, LANES), lambda b, i, m: (b, i, m)),
        out_shape=jax.ShapeDtypeStruct((batch, seq, qw), BF16),
        scratch_shapes=[pltpu.VMEM((nt, t, t), jnp.int32), pltpu.VMEM((nt, t, t), F32),
                        pltpu.VMEM((nt, t, t), F32), pltpu.VMEM((nt, t, t), BF16),
                        pltpu.VMEM((2, t, LANES), F32)],
        compiler_params=_params("parallel", "arbitrary", "arbitrary"),
        name="dsa_attention",
    )(jnp.asarray([s * LOG2E for s in _alibi_slopes(DSA_HEADS)], F32), proj, proj, wi, proj, proj, proj)


def _sb_kernel(q_ref, k_ref, v_ref, o_ref, later_ref, acc_ref, *, tq, tk):
    i = pl.program_id(2)
    r = tq // tk
    lo = _lane_lo()
    rel = _tile_rel(tq, tk)
    suffix = jnp.where(_tile_rel(tk, tk) >= 0, 1.0, 0.0).astype(BF16)
    qp = q_ref[0]
    qs = [jnp.where(lo if hh == 0 else jnp.logical_not(lo), qp, jnp.zeros_like(qp)) for hh in range(2)]

    def tile(j, diag):
        start = pl.multiple_of(j * tk, tk)
        k = k_ref[0, pl.ds(start, tk), :]
        v = v_ref[0, pl.ds(start, tk), :]
        hs = range(2)
        zs = [_dot_t(qs[hh], k) for hh in hs]
        lgs = []
        for z in zs:
            nz = -z
            lgs.append(jnp.minimum(nz, 0.0) - jnp.log2(1.0 + jnp.exp2(jnp.minimum(z, nz))))
        if diag is not None:
            strict = rel > diag * tk
            lgs = [jnp.where(strict, lg, 0.0) for lg in lgs]
        lbs = [lg.astype(BF16) for lg in lgs]
        runs = [_dot(lb, suffix) for lb in lbs]
        tots = [jnp.broadcast_to(run[:, 0:1], (tq, LANES)) for run in runs]
        laters = [later_ref[hh] for hh in hs]
        cols = [slice(c0, c0 + LANES) for c0 in range(0, tk, LANES)]
        a_s = [jnp.concatenate([jnp.exp2(zs[hh][:, c] + runs[hh][:, c] + laters[hh]) for c in cols], axis=1)
               for hh in hs]
        if diag is not None:
            a_s = [jnp.where(strict, a, 0.0) for a in a_s]
        for hh in hs:
            later_ref[hh] = laters[hh] + tots[hh]
        for hh in hs:
            acc_ref[hh] += _dot(a_s[hh].astype(BF16), v)

    later_ref[...] = jnp.zeros(later_ref.shape, F32)
    acc_ref[...] = jnp.zeros(acc_ref.shape, F32)
    for d in reversed(range(r)):
        tile(i * r + d, d)

    @pl.loop(0, i * r)
    def _(jj):
        tile(i * r - 1 - jj, None)

    o_ref[0] = jnp.where(lo, acc_ref[0], acc_ref[1]).astype(BF16)


def _sb_attention(qkv, batch, seq, tq=512, tk=256):
    tq, tk = min(tq, seq), min(tk, seq)
    w = SB_HEADS * SB_HEAD_DIM
    pairs = SB_HEADS // 2
    return pl.pallas_call(
        functools.partial(_sb_kernel, tq=tq, tk=tk),
        grid=(batch, pairs, seq // tq),
        in_specs=[pl.BlockSpec((1, tq, LANES), lambda b, m, i: (b, i, m)),
                  pl.BlockSpec((1, seq, LANES), lambda b, m, i: (b, 0, pairs + m)),
                  pl.BlockSpec((1, seq, LANES), lambda b, m, i: (b, 0, 2 * pairs + m))],
        out_specs=pl.BlockSpec((1, tq, LANES), lambda b, m, i: (b, i, m)),
        out_shape=jax.ShapeDtypeStruct((batch, seq, w), BF16),
        scratch_shapes=[pltpu.VMEM((2, tq, LANES), F32), pltpu.VMEM((2, tq, LANES), F32)],
        compiler_params=_params("parallel", "parallel", "arbitrary"),
        name="sb_attention",
    )(qkv, qkv, qkv)


def _dup_heads(w, n_heads, dim):
    d = w.shape[0]
    w = w.reshape(d, n_heads, 1, dim)
    return jnp.broadcast_to(w, (d, n_heads, 2, dim)).reshape(d, n_heads * 2 * dim)


def _pad_heads(w, n_heads, dim, to):
    d = w.shape[0]
    w = w.reshape(d, n_heads, dim)
    return jnp.pad(w, ((0, 0), (0, 0), (0, to - dim))).reshape(d, n_heads * to)


def _swa_w_in(w):
    qw = SWA_HEADS * SWA_HEAD_DIM
    kw = SWA_KV_HEADS * SWA_HEAD_DIM
    q, k, v = w[:, :qw], w[:, qw:qw + kw], w[:, qw + kw:]
    q = q * (LOG2E * SWA_HEAD_DIM ** -0.5)
    return jnp.concatenate([q, _dup_heads(k, SWA_KV_HEADS, SWA_HEAD_DIM),
                            _dup_heads(v, SWA_KV_HEADS, SWA_HEAD_DIM)], axis=1).astype(BF16)


def _mla_w_in(w):
    d = w.shape[0]
    lat = MLA_Q_LORA + MLA_KV_LORA
    z = lambda c: jnp.zeros((d, c), w.dtype)
    return jnp.concatenate([w[:, :lat], z(MLA_NOPE), w[:, lat:], z(LANES - MLA_NOPE - MLA_ROPE)],
                           axis=1).astype(BF16)


def _sb_w_in(w):
    qw = SB_HEADS * SB_HEAD_DIM
    return jnp.concatenate([w[:, :qw] * (LOG2E * SB_HEAD_DIM ** -0.5), w[:, qw:]], axis=1).astype(BF16)


def _dsa_w_in(w):
    qw = DSA_HEADS * DSA_HEAD_DIM
    kw = DSA_KV_HEADS * DSA_HEAD_DIM
    iw = IDX_HEADS * IDX_DIM
    c = np.cumsum([qw, kw, kw, iw, IDX_DIM]).tolist()
    q, k, v, qi, ki, wi = (w[:, :c[0]], w[:, c[0]:c[1]], w[:, c[1]:c[2]], w[:, c[2]:c[3]],
                           w[:, c[3]:c[4]], w[:, c[4]:])
    q = q * (LOG2E * DSA_HEAD_DIM ** -0.5)
    qi = qi * (IDX_DIM ** -0.5)
    main = jnp.concatenate([q, _dup_heads(k, DSA_KV_HEADS, DSA_HEAD_DIM),
                            _dup_heads(v, DSA_KV_HEADS, DSA_HEAD_DIM), qi, _dup_heads(ki, 1, IDX_DIM)],
                           axis=1).astype(BF16)
    wi = jnp.pad(wi, ((0, 0), (0, LANES - IDX_HEADS))).astype(BF16)
    return main, wi


def kernel(x, ffn1_w_gu, ffn1_w_down, ffn2_w_gu, ffn2_w_down, ln_g, ln_b, a_w_in, a_sinks, a_w_o,
           b_w_in, b_q_norm, b_w_uq, b_kv_norm, b_w_ukv, b_w_o, c_w_in, c_w_o, d_w_in, d_w_o):
    batch, seq, d = x.shape
    n = batch * seq
    x = x.reshape(n, d)
    for i in range(DEPTH):
        mixer, j = i % N_MIXERS, i // N_MIXERS

        def ffn1(projs):
            return _ffn_proj(x, ffn1_w_gu[i].astype(BF16), ffn1_w_down[i].astype(BF16), ln_g[i, 0],
                             ln_b[i, 0], projs)

        if mixer == 0:
            x, qkv = ffn1([(_swa_w_in(a_w_in[j]), BF16)])
            o = _swa_attention(qkv.reshape(batch, seq, -1), a_sinks[j] * LOG2E, batch, seq)
            w_o = a_w_o[j]
        elif mixer == 1:
            x, c = ffn1([(_mla_w_in(b_w_in[j]), F32)])
            hd = MLA_NOPE + MLA_ROPE
            wq = _pad_heads(b_w_uq[j], MLA_HEADS, hd, LANES).astype(BF16)
            wkv = b_w_ukv[j].reshape(MLA_KV_LORA, MLA_HEADS, MLA_NOPE + MLA_V)
            wk = _pad_heads(wkv[:, :, :MLA_NOPE].reshape(MLA_KV_LORA, -1), MLA_HEADS, MLA_NOPE,
                            LANES).astype(BF16)
            wv = wkv[:, :, MLA_NOPE:].reshape(MLA_KV_LORA, -1).astype(BF16)
            q, k, v = _mla_up(c, b_q_norm[j], b_kv_norm[j], wq, wk, wv, seq)
            o = _mla_attention(q.reshape(batch, seq, -1), k.reshape(batch, seq, -1),
                               v.reshape(batch, seq, -1))
            w_o = b_w_o[j]
        elif mixer == 2:
            w_main, w_wi = _dsa_w_in(c_w_in[j])
            x, proj, wi = ffn1([(w_main, BF16), (w_wi, F32)])
            o = _dsa_attention(proj.reshape(batch, seq, -1), wi.reshape(batch, seq, -1), batch, seq)
            w_o = c_w_o[j]
        else:
            x, qkv = ffn1([(_sb_w_in(d_w_in[j]), BF16)])
            o = _sb_attention(qkv.reshape(batch, seq, -1), batch, seq)
            w_o = d_w_o[j]
        x = _proj_ffn(o.reshape(n, -1), w_o.astype(BF16), x, ln_g[i, 1], ln_b[i, 1],
                      ffn2_w_gu[i].astype(BF16), ffn2_w_down[i].astype(BF16), ln_g[i, 2], ln_b[i, 2])
    return x.reshape(batch, seq, d)
```

```python
import functools
import math

import numpy as np
import jax
import jax.numpy as jnp
from jax import lax
from jax.experimental import pallas as pl
from jax.experimental.pallas import tpu as pltpu

D_MODEL = 1024
DEPTH = 4
N_MIXERS = 4
D_FF = 2816
LN_EPS = 1e-5
RMS_EPS = 1e-6
NEG = -1e30

SWA_HEADS, SWA_KV_HEADS, SWA_HEAD_DIM, SWA_BLOCK = 16, 4, 64, 128
MLA_HEADS, MLA_Q_LORA, MLA_KV_LORA, MLA_NOPE, MLA_ROPE, MLA_V = 16, 768, 256, 64, 32, 64
ROPE_THETA = 10000.0
DSA_HEADS, DSA_KV_HEADS, DSA_HEAD_DIM = 16, 4, 64
IDX_HEADS, IDX_DIM, IDX_TOPK = 8, 64, 256
SB_HEADS, SB_HEAD_DIM = 16, 64

ALPHA = (2.0 * DEPTH) ** 0.25
LOG2E = math.log2(math.e)

LANES = 128
HALF = 64
VMEM_LIMIT = 56 * 1024 * 1024
BF16 = jnp.bfloat16
F32 = jnp.float32


def _params(*sem, flags=None):
    return pltpu.CompilerParams(dimension_semantics=sem, vmem_limit_bytes=VMEM_LIMIT, flags=flags)


def _dot(a, b):
    return jnp.dot(a, b, preferred_element_type=F32)


def _dot_t(a, b):
    return lax.dot_general(a, b, (((1,), (1,)), ((), ())), preferred_element_type=F32)


def _alibi_slopes(n_heads):
    return [2.0 ** (-8.0 * (i + 1) / n_heads) for i in range(n_heads)]


def _layer_norm(y, g, b):
    mu = jnp.mean(y, axis=-1, keepdims=True)
    yc = y - mu
    var = jnp.mean(yc * yc, axis=-1, keepdims=True)
    return yc * lax.rsqrt(var + LN_EPS) * g + b


def _lane_lo():
    return lax.broadcasted_iota(jnp.int32, (1, LANES), 1) < HALF


def _resident(shape):
    return pl.BlockSpec(shape, lambda i: (0,) * len(shape), pipeline_mode=pl.Buffered(1))


def _rows(tm, width):
    return pl.BlockSpec((tm, width), lambda i: (i, 0))


def _ffn_ln_value(x, wgu_ref, wd_ref, g_ref, b_ref):
    f = wd_ref.shape[0]
    xb = x.astype(BF16)
    h = _dot(xb, wgu_ref[:, :f])
    u = _dot(xb, wgu_ref[:, f:])
    a = h * (1.0 / (1.0 + jnp.exp(-h))) * u
    y = ALPHA * x + 0.5 * _dot(a.astype(BF16), wd_ref[...])
    return _layer_norm(y, g_ref[...], b_ref[...])


def _ffn_proj_kernel(x_ref, wgu_ref, wd_ref, g_ref, b_ref, *refs, n_proj):
    w_refs, o_ref, p_refs = refs[:n_proj], refs[n_proj], refs[n_proj + 1:]
    y = _ffn_ln_value(x_ref[...], wgu_ref, wd_ref, g_ref, b_ref)
    o_ref[...] = y
    yb = y.astype(BF16)
    for w_ref, p_ref in zip(w_refs, p_refs):
        p_ref[...] = _dot(yb, w_ref[...]).astype(p_ref.dtype)


def _ffn_proj(x, w_gu, w_down, g, b, projs, tm=512):
    n, d = x.shape
    tm = min(tm, n)
    ws = [w for w, _ in projs]
    return pl.pallas_call(
        functools.partial(_ffn_proj_kernel, n_proj=len(projs)),
        grid=(n // tm,),
        in_specs=[_rows(tm, d), _resident(w_gu.shape), _resident(w_down.shape), _resident((1, d)),
                  _resident((1, d))] + [_resident(w.shape) for w in ws],
        out_specs=[_rows(tm, d)] + [_rows(tm, w.shape[1]) for w in ws],
        out_shape=[jax.ShapeDtypeStruct((n, d), F32)]
        + [jax.ShapeDtypeStruct((n, w.shape[1]), dt) for w, dt in projs],
        compiler_params=_params("parallel"),
        name="ffn_in_proj",
    )(x, w_gu, w_down, g.reshape(1, d), b.reshape(1, d), *ws)


def _proj_ffn_kernel(o_ref, wo_ref, x_ref, g1_ref, b1_ref, wgu_ref, wd_ref, g2_ref, b2_ref, out_ref):
    y = ALPHA * x_ref[...] + _dot(o_ref[...], wo_ref[...])
    x2 = _layer_norm(y, g1_ref[...], b1_ref[...])
    out_ref[...] = _ffn_ln_value(x2, wgu_ref, wd_ref, g2_ref, b2_ref)


def _proj_ffn(o, w_o, x, g1, b1, w_gu, w_down, g2, b2, tm=512):
    n, k = o.shape
    d = w_o.shape[1]
    tm = min(tm, n)
    vec = lambda a: a.reshape(1, d)
    return pl.pallas_call(
        _proj_ffn_kernel,
        grid=(n // tm,),
        in_specs=[_rows(tm, k), _resident(w_o.shape), _rows(tm, d), _resident((1, d)), _resident((1, d)),
                  _resident(w_gu.shape), _resident(w_down.shape), _resident((1, d)), _resident((1, d))],
        out_specs=_rows(tm, d),
        out_shape=jax.ShapeDtypeStruct((n, d), F32),
        compiler_params=_params("parallel"),
        name="out_proj_ffn",
    )(o, w_o, x, vec(g1), vec(b1), w_gu, w_down, vec(g2), vec(b2))


def _swa_kernel(sink_ref, q_ref, kc_ref, kp_ref, vc_ref, vp_ref, o_ref, *, tq):
    i = pl.program_id(1)
    t = SWA_BLOCK
    rep = SWA_HEADS // SWA_KV_HEADS
    kvs = range(SWA_KV_HEADS)
    lo = _lane_lo()
    rel = _tile_rel(t, 2 * t) + t
    col = lax.broadcasted_iota(jnp.int32, (t, 2 * t), 1)
    dist = rel.astype(F32)
    in_band = jnp.logical_and(rel >= 0, rel < SWA_BLOCK)
    slopes = [s * LOG2E for s in _alibi_slopes(SWA_HEADS)]
    lane = lambda g: slice(g * LANES, (g + 1) * LANES)
    for qb in range(tq // t):
        r0 = qb * t
        if qb == 0:
            ks = [jnp.concatenate([kp_ref[0, :, lane(g)], kc_ref[0, 0:t, lane(g)]], axis=0) for g in kvs]
            vs = [jnp.concatenate([vp_ref[0, :, lane(g)], vc_ref[0, 0:t, lane(g)]], axis=0) for g in kvs]
            valid = jnp.logical_and(in_band, col >= jnp.where(i == 0, t, 0))
        else:
            ks = [kc_ref[0, r0 - t:r0 + t, lane(g)] for g in kvs]
            vs = [vc_ref[0, r0 - t:r0 + t, lane(g)] for g in kvs]
            valid = in_band
        qst = []
        for g in kvs:
            parts = []
            for r in range(rep):
                h = g * rep + r
                qp = q_ref[0, r0:r0 + t, lane(h // 2)]
                parts.append(jnp.where(lo if h % 2 == 0 else jnp.logical_not(lo), qp, jnp.zeros_like(qp)))
            qst.append(jnp.concatenate(parts, axis=0))
        ss = [_dot_t(qst[g], ks[g]) for g in kvs]
        ps, dens = [], []
        for g in kvs:
            prow, drow = [], []
            for r in range(rep):
                h = g * rep + r
                s = jnp.where(valid, ss[g][r * t:(r + 1) * t] - slopes[h] * dist, NEG)
                sink = sink_ref[h]
                mx = jnp.maximum(jnp.max(s, axis=-1, keepdims=True), sink)
                p = jnp.exp2(s - mx)
                drow.append(jnp.sum(p, axis=-1, keepdims=True) + jnp.exp2(sink - mx))
                prow.append(p.astype(BF16))
            ps.append(jnp.concatenate(prow, axis=0))
            dens.append(drow)
        outs = [_dot(ps[g], vs[g]) for g in kvs]
        for g in kvs:
            for r in range(0, rep, 2):
                even = outs[g][r * t:(r + 1) * t] / dens[g][r]
                odd = outs[g][(r + 1) * t:(r + 2) * t] / dens[g][r + 1]
                o_ref[0, r0:r0 + t, lane((g * rep + r) // 2)] = jnp.where(lo, even, odd).astype(BF16)


def _swa_attention(qkv, sinks, batch, seq, tq=512):
    t = SWA_BLOCK
    tq = min(tq, seq)
    sub = tq // t
    qw = SWA_HEADS * SWA_HEAD_DIM
    kw = SWA_KV_HEADS * LANES
    qblk, kblk, vblk = 0, qw // kw, qw // kw + 1
    prev = lambda i: jnp.maximum(i * sub - 1, 0)
    return pl.pallas_call(
        functools.partial(_swa_kernel, tq=tq),
        grid=(batch, seq // tq),
        in_specs=[pl.BlockSpec(memory_space=pltpu.SMEM),
                  pl.BlockSpec((1, tq, qw), lambda b, i: (b, i, qblk)),
                  pl.BlockSpec((1, tq, kw), lambda b, i: (b, i, kblk)),
                  pl.BlockSpec((1, t, kw), lambda b, i: (b, prev(i), kblk)),
                  pl.BlockSpec((1, tq, kw), lambda b, i: (b, i, vblk)),
                  pl.BlockSpec((1, t, kw), lambda b, i: (b, prev(i), vblk))],
        out_specs=pl.BlockSpec((1, tq, qw), lambda b, i: (b, i, 0)),
        out_shape=jax.ShapeDtypeStruct((batch, seq, qw), BF16),
        compiler_params=_params("parallel", "arbitrary"),
        name="swa_attention",
    )(sinks, qkv, qkv, qkv, qkv, qkv)


def _mla_up_kernel(c_ref, gq_ref, gkv_ref, wq_ref, wk_ref, wv_ref, cq_ref, s1q_ref, s2q_ref,
                   ck_ref, s1k_ref, s2k_ref, q_ref, k_ref, v_ref):
    c = c_ref[...]
    cq = c[:, :MLA_Q_LORA]
    ckv = c[:, MLA_Q_LORA:MLA_Q_LORA + MLA_KV_LORA]
    kr = c[:, MLA_Q_LORA + MLA_KV_LORA:]
    cq = cq * lax.rsqrt(jnp.mean(cq * cq, axis=-1, keepdims=True) + RMS_EPS) * gq_ref[...]
    ckv = ckv * lax.rsqrt(jnp.mean(ckv * ckv, axis=-1, keepdims=True) + RMS_EPS) * gkv_ref[...]
    ckv_b = ckv.astype(BF16)
    q = _dot(cq.astype(BF16), wq_ref[...])
    kn = _dot(ckv_b, wk_ref[...])
    v_ref[...] = _dot(ckv_b, wv_ref[...]).astype(BF16)

    def rot(xh, cc, s1, s2):
        return xh * cc + pltpu.roll(xh, 16, 1) * s1 + pltpu.roll(xh, LANES - 16, 1) * s2

    kr = rot(kr, ck_ref[...], s1k_ref[...], s2k_ref[...])
    cq_t, s1q, s2q = cq_ref[...], s1q_ref[...], s2q_ref[...]
    for h in range(MLA_HEADS):
        sl = slice(h * LANES, (h + 1) * LANES)
        q_ref[:, sl] = rot(q[:, sl], cq_t, s1q, s2q).astype(BF16)
        k_ref[:, sl] = (kn[:, sl] + kr).astype(BF16)


def _mla_rope_tables(seq, scale):
    half = MLA_ROPE // 2
    inv = ROPE_THETA ** (-jnp.arange(0, MLA_ROPE, 2, dtype=F32) / MLA_ROPE)
    ang = jnp.arange(seq, dtype=F32)[:, None] * inv[None, :]
    cos, sin = jnp.cos(ang), jnp.sin(ang)
    z = lambda w: jnp.zeros((seq, w), F32)
    one = jnp.ones((seq, MLA_NOPE), F32)
    cc = jnp.concatenate([one, cos, cos, z(LANES - MLA_NOPE - MLA_ROPE)], axis=1) * scale
    s1 = jnp.concatenate([z(MLA_NOPE + half), sin, z(LANES - MLA_NOPE - MLA_ROPE)], axis=1) * scale
    s2 = jnp.concatenate([z(MLA_NOPE), -sin, z(LANES - MLA_NOPE - half)], axis=1) * scale
    return cc, s1, s2


def _mla_up(c, gq, gkv, wq, wk, wv, seq, tm=512):
    n, cw = c.shape
    tm = min(tm, seq)
    spt = seq // tm
    scale = LOG2E * (MLA_NOPE + MLA_ROPE) ** -0.5
    tq = _mla_rope_tables(seq, scale)
    tk = _mla_rope_tables(seq, 1.0)
    qk_w = MLA_HEADS * LANES
    v_w = MLA_HEADS * MLA_V
    full = lambda a: _resident(a.shape)
    tab = pl.BlockSpec((tm, LANES), lambda i: (i % spt, 0))
    gq2, gkv2 = gq.reshape(1, -1), gkv.reshape(1, -1)
    return pl.pallas_call(
        _mla_up_kernel,
        grid=(n // tm,),
        in_specs=[pl.BlockSpec((tm, cw), lambda i: (i, 0)), full(gq2), full(gkv2), full(wq), full(wk),
                  full(wv), tab, tab, tab, tab, tab, tab],
        out_specs=[pl.BlockSpec((tm, qk_w), lambda i: (i, 0)),
                   pl.BlockSpec((tm, qk_w), lambda i: (i, 0)),
                   pl.BlockSpec((tm, v_w), lambda i: (i, 0))],
        out_shape=[jax.ShapeDtypeStruct((n, qk_w), BF16), jax.ShapeDtypeStruct((n, qk_w), BF16),
                   jax.ShapeDtypeStruct((n, v_w), BF16)],
        compiler_params=_params("parallel"),
        name="mla_up_proj",
    )(c, gq2, gkv2, wq, wk, wv, *tq, *tk)


SUM_LANE = (HALF, 0)


def _with_sum_column(v):
    lane = lax.broadcasted_iota(jnp.int32, (1, LANES), 1)
    unit = lambda at: jnp.where(lane == at, 1.0, 0.0).astype(v.dtype)
    low = lane < HALF
    return [jnp.where(low, v, unit(SUM_LANE[0])), jnp.where(low, unit(SUM_LANE[1]), v)]


def _flash_finish(acc_ref):
    outs = [acc_ref[hh] / acc_ref[hh][:, SUM_LANE[hh]:SUM_LANE[hh] + 1] for hh in range(2)]
    return jnp.where(_lane_lo(), outs[0], outs[1])


def _row_max(s):
    mx = s[:, :LANES]
    for c0 in range(LANES, s.shape[1], LANES):
        mx = jnp.maximum(mx, s[:, c0:c0 + LANES])
    return jnp.max(mx, axis=-1, keepdims=True)


def _flash_init(tq):
    m = jnp.full((tq, LANES), NEG, F32)
    return (m, m)


def _flash_pair(ss, v, m_old, acc_ref):
    hs = range(2)
    vs = _with_sum_column(v)
    cols = [slice(c0, c0 + LANES) for c0 in range(0, ss[0].shape[1], LANES)]
    ms = [jnp.maximum(m_old[hh], _row_max(ss[hh])) for hh in hs]
    ps = [jnp.concatenate([jnp.exp2(ss[hh][:, c] - ms[hh]) for c in cols], axis=1).astype(BF16) for hh in hs]
    al = [jnp.exp2(m_old[hh] - ms[hh]) for hh in hs]
    for hh in hs:
        acc_ref[hh] = al[hh] * acc_ref[hh] + _dot(ps[hh], vs[hh])
    return tuple(ms)


SOFTMAX_ROWS = 32


def _flash_scratch(tq, tk):
    return [pltpu.VMEM((2, tq, tk), F32), pltpu.VMEM((2, tq, tk), BF16), pltpu.VMEM((2, tq, LANES), F32),
            pltpu.VMEM((2, tq, LANES), F32)]


def _flash_reset(m_ref, acc_ref):
    m_ref[...] = jnp.full(m_ref.shape, NEG, F32)
    acc_ref[...] = jnp.zeros(acc_ref.shape, F32)


def _flash_update(s_ref, p_ref, m_ref, acc_ref, v):
    _, tq, tk = s_ref.shape
    vs = _with_sum_column(v)
    for c in range(tq // SOFTMAX_ROWS):
        rows = slice(c * SOFTMAX_ROWS, (c + 1) * SOFTMAX_ROWS)
        for hh in range(2):
            s = s_ref[hh, rows, :]
            m_old = m_ref[hh, rows, :]
            m_new = jnp.maximum(m_old, _row_max(s))
            m_ref[hh, rows, :] = m_new
            acc_ref[hh, rows, :] = jnp.exp2(m_old - m_new) * acc_ref[hh, rows, :]
            for c0 in range(0, tk, LANES):
                p_ref[hh, rows, c0:c0 + LANES] = jnp.exp2(s[:, c0:c0 + LANES] - m_new).astype(BF16)
    for hh in range(2):
        acc_ref[hh] += _dot(p_ref[hh], vs[hh])


def _tile_rel(tq, tk):
    return (lax.broadcasted_iota(jnp.int32, (tq, tk), 0) - lax.broadcasted_iota(jnp.int32, (tq, tk), 1))


def _mla_attn_kernel(q_ref, k_ref, v_ref, o_ref, s_ref, p_ref, m_ref, acc_ref, *, tq, tk):
    r = tq // tk
    rel = _tile_rel(tq, tk)
    sls = [slice(hh * LANES, (hh + 1) * LANES) for hh in range(2)]
    for i in range(q_ref.shape[1] // tq):
        rows = slice(i * tq, (i + 1) * tq)

        def tile(j, diag, rows=rows):
            start = pl.multiple_of(j * tk, tk)
            for hh in range(2):
                s = _dot_t(q_ref[0, rows, sls[hh]], k_ref[0, pl.ds(start, tk), sls[hh]])
                if diag is not None:
                    s = jnp.where(rel >= diag * tk, s, NEG)
                s_ref[hh] = s
            _flash_update(s_ref, p_ref, m_ref, acc_ref, v_ref[0, pl.ds(start, tk), :])

        _flash_reset(m_ref, acc_ref)
        for d in range(r):
            tile(i * r + d, d)
        if i:
            pl.loop(0, i * r)(lambda j, tile=tile: tile(j, None))
        o_ref[0, rows, :] = _flash_finish(acc_ref).astype(BF16)


def _mla_attention(q, k, v, tq=512, tk=512):
    batch, seq, _ = q.shape
    tq, tk = min(tq, seq), min(tk, seq)
    return pl.pallas_call(
        functools.partial(_mla_attn_kernel, tq=tq, tk=tk),
        grid=(batch, MLA_HEADS // 2),
        in_specs=[pl.BlockSpec((1, seq, 2 * LANES), lambda b, m: (b, 0, m)),
                  pl.BlockSpec((1, seq, 2 * LANES), lambda b, m: (b, 0, m)),
                  pl.BlockSpec((1, seq, LANES), lambda b, m: (b, 0, m))],
        out_specs=pl.BlockSpec((1, seq, LANES), lambda b, m: (b, 0, m)),
        out_shape=jax.ShapeDtypeStruct(v.shape, BF16),
        scratch_shapes=_flash_scratch(tq, tk),
        compiler_params=_params("parallel", "parallel"),
        name="mla_attention",
    )(q, k, v)


def _order_key(x):
    bits = lax.bitcast_convert_type(x, jnp.int32)
    return bits ^ (lax.shift_right_arithmetic(bits, 31) & 0x7FFFFFFF)


def _dsa_kernel(qi_ref, ki_ref, wi_ref, q_ref, k_ref, v_ref, o_ref, key_ref, bias_ref,
                dist_ref, byte_ref, acc_ref, *, t, topk):
    i = pl.program_id(1)
    lo = _lane_lo()
    row = lax.broadcasted_iota(jnp.int32, (t, t), 0)
    col = lax.broadcasted_iota(jnp.int32, (t, t), 1)
    rel = row - col

    def _select():
        wi_t = jnp.transpose(wi_ref[0] * (IDX_HEADS ** -0.5))
        rel_t = col - row
        pack = 16

        def score_tile(j, carry):
            start = pl.multiple_of(j * t, t)
            ki = ki_ref[0, pl.ds(start, t), :]
            sc = jnp.zeros((t, t), F32)
            for p in range(IDX_HEADS // 2):
                qp = qi_ref[0, :, p * LANES:(p + 1) * LANES]
                for hh in range(2):
                    h = 2 * p + hh
                    qm = jnp.where(lo if hh == 0 else jnp.logical_not(lo), qp, jnp.zeros_like(qp))
                    dots = _dot_t(ki, qm)
                    sc = sc + wi_t[h:h + 1, :] * jnp.maximum(dots, 0.0)
            sc = sc + 0.0
            sc = jnp.where(rel_t >= (j - i) * t, sc, NEG)
            key_ref[j] = _order_key(sc)
            dist_ref[j] = ((i - j) * t + rel).astype(F32)
            return carry

        lax.fori_loop(0, i + 1, score_tile, 0)

        one_b, zero_b = jnp.ones((), BF16), jnp.zeros((), BF16)

        def count_ge(cand):
            cand_b = cand.astype(BF16)

            def body(j, part):
                ge = jnp.where(byte_ref[j] >= cand_b, one_b, zero_b)
                slabs = [ge[r0:r0 + pack] for r0 in range(0, t, pack)]
                while len(slabs) > 1:
                    slabs = [a + b for a, b in zip(slabs[0::2], slabs[1::2])]
                return part + slabs[0]
            part = lax.fori_loop(0, i + 1, body, jnp.zeros((pack, t), BF16))
            return jnp.sum(part.astype(F32), axis=0, keepdims=True)

        prefix = jnp.zeros((1, t), jnp.int32)
        above = jnp.zeros((1, t), F32)
        for stage in range(4):
            shift = 24 - 8 * stage

            def fill(j, carry, stage=stage, shift=shift, prefix=prefix):
                key = key_ref[j]
                if stage == 0:
                    byte = lax.shift_right_arithmetic(key, shift) + 128
                else:
                    match = lax.shift_right_arithmetic(key, shift + 8) == prefix
                    byte = jnp.where(match, lax.shift_right_logical(key, shift) & 255, -1)
                byte_ref[j] = byte.astype(F32).astype(BF16)
                return carry

            lax.fori_loop(0, i + 1, fill, 0)
            val = jnp.zeros((1, t), F32)
            rej = jnp.zeros((1, t), F32)
            for bit in (128, 64, 32, 16, 8, 4, 2, 1):
                cand = val + float(bit)
                cnt = count_ge(cand)
                ok = above + cnt >= topk
                val = jnp.where(ok, cand, val)
                rej = jnp.where(ok, rej, cnt)
            above = above + rej
            byte_i = val.astype(jnp.int32) - (128 if stage == 0 else 0)
            prefix = byte_i if stage == 0 else lax.shift_left(prefix, 8) | byte_i
        thr = prefix
        need = topk - above
        before = jnp.where(row > col, 1.0, 0.0).astype(BF16)

        def select_tile(j, seen):
            key = key_ref[j]
            eq = key == thr
            eq_b = jnp.where(eq, 1.0, 0.0).astype(BF16)
            rank = seen + _dot(before, eq_b)
            sel = jnp.logical_or(key > thr, jnp.logical_and(eq, rank < need))
            sel = jnp.logical_and(sel, rel_t >= (j - i) * t)
            bias_ref[j] = jnp.transpose(jnp.where(sel, 0.0, NEG))
            return seen + jnp.sum(eq_b.astype(F32), axis=0, keepdims=True)

        lax.fori_loop(0, i + 1, select_tile, jnp.zeros((1, t), F32))

    _select()
    all_slopes = [s * LOG2E for s in _alibi_slopes(DSA_HEADS)]
    pairs = DSA_HEADS // 2
    for m in range(pairs):
        lanes = slice(m * LANES, (m + 1) * LANES)
        g = m // (pairs // DSA_KV_HEADS)
        kv_lanes = slice(g * LANES, (g + 1) * LANES)
        qp = q_ref[0, :, lanes]
        qs = [jnp.where(lo if hh == 0 else jnp.logical_not(lo), qp, jnp.zeros_like(qp)) for hh in range(2)]
        slopes = all_slopes[2 * m:2 * m + 2]

        def tile(j, carry, qs=qs, slopes=slopes, kv_lanes=kv_lanes):
            start = pl.multiple_of(j * t, t)
            k = k_ref[0, pl.ds(start, t), kv_lanes]
            v = v_ref[0, pl.ds(start, t), kv_lanes]
            ss = [_dot_t(qs[hh], k) for hh in range(2)]
            dist, bias = dist_ref[j], bias_ref[j]
            ss = [ss[hh] - slopes[hh] * dist + bias for hh in range(2)]
            return _flash_pair(ss, v, carry, acc_ref)

        acc_ref[...] = jnp.zeros(acc_ref.shape, F32)
        carry = tile(i, _flash_init(t))
        lax.fori_loop(0, i, tile, carry)
        o_ref[0, :, lanes] = _flash_finish(acc_ref).astype(BF16)


def _dsa_attention(proj, wi, batch, seq, t=512):
    t = min(t, seq)
    nt = seq // t
    topk = min(IDX_TOPK, seq // 4)
    qw = DSA_HEADS * DSA_HEAD_DIM
    q0 = 0
    k0 = qw // LANES
    v0 = k0 + DSA_KV_HEADS
    qi0 = (v0 + DSA_KV_HEADS) * LANES // (IDX_HEADS * IDX_DIM)
    ki0 = v0 + DSA_KV_HEADS + IDX_HEADS * IDX_DIM // LANES
    kvw = DSA_KV_HEADS * LANES
    return pl.pallas_call(
        functools.partial(_dsa_kernel, t=t, topk=topk),
        grid=(batch, nt),
        in_specs=[pl.BlockSpec((1, t, IDX_HEADS * IDX_DIM), lambda b, i: (b, i, qi0)),
                  pl.BlockSpec((1, seq, LANES), lambda b, i: (b, 0, ki0)),
                  pl.BlockSpec((1, t, LANES), lambda b, i: (b, i, 0)),
                  pl.BlockSpec((1, t, qw), lambda b, i: (b, i, q0)),
                  pl.BlockSpec((1, seq, kvw), lambda b, i: (b, 0, k0 * LANES // kvw)),
                  pl.BlockSpec((1, seq, kvw), lambda b, i: (b, 0, v0 * LANES // kvw))],
        out_specs=pl.BlockSpec((1, t, qw), lambda b, i: (b, i, 0)),
        out_shape=jax.ShapeDtypeStruct((batch, seq, qw), BF16),
        scratch_shapes=[pltpu.VMEM((nt, t, t), jnp.int32), pltpu.VMEM((nt, t, t), F32),
                        pltpu.VMEM((nt, t, t), F32), pltpu.VMEM((nt, t, t), BF16),
                        pltpu.VMEM((2, t, LANES), F32)],
        compiler_params=_params("parallel", "arbitrary"),
        name="dsa_attention",
    )(proj, proj, wi, proj, proj, proj)


def _sb_kernel(q_ref, k_ref, v_ref, o_ref, later_ref, acc_ref, *, tq, tk):
    r = tq // tk
    lo = _lane_lo()
    rel = _tile_rel(tq, tk)
    suffix = jnp.where(_tile_rel(tk, tk) >= 0, 1.0, 0.0).astype(BF16)
    cols = [slice(c0, c0 + LANES) for c0 in range(0, tk, LANES)]
    hs = range(2)
    for i in range(q_ref.shape[1] // tq):
        rows = slice(i * tq, (i + 1) * tq)
        qp = q_ref[0, rows, :]
        qs = [jnp.where(lo if hh == 0 else jnp.logical_not(lo), qp, jnp.zeros_like(qp)) for hh in hs]

        def tile(j, diag, qs=qs):
            start = pl.multiple_of(j * tk, tk)
            k = k_ref[0, pl.ds(start, tk), :]
            v = v_ref[0, pl.ds(start, tk), :]
            zs = [_dot_t(qs[hh], k) for hh in hs]
            lgs = []
            for z in zs:
                nz = -z
                lgs.append(jnp.minimum(nz, 0.0) - jnp.log2(1.0 + jnp.exp2(jnp.minimum(z, nz))))
            if diag is not None:
                strict = rel > diag * tk
                lgs = [jnp.where(strict, lg, 0.0) for lg in lgs]
            lbs = [lg.astype(BF16) for lg in lgs]
            runs = [_dot(lb, suffix) for lb in lbs]
            tots = [jnp.broadcast_to(run[:, 0:1], (tq, LANES)) for run in runs]
            laters = [later_ref[hh] for hh in hs]
            a_s = [jnp.concatenate([jnp.exp2(zs[hh][:, c] + runs[hh][:, c] + laters[hh]) for c in cols], axis=1)
                   for hh in hs]
            if diag is not None:
                a_s = [jnp.where(strict, a, 0.0) for a in a_s]
            for hh in hs:
                later_ref[hh] = laters[hh] + tots[hh]
            for hh in hs:
                acc_ref[hh] += _dot(a_s[hh].astype(BF16), v)

        later_ref[...] = jnp.zeros(later_ref.shape, F32)
        acc_ref[...] = jnp.zeros(acc_ref.shape, F32)
        for d in reversed(range(r)):
            tile(i * r + d, d)
        if i:
            pl.loop(0, i * r)(lambda jj, tile=tile, i=i: tile(i * r - 1 - jj, None))
        o_ref[0, rows, :] = jnp.where(lo, acc_ref[0], acc_ref[1]).astype(BF16)


def _sb_attention(qkv, batch, seq, tq=512, tk=512):
    tq, tk = min(tq, seq), min(tk, seq)
    w = SB_HEADS * SB_HEAD_DIM
    pairs = SB_HEADS // 2
    return pl.pallas_call(
        functools.partial(_sb_kernel, tq=tq, tk=tk),
        grid=(batch, pairs),
        in_specs=[pl.BlockSpec((1, seq, LANES), lambda b, m: (b, 0, m)),
                  pl.BlockSpec((1, seq, LANES), lambda b, m: (b, 0, pairs + m)),
                  pl.BlockSpec((1, seq, LANES), lambda b, m: (b, 0, 2 * pairs + m))],
        out_specs=pl.BlockSpec((1, seq, LANES), lambda b, m: (b, 0, m)),
        out_shape=jax.ShapeDtypeStruct((batch, seq, w), BF16),
        scratch_shapes=[pltpu.VMEM((2, tq, LANES), F32), pltpu.VMEM((2, tq, LANES), F32)],
        compiler_params=_params("parallel", "parallel"),
        name="sb_attention",
    )(qkv, qkv, qkv)


def _dup_heads(w, n_heads, dim):
    d = w.shape[0]
    w = w.reshape(d, n_heads, 1, dim)
    return jnp.broadcast_to(w, (d, n_heads, 2, dim)).reshape(d, n_heads * 2 * dim)


def _pad_heads(w, n_heads, dim, to):
    d = w.shape[0]
    w = w.reshape(d, n_heads, dim)
    return jnp.pad(w, ((0, 0), (0, 0), (0, to - dim))).reshape(d, n_heads * to)


def _swa_w_in(w):
    qw = SWA_HEADS * SWA_HEAD_DIM
    kw = SWA_KV_HEADS * SWA_HEAD_DIM
    q, k, v = w[:, :qw], w[:, qw:qw + kw], w[:, qw + kw:]
    q = q * (LOG2E * SWA_HEAD_DIM ** -0.5)
    return jnp.concatenate([q, _dup_heads(k, SWA_KV_HEADS, SWA_HEAD_DIM),
                            _dup_heads(v, SWA_KV_HEADS, SWA_HEAD_DIM)], axis=1).astype(BF16)


def _mla_w_in(w):
    d = w.shape[0]
    lat = MLA_Q_LORA + MLA_KV_LORA
    z = lambda c: jnp.zeros((d, c), w.dtype)
    return jnp.concatenate([w[:, :lat], z(MLA_NOPE), w[:, lat:], z(LANES - MLA_NOPE - MLA_ROPE)],
                           axis=1).astype(BF16)


def _sb_w_in(w):
    qw = SB_HEADS * SB_HEAD_DIM
    return jnp.concatenate([w[:, :qw] * (LOG2E * SB_HEAD_DIM ** -0.5), w[:, qw:]], axis=1).astype(BF16)


def _dsa_w_in(w):
    qw = DSA_HEADS * DSA_HEAD_DIM
    kw = DSA_KV_HEADS * DSA_HEAD_DIM
    iw = IDX_HEADS * IDX_DIM
    c = np.cumsum([qw, kw, kw, iw, IDX_DIM]).tolist()
    q, k, v, qi, ki, wi = (w[:, :c[0]], w[:, c[0]:c[1]], w[:, c[1]:c[2]], w[:, c[2]:c[3]],
                           w[:, c[3]:c[4]], w[:, c[4]:])
    q = q * (LOG2E * DSA_HEAD_DIM ** -0.5)
    qi = qi * (IDX_DIM ** -0.5)
    main = jnp.concatenate([q, _dup_heads(k, DSA_KV_HEADS, DSA_HEAD_DIM),
                            _dup_heads(v, DSA_KV_HEADS, DSA_HEAD_DIM), qi, _dup_heads(ki, 1, IDX_DIM)],
                           axis=1).astype(BF16)
    wi = jnp.pad(wi, ((0, 0), (0, LANES - IDX_HEADS))).astype(BF16)
    return main, wi


def kernel(x, ffn1_w_gu, ffn1_w_down, ffn2_w_gu, ffn2_w_down, ln_g, ln_b, a_w_in, a_sinks, a_w_o,
           b_w_in, b_q_norm, b_w_uq, b_kv_norm, b_w_ukv, b_w_o, c_w_in, c_w_o, d_w_in, d_w_o):
    batch, seq, d = x.shape
    n = batch * seq
    x = x.reshape(n, d)
    for i in range(DEPTH):
        mixer, j = i % N_MIXERS, i // N_MIXERS

        def ffn1(projs):
            return _ffn_proj(x, ffn1_w_gu[i].astype(BF16), ffn1_w_down[i].astype(BF16), ln_g[i, 0],
                             ln_b[i, 0], projs)

        if mixer == 0:
            x, qkv = ffn1([(_swa_w_in(a_w_in[j]), BF16)])
            o = _swa_attention(qkv.reshape(batch, seq, -1), a_sinks[j] * LOG2E, batch, seq)
            w_o = a_w_o[j]
        elif mixer == 1:
            x, c = ffn1([(_mla_w_in(b_w_in[j]), F32)])
            hd = MLA_NOPE + MLA_ROPE
            wq = _pad_heads(b_w_uq[j], MLA_HEADS, hd, LANES).astype(BF16)
            wkv = b_w_ukv[j].reshape(MLA_KV_LORA, MLA_HEADS, MLA_NOPE + MLA_V)
            wk = _pad_heads(wkv[:, :, :MLA_NOPE].reshape(MLA_KV_LORA, -1), MLA_HEADS, MLA_NOPE,
                            LANES).astype(BF16)
            wv = wkv[:, :, MLA_NOPE:].reshape(MLA_KV_LORA, -1).astype(BF16)
            q, k, v = _mla_up(c, b_q_norm[j], b_kv_norm[j], wq, wk, wv, seq)
            o = _mla_attention(q.reshape(batch, seq, -1), k.reshape(batch, seq, -1),
                               v.reshape(batch, seq, -1))
            w_o = b_w_o[j]
        elif mixer == 2:
            w_main, w_wi = _dsa_w_in(c_w_in[j])
            x, proj, wi = ffn1([(w_main, BF16), (w_wi, F32)])
            o = _dsa_attention(proj.reshape(batch, seq, -1), wi.reshape(batch, seq, -1), batch, seq)
            w_o = c_w_o[j]
        else:
            x, qkv = ffn1([(_sb_w_in(d_w_in[j]), BF16)])
            o = _sb_attention(qkv.reshape(batch, seq, -1), batch, seq)
            w_o = d_w_o[j]
        x = _proj_ffn(o.reshape(n, -1), w_o.astype(BF16), x, ln_g[i, 1], ln_b[i, 1],
                      ffn2_w_gu[i].astype(BF16), ffn2_w_down[i].astype(BF16), ln_g[i, 2], ln_b[i, 2])
    return x.reshape(batch, seq, d)
```

```python
import functools
import math

import numpy as np
import jax
import jax.numpy as jnp
from jax import lax
from jax.experimental import pallas as pl
from jax.experimental.pallas import tpu as pltpu

D_MODEL = 1024
DEPTH = 4
N_MIXERS = 4
D_FF = 2816
LN_EPS = 1e-5
RMS_EPS = 1e-6
NEG = -1e30

SWA_HEADS, SWA_KV_HEADS, SWA_HEAD_DIM, SWA_BLOCK = 16, 4, 64, 128
MLA_HEADS, MLA_Q_LORA, MLA_KV_LORA, MLA_NOPE, MLA_ROPE, MLA_V = 16, 768, 256, 64, 32, 64
ROPE_THETA = 10000.0
DSA_HEADS, DSA_KV_HEADS, DSA_HEAD_DIM = 16, 4, 64
IDX_HEADS, IDX_DIM, IDX_TOPK = 8, 64, 256
SB_HEADS, SB_HEAD_DIM = 16, 64

ALPHA = (2.0 * DEPTH) ** 0.25
LOG2E = math.log2(math.e)

LANES = 128
HALF = 64
VMEM_LIMIT = 56 * 1024 * 1024
BF16 = jnp.bfloat16
F32 = jnp.float32


def _params(*sem, flags=None):
    return pltpu.CompilerParams(dimension_semantics=sem, vmem_limit_bytes=VMEM_LIMIT, flags=flags)


def _dot(a, b):
    return jnp.dot(a, b, preferred_element_type=F32)


def _dot_t(a, b):
    return lax.dot_general(a, b, (((1,), (1,)), ((), ())), preferred_element_type=F32)


def _alibi_slopes(n_heads):
    return [2.0 ** (-8.0 * (i + 1) / n_heads) for i in range(n_heads)]


def _layer_norm(y, g, b):
    mu = jnp.mean(y, axis=-1, keepdims=True)
    yc = y - mu
    var = jnp.mean(yc * yc, axis=-1, keepdims=True)
    return yc * lax.rsqrt(var + LN_EPS) * g + b


def _lane_lo():
    return lax.broadcasted_iota(jnp.int32, (1, LANES), 1) < HALF


def _resident(shape):
    return pl.BlockSpec(shape, lambda i: (0,) * len(shape), pipeline_mode=pl.Buffered(1))


def _rows(tm, width):
    return pl.BlockSpec((tm, width), lambda i: (i, 0))


def _ffn_ln_value(x, wgu_ref, wd_ref, g_ref, b_ref):
    f = wd_ref.shape[0]
    xb = x.astype(BF16)
    h = _dot(xb, wgu_ref[:, :f])
    u = _dot(xb, wgu_ref[:, f:])
    a = h * (1.0 / (1.0 + jnp.exp(-h))) * u
    y = ALPHA * x + 0.5 * _dot(a.astype(BF16), wd_ref[...])
    return _layer_norm(y, g_ref[...], b_ref[...])


def _ffn_proj_kernel(x_ref, wgu_ref, wd_ref, g_ref, b_ref, *refs, n_proj):
    w_refs, o_ref, p_refs = refs[:n_proj], refs[n_proj], refs[n_proj + 1:]
    y = _ffn_ln_value(x_ref[...], wgu_ref, wd_ref, g_ref, b_ref)
    o_ref[...] = y
    yb = y.astype(BF16)
    for w_ref, p_ref in zip(w_refs, p_refs):
        p_ref[...] = _dot(yb, w_ref[...]).astype(p_ref.dtype)


def _ffn_proj(x, w_gu, w_down, g, b, projs, tm=512):
    n, d = x.shape
    tm = min(tm, n)
    ws = [w for w, _ in projs]
    return pl.pallas_call(
        functools.partial(_ffn_proj_kernel, n_proj=len(projs)),
        grid=(n // tm,),
        in_specs=[_rows(tm, d), _resident(w_gu.shape), _resident(w_down.shape), _resident((1, d)),
                  _resident((1, d))] + [_resident(w.shape) for w in ws],
        out_specs=[_rows(tm, d)] + [_rows(tm, w.shape[1]) for w in ws],
        out_shape=[jax.ShapeDtypeStruct((n, d), F32)]
        + [jax.ShapeDtypeStruct((n, w.shape[1]), dt) for w, dt in projs],
        compiler_params=_params("parallel"),
        name="ffn_in_proj",
    )(x, w_gu, w_down, g.reshape(1, d), b.reshape(1, d), *ws)


def _proj_ffn_kernel(o_ref, wo_ref, x_ref, g1_ref, b1_ref, wgu_ref, wd_ref, g2_ref, b2_ref, out_ref):
    y = ALPHA * x_ref[...] + _dot(o_ref[...], wo_ref[...])
    x2 = _layer_norm(y, g1_ref[...], b1_ref[...])
    out_ref[...] = _ffn_ln_value(x2, wgu_ref, wd_ref, g2_ref, b2_ref)


def _proj_ffn(o, w_o, x, g1, b1, w_gu, w_down, g2, b2, tm=512):
    n, k = o.shape
    d = w_o.shape[1]
    tm = min(tm, n)
    vec = lambda a: a.reshape(1, d)
    return pl.pallas_call(
        _proj_ffn_kernel,
        grid=(n // tm,),
        in_specs=[_rows(tm, k), _resident(w_o.shape), _rows(tm, d), _resident((1, d)), _resident((1, d)),
                  _resident(w_gu.shape), _resident(w_down.shape), _resident((1, d)), _resident((1, d))],
        out_specs=_rows(tm, d),
        out_shape=jax.ShapeDtypeStruct((n, d), F32),
        compiler_params=_params("parallel"),
        name="out_proj_ffn",
    )(o, w_o, x, vec(g1), vec(b1), w_gu, w_down, vec(g2), vec(b2))


def _swa_kernel(sink_ref, q_ref, kc_ref, kp_ref, vc_ref, vp_ref, o_ref, *, tq):
    i = pl.program_id(1)
    t = SWA_BLOCK
    rep = SWA_HEADS // SWA_KV_HEADS
    kvs = range(SWA_KV_HEADS)
    lo = _lane_lo()
    rel = _tile_rel(t, 2 * t) + t
    col = lax.broadcasted_iota(jnp.int32, (t, 2 * t), 1)
    dist = rel.astype(F32)
    in_band = jnp.logical_and(rel >= 0, rel < SWA_BLOCK)
    slopes = [s * LOG2E for s in _alibi_slopes(SWA_HEADS)]
    lane = lambda g: slice(g * LANES, (g + 1) * LANES)
    for qb in range(tq // t):
        r0 = qb * t
        if qb == 0:
            ks = [jnp.concatenate([kp_ref[0, :, lane(g)], kc_ref[0, 0:t, lane(g)]], axis=0) for g in kvs]
            vs = [jnp.concatenate([vp_ref[0, :, lane(g)], vc_ref[0, 0:t, lane(g)]], axis=0) for g in kvs]
            valid = jnp.logical_and(in_band, col >= jnp.where(i == 0, t, 0))
        else:
            ks = [kc_ref[0, r0 - t:r0 + t, lane(g)] for g in kvs]
            vs = [vc_ref[0, r0 - t:r0 + t, lane(g)] for g in kvs]
            valid = in_band
        qst = []
        for g in kvs:
            parts = []
            for r in range(rep):
                h = g * rep + r
                qp = q_ref[0, r0:r0 + t, lane(h // 2)]
                parts.append(jnp.where(lo if h % 2 == 0 else jnp.logical_not(lo), qp, jnp.zeros_like(qp)))
            qst.append(jnp.concatenate(parts, axis=0))
        ss = [_dot_t(qst[g], ks[g]) for g in kvs]
        ps, dens = [], []
        for g in kvs:
            prow, drow = [], []
            for r in range(rep):
                h = g * rep + r
                s = jnp.where(valid, ss[g][r * t:(r + 1) * t] - slopes[h] * dist, NEG)
                sink = sink_ref[h]
                mx = jnp.maximum(jnp.max(s, axis=-1, keepdims=True), sink)
                p = jnp.exp2(s - mx)
                drow.append(jnp.sum(p, axis=-1, keepdims=True) + jnp.exp2(sink - mx))
                prow.append(p.astype(BF16))
            ps.append(jnp.concatenate(prow, axis=0))
            dens.append(drow)
        outs = [_dot(ps[g], vs[g]) for g in kvs]
        for g in kvs:
            for r in range(0, rep, 2):
                even = outs[g][r * t:(r + 1) * t] / dens[g][r]
                odd = outs[g][(r + 1) * t:(r + 2) * t] / dens[g][r + 1]
                o_ref[0, r0:r0 + t, lane((g * rep + r) // 2)] = jnp.where(lo, even, odd).astype(BF16)


def _swa_attention(qkv, sinks, batch, seq, tq=512):
    t = SWA_BLOCK
    tq = min(tq, seq)
    sub = tq // t
    qw = SWA_HEADS * SWA_HEAD_DIM
    kw = SWA_KV_HEADS * LANES
    qblk, kblk, vblk = 0, qw // kw, qw // kw + 1
    prev = lambda i: jnp.maximum(i * sub - 1, 0)
    return pl.pallas_call(
        functools.partial(_swa_kernel, tq=tq),
        grid=(batch, seq // tq),
        in_specs=[pl.BlockSpec(memory_space=pltpu.SMEM),
                  pl.BlockSpec((1, tq, qw), lambda b, i: (b, i, qblk)),
                  pl.BlockSpec((1, tq, kw), lambda b, i: (b, i, kblk)),
                  pl.BlockSpec((1, t, kw), lambda b, i: (b, prev(i), kblk)),
                  pl.BlockSpec((1, tq, kw), lambda b, i: (b, i, vblk)),
                  pl.BlockSpec((1, t, kw), lambda b, i: (b, prev(i), vblk))],
        out_specs=pl.BlockSpec((1, tq, qw), lambda b, i: (b, i, 0)),
        out_shape=jax.ShapeDtypeStruct((batch, seq, qw), BF16),
        compiler_params=_params("parallel", "arbitrary"),
        name="swa_attention",
    )(sinks, qkv, qkv, qkv, qkv, qkv)


def _mla_up_kernel(c_ref, gq_ref, gkv_ref, wq_ref, wk_ref, wv_ref, cq_ref, s1q_ref, s2q_ref,
                   ck_ref, s1k_ref, s2k_ref, q_ref, k_ref, v_ref):
    c = c_ref[...]
    cq = c[:, :MLA_Q_LORA]
    ckv = c[:, MLA_Q_LORA:MLA_Q_LORA + MLA_KV_LORA]
    kr = c[:, MLA_Q_LORA + MLA_KV_LORA:]
    cq = cq * lax.rsqrt(jnp.mean(cq * cq, axis=-1, keepdims=True) + RMS_EPS) * gq_ref[...]
    ckv = ckv * lax.rsqrt(jnp.mean(ckv * ckv, axis=-1, keepdims=True) + RMS_EPS) * gkv_ref[...]
    ckv_b = ckv.astype(BF16)
    q = _dot(cq.astype(BF16), wq_ref[...])
    kn = _dot(ckv_b, wk_ref[...])
    v_ref[...] = _dot(ckv_b, wv_ref[...]).astype(BF16)

    def rot(xh, cc, s1, s2):
        return xh * cc + pltpu.roll(xh, 16, 1) * s1 + pltpu.roll(xh, LANES - 16, 1) * s2

    kr = rot(kr, ck_ref[...], s1k_ref[...], s2k_ref[...])
    cq_t, s1q, s2q = cq_ref[...], s1q_ref[...], s2q_ref[...]
    for h in range(MLA_HEADS):
        sl = slice(h * LANES, (h + 1) * LANES)
        q_ref[:, sl] = rot(q[:, sl], cq_t, s1q, s2q).astype(BF16)
        k_ref[:, sl] = (kn[:, sl] + kr).astype(BF16)


def _mla_rope_tables(seq, scale):
    half = MLA_ROPE // 2
    inv = ROPE_THETA ** (-jnp.arange(0, MLA_ROPE, 2, dtype=F32) / MLA_ROPE)
    ang = jnp.arange(seq, dtype=F32)[:, None] * inv[None, :]
    cos, sin = jnp.cos(ang), jnp.sin(ang)
    z = lambda w: jnp.zeros((seq, w), F32)
    one = jnp.ones((seq, MLA_NOPE), F32)
    cc = jnp.concatenate([one, cos, cos, z(LANES - MLA_NOPE - MLA_ROPE)], axis=1) * scale
    s1 = jnp.concatenate([z(MLA_NOPE + half), sin, z(LANES - MLA_NOPE - MLA_ROPE)], axis=1) * scale
    s2 = jnp.concatenate([z(MLA_NOPE), -sin, z(LANES - MLA_NOPE - half)], axis=1) * scale
    return cc, s1, s2


def _mla_up(c, gq, gkv, wq, wk, wv, seq, tm=512):
    n, cw = c.shape
    tm = min(tm, seq)
    spt = seq // tm
    scale = LOG2E * (MLA_NOPE + MLA_ROPE) ** -0.5
    tq = _mla_rope_tables(seq, scale)
    tk = _mla_rope_tables(seq, 1.0)
    qk_w = MLA_HEADS * LANES
    v_w = MLA_HEADS * MLA_V
    full = lambda a: _resident(a.shape)
    tab = pl.BlockSpec((tm, LANES), lambda i: (i % spt, 0))
    gq2, gkv2 = gq.reshape(1, -1), gkv.reshape(1, -1)
    return pl.pallas_call(
        _mla_up_kernel,
        grid=(n // tm,),
        in_specs=[pl.BlockSpec((tm, cw), lambda i: (i, 0)), full(gq2), full(gkv2), full(wq), full(wk),
                  full(wv), tab, tab, tab, tab, tab, tab],
        out_specs=[pl.BlockSpec((tm, qk_w), lambda i: (i, 0)),
                   pl.BlockSpec((tm, qk_w), lambda i: (i, 0)),
                   pl.BlockSpec((tm, v_w), lambda i: (i, 0))],
        out_shape=[jax.ShapeDtypeStruct((n, qk_w), BF16), jax.ShapeDtypeStruct((n, qk_w), BF16),
                   jax.ShapeDtypeStruct((n, v_w), BF16)],
        compiler_params=_params("parallel"),
        name="mla_up_proj",
    )(c, gq2, gkv2, wq, wk, wv, *tq, *tk)


SUM_LANE = (HALF, 0)


def _with_sum_column(v):
    lane = lax.broadcasted_iota(jnp.int32, (1, LANES), 1)
    unit = lambda at: jnp.where(lane == at, 1.0, 0.0).astype(v.dtype)
    low = lane < HALF
    return [jnp.where(low, v, unit(SUM_LANE[0])), jnp.where(low, unit(SUM_LANE[1]), v)]


def _flash_finish(acc_ref):
    outs = [acc_ref[hh] / acc_ref[hh][:, SUM_LANE[hh]:SUM_LANE[hh] + 1] for hh in range(2)]
    return jnp.where(_lane_lo(), outs[0], outs[1])


def _row_max(s):
    mx = s[:, :LANES]
    for c0 in range(LANES, s.shape[1], LANES):
        mx = jnp.maximum(mx, s[:, c0:c0 + LANES])
    return jnp.max(mx, axis=-1, keepdims=True)


def _flash_init(tq):
    m = jnp.full((tq, LANES), NEG, F32)
    return (m, m)


def _flash_pair(ss, v, m_old, acc_ref):
    hs = range(2)
    vs = _with_sum_column(v)
    cols = [slice(c0, c0 + LANES) for c0 in range(0, ss[0].shape[1], LANES)]
    ms = [jnp.maximum(m_old[hh], _row_max(ss[hh])) for hh in hs]
    ps = [jnp.concatenate([jnp.exp2(ss[hh][:, c] - ms[hh]) for c in cols], axis=1).astype(BF16) for hh in hs]
    al = [jnp.exp2(m_old[hh] - ms[hh]) for hh in hs]
    for hh in hs:
        acc_ref[hh] = al[hh] * acc_ref[hh] + _dot(ps[hh], vs[hh])
    return tuple(ms)


SOFTMAX_ROWS = 32


def _flash_scratch(tq, tk):
    return [pltpu.VMEM((2, tq, tk), F32), pltpu.VMEM((2, tq, tk), BF16), pltpu.VMEM((2, tq, LANES), F32),
            pltpu.VMEM((2, tq, LANES), F32)]


def _flash_reset(m_ref, acc_ref):
    m_ref[...] = jnp.full(m_ref.shape, NEG, F32)
    acc_ref[...] = jnp.zeros(acc_ref.shape, F32)


def _flash_update(s_ref, p_ref, m_ref, acc_ref, v):
    _, tq, tk = s_ref.shape
    vs = _with_sum_column(v)
    for c in range(tq // SOFTMAX_ROWS):
        rows = slice(c * SOFTMAX_ROWS, (c + 1) * SOFTMAX_ROWS)
        for hh in range(2):
            s = s_ref[hh, rows, :]
            m_old = m_ref[hh, rows, :]
            m_new = jnp.maximum(m_old, _row_max(s))
            m_ref[hh, rows, :] = m_new
            acc_ref[hh, rows, :] = jnp.exp2(m_old - m_new) * acc_ref[hh, rows, :]
            for c0 in range(0, tk, LANES):
                p_ref[hh, rows, c0:c0 + LANES] = jnp.exp2(s[:, c0:c0 + LANES] - m_new).astype(BF16)
    for hh in range(2):
        acc_ref[hh] += _dot(p_ref[hh], vs[hh])


def _tile_rel(tq, tk):
    return (lax.broadcasted_iota(jnp.int32, (tq, tk), 0) - lax.broadcasted_iota(jnp.int32, (tq, tk), 1))


def _mla_attn_kernel(q_ref, k_ref, v_ref, o_ref, s_ref, p_ref, m_ref, acc_ref, *, tq, tk):
    r = tq // tk
    rel = _tile_rel(tq, tk)
    sls = [slice(hh * LANES, (hh + 1) * LANES) for hh in range(2)]
    for i in range(q_ref.shape[1] // tq):
        rows = slice(i * tq, (i + 1) * tq)

        def tile(j, diag, rows=rows):
            start = pl.multiple_of(j * tk, tk)
            for hh in range(2):
                s = _dot_t(q_ref[0, rows, sls[hh]], k_ref[0, pl.ds(start, tk), sls[hh]])
                if diag is not None:
                    s = jnp.where(rel >= diag * tk, s, NEG)
                s_ref[hh] = s
            _flash_update(s_ref, p_ref, m_ref, acc_ref, v_ref[0, pl.ds(start, tk), :])

        _flash_reset(m_ref, acc_ref)
        for d in range(r):
            tile(i * r + d, d)
        if i:
            pl.loop(0, i * r)(lambda j, tile=tile: tile(j, None))
        o_ref[0, rows, :] = _flash_finish(acc_ref).astype(BF16)


def _mla_attention(q, k, v, tq=512, tk=512):
    batch, seq, _ = q.shape
    tq, tk = min(tq, seq), min(tk, seq)
    return pl.pallas_call(
        functools.partial(_mla_attn_kernel, tq=tq, tk=tk),
        grid=(batch, MLA_HEADS // 2),
        in_specs=[pl.BlockSpec((1, seq, 2 * LANES), lambda b, m: (b, 0, m)),
                  pl.BlockSpec((1, seq, 2 * LANES), lambda b, m: (b, 0, m)),
                  pl.BlockSpec((1, seq, LANES), lambda b, m: (b, 0, m))],
        out_specs=pl.BlockSpec((1, seq, LANES), lambda b, m: (b, 0, m)),
        out_shape=jax.ShapeDtypeStruct(v.shape, BF16),
        scratch_shapes=_flash_scratch(tq, tk),
        compiler_params=_params("parallel", "parallel"),
        name="mla_attention",
    )(q, k, v)


def _order_key(x):
    bits = lax.bitcast_convert_type(x, jnp.int32)
    return bits ^ (lax.shift_right_arithmetic(bits, 31) & 0x7FFFFFFF)


def _dsa_kernel(slope_ref, pos_ref, qi_ref, ki_ref, wi_ref, q_ref, k_ref, v_ref, o_ref, key_ref, bias_ref,
                byte_ref, acc_ref, *, t, topk):
    i = pl.program_id(1)
    lo = _lane_lo()
    row = lax.broadcasted_iota(jnp.int32, (t, t), 0)
    col = lax.broadcasted_iota(jnp.int32, (t, t), 1)
    rel = row - col

    def _select():
        wi_t = jnp.transpose(wi_ref[0] * (IDX_HEADS ** -0.5))
        rel_t = col - row
        pack = 16

        def score_tile(j, carry):
            start = pl.multiple_of(j * t, t)
            ki = ki_ref[0, pl.ds(start, t), :]
            sc = jnp.zeros((t, t), F32)
            for p in range(IDX_HEADS // 2):
                qp = qi_ref[0, :, p * LANES:(p + 1) * LANES]
                for hh in range(2):
                    h = 2 * p + hh
                    qm = jnp.where(lo if hh == 0 else jnp.logical_not(lo), qp, jnp.zeros_like(qp))
                    dots = _dot_t(ki, qm)
                    sc = sc + wi_t[h:h + 1, :] * jnp.maximum(dots, 0.0)
            sc = sc + 0.0
            sc = jnp.where(rel_t >= (j - i) * t, sc, NEG)
            key_ref[j] = _order_key(sc)
            return carry

        lax.fori_loop(0, i + 1, score_tile, 0)

        one_b, zero_b = jnp.ones((), BF16), jnp.zeros((), BF16)

        def count_ge(cand):
            cand_b = cand.astype(BF16)

            def body(j, part):
                ge = jnp.where(byte_ref[j] >= cand_b, one_b, zero_b)
                slabs = [ge[r0:r0 + pack] for r0 in range(0, t, pack)]
                while len(slabs) > 1:
                    slabs = [a + b for a, b in zip(slabs[0::2], slabs[1::2])]
                return part + slabs[0]
            part = lax.fori_loop(0, i + 1, body, jnp.zeros((pack, t), BF16))
            return jnp.sum(part.astype(F32), axis=0, keepdims=True)

        prefix = jnp.zeros((1, t), jnp.int32)
        above = jnp.zeros((1, t), F32)
        for stage in range(4):
            shift = 24 - 8 * stage

            def fill(j, carry, stage=stage, shift=shift, prefix=prefix):
                key = key_ref[j]
                if stage == 0:
                    byte = lax.shift_right_arithmetic(key, shift) + 128
                else:
                    match = lax.shift_right_arithmetic(key, shift + 8) == prefix
                    byte = jnp.where(match, lax.shift_right_logical(key, shift) & 255, -1)
                byte_ref[j] = byte.astype(F32).astype(BF16)
                return carry

            lax.fori_loop(0, i + 1, fill, 0)
            val = jnp.zeros((1, t), F32)
            rej = jnp.zeros((1, t), F32)
            for bit in (128, 64, 32, 16, 8, 4, 2, 1):
                cand = val + float(bit)
                cnt = count_ge(cand)
                ok = above + cnt >= topk
                val = jnp.where(ok, cand, val)
                rej = jnp.where(ok, rej, cnt)
            above = above + rej
            byte_i = val.astype(jnp.int32) - (128 if stage == 0 else 0)
            prefix = byte_i if stage == 0 else lax.shift_left(prefix, 8) | byte_i
        thr = prefix
        need = topk - above
        before = jnp.where(row > col, 1.0, 0.0).astype(BF16)

        def select_tile(j, seen):
            key = key_ref[j]
            eq = key == thr
            eq_b = jnp.where(eq, 1.0, 0.0).astype(BF16)
            rank = seen + _dot(before, eq_b)
            sel = jnp.logical_or(key > thr, jnp.logical_and(eq, rank < need))
            sel = jnp.logical_and(sel, rel_t >= (j - i) * t)
            bias_ref[j] = jnp.transpose(jnp.where(sel, 0.0, NEG))
            return seen + jnp.sum(eq_b.astype(F32), axis=0, keepdims=True)

        lax.fori_loop(0, i + 1, select_tile, jnp.zeros((1, t), F32))

    _select()
    pairs = DSA_HEADS // 2
    for m in range(pairs):
        lanes = slice(m * LANES, (m + 1) * LANES)
        g = m // (pairs // DSA_KV_HEADS)
        kv_lanes = slice(g * LANES, (g + 1) * LANES)
        qp = q_ref[0, :, lanes]
        qs = [jnp.where(lo, qp, slope_ref[2 * m:2 * m + 1, :]),
              jnp.where(lo, slope_ref[2 * m + 1:2 * m + 2, :], qp)]

        def tile(j, carry, qs=qs, kv_lanes=kv_lanes):
            start = pl.multiple_of(j * t, t)
            k = k_ref[0, pl.ds(start, t), kv_lanes]
            v = v_ref[0, pl.ds(start, t), kv_lanes]
            ks = [jnp.where(lo, k, pos_ref[0, pl.ds(start, t), :]), jnp.where(lo, pos_ref[1, pl.ds(start, t), :], k)]
            bias = bias_ref[j]
            ss = [_dot_t(qs[hh], ks[hh]) + bias for hh in range(2)]
            return _flash_pair(ss, v, carry, acc_ref)

        acc_ref[...] = jnp.zeros(acc_ref.shape, F32)
        carry = tile(i, _flash_init(t))
        lax.fori_loop(0, i, tile, carry)
        o_ref[0, :, lanes] = _flash_finish(acc_ref).astype(BF16)


ALIBI_PIECES = 6


def _dsa_alibi_tables(seq):
    bf = lambda a: np.asarray(np.asarray(a, np.float32).astype(jnp.bfloat16), np.float32)
    slopes = np.asarray(_alibi_slopes(DSA_HEADS), np.float64) * LOG2E
    slopes = slopes.astype(np.float32)
    s1 = bf(slopes)
    s2 = bf(slopes - s1)
    s3 = bf(slopes - s1 - s2)
    slope_tab = np.zeros((DSA_HEADS, LANES), np.float32)
    for h in range(DSA_HEADS):
        off = HALF if h % 2 == 0 else 0
        slope_tab[h, off:off + ALIBI_PIECES] = [s1[h], s2[h], s3[h], s1[h], s2[h], s3[h]]
    pos = np.arange(seq)
    hi, lw = (pos // 256 * 256).astype(np.float32), (pos % 256).astype(np.float32)
    pieces = np.stack([hi, hi, hi, lw, lw, lw], axis=1)
    pos_tab = np.zeros((2, seq, LANES), np.float32)
    pos_tab[0, :, HALF:HALF + ALIBI_PIECES] = pieces
    pos_tab[1, :, :ALIBI_PIECES] = pieces
    return jnp.asarray(slope_tab, BF16), jnp.asarray(pos_tab, BF16)


def _dsa_attention(proj, wi, batch, seq, t=512):
    t = min(t, seq)
    nt = seq // t
    topk = min(IDX_TOPK, seq // 4)
    qw = DSA_HEADS * DSA_HEAD_DIM
    q0 = 0
    k0 = qw // LANES
    v0 = k0 + DSA_KV_HEADS
    qi0 = (v0 + DSA_KV_HEADS) * LANES // (IDX_HEADS * IDX_DIM)
    ki0 = v0 + DSA_KV_HEADS + IDX_HEADS * IDX_DIM // LANES
    kvw = DSA_KV_HEADS * LANES
    slope_tab, pos_tab = _dsa_alibi_tables(seq)
    return pl.pallas_call(
        functools.partial(_dsa_kernel, t=t, topk=topk),
        grid=(batch, nt),
        in_specs=[pl.BlockSpec(slope_tab.shape, lambda b, i: (0, 0)),
                  pl.BlockSpec(pos_tab.shape, lambda b, i: (0, 0, 0)),
                  pl.BlockSpec((1, t, IDX_HEADS * IDX_DIM), lambda b, i: (b, i, qi0)),
                  pl.BlockSpec((1, seq, LANES), lambda b, i: (b, 0, ki0)),
                  pl.BlockSpec((1, t, LANES), lambda b, i: (b, i, 0)),
                  pl.BlockSpec((1, t, qw), lambda b, i: (b, i, q0)),
                  pl.BlockSpec((1, seq, kvw), lambda b, i: (b, 0, k0 * LANES // kvw)),
                  pl.BlockSpec((1, seq, kvw), lambda b, i: (b, 0, v0 * LANES // kvw))],
        out_specs=pl.BlockSpec((1, t, qw), lambda b, i: (b, i, 0)),
        out_shape=jax.ShapeDtypeStruct((batch, seq, qw), BF16),
        scratch_shapes=[pltpu.VMEM((nt, t, t), jnp.int32), pltpu.VMEM((nt, t, t), F32),
                        pltpu.VMEM((nt, t, t), BF16), pltpu.VMEM((2, t, LANES), F32)],
        compiler_params=_params("parallel", "arbitrary"),
        name="dsa_attention",
    )(slope_tab, pos_tab, proj, proj, wi, proj, proj, proj)


def _sb_kernel(q_ref, k_ref, v_ref, o_ref, later_ref, acc_ref, *, tq, tk):
    r = tq // tk
    lo = _lane_lo()
    rel = _tile_rel(tq, tk)
    blk = min(tk, 2 * LANES)
    suffix = jnp.where(_tile_rel(blk, blk) >= 0, 1.0, 0.0).astype(BF16)
    cols = [slice(c0, c0 + LANES) for c0 in range(0, tk, LANES)]
    hs = range(2)
    for i in range(q_ref.shape[1] // tq):
        rows = slice(i * tq, (i + 1) * tq)
        qp = q_ref[0, rows, :]
        qs = [jnp.where(lo if hh == 0 else jnp.logical_not(lo), qp, jnp.zeros_like(qp)) for hh in hs]

        def tile(j, diag, qs=qs):
            start = pl.multiple_of(j * tk, tk)
            k = k_ref[0, pl.ds(start, tk), :]
            v = v_ref[0, pl.ds(start, tk), :]
            zs = [_dot_t(qs[hh], k) for hh in hs]
            lgs = []
            for z in zs:
                nz = -z
                lgs.append(jnp.minimum(nz, 0.0) - jnp.log2(1.0 + jnp.exp2(jnp.minimum(z, nz))))
            if diag is not None:
                strict = rel > diag * tk
                lgs = [jnp.where(strict, lg, 0.0) for lg in lgs]
            lbs = [lg.astype(BF16) for lg in lgs]
            runs, tots = [], []
            for hh in hs:
                parts, after = [], later_ref[hh]
                for c0 in reversed(range(0, tk, blk)):
                    part = _dot(lbs[hh][:, c0:c0 + blk], suffix) + jnp.concatenate([after] * (blk // LANES), axis=1)
                    after = jnp.broadcast_to(part[:, 0:1], (tq, LANES))
                    parts.insert(0, part)
                runs.append(jnp.concatenate(parts, axis=1))
                tots.append(after)
            a_s = [jnp.exp2(zs[hh] + runs[hh]) for hh in hs]
            if diag is not None:
                a_s = [jnp.where(strict, a, 0.0) for a in a_s]
            for hh in hs:
                later_ref[hh] = tots[hh]
            for hh in hs:
                acc_ref[hh] += _dot(a_s[hh].astype(BF16), v)

        later_ref[...] = jnp.zeros(later_ref.shape, F32)
        acc_ref[...] = jnp.zeros(acc_ref.shape, F32)
        for d in reversed(range(r)):
            tile(i * r + d, d)
        if i:
            pl.loop(0, i * r)(lambda jj, tile=tile, i=i: tile(i * r - 1 - jj, None))
        o_ref[0, rows, :] = jnp.where(lo, acc_ref[0], acc_ref[1]).astype(BF16)


def _sb_attention(qkv, batch, seq, tq=512, tk=512):
    tq, tk = min(tq, seq), min(tk, seq)
    w = SB_HEADS * SB_HEAD_DIM
    pairs = SB_HEADS // 2
    return pl.pallas_call(
        functools.partial(_sb_kernel, tq=tq, tk=tk),
        grid=(batch, pairs),
        in_specs=[pl.BlockSpec((1, seq, LANES), lambda b, m: (b, 0, m)),
                  pl.BlockSpec((1, seq, LANES), lambda b, m: (b, 0, pairs + m)),
                  pl.BlockSpec((1, seq, LANES), lambda b, m: (b, 0, 2 * pairs + m))],
        out_specs=pl.BlockSpec((1, seq, LANES), lambda b, m: (b, 0, m)),
        out_shape=jax.ShapeDtypeStruct((batch, seq, w), BF16),
        scratch_shapes=[pltpu.VMEM((2, tq, LANES), F32), pltpu.VMEM((2, tq, LANES), F32)],
        compiler_params=_params("parallel", "parallel"),
        name="sb_attention",
    )(qkv, qkv, qkv)


def _dup_heads(w, n_heads, dim):
    d = w.shape[0]
    w = w.reshape(d, n_heads, 1, dim)
    return jnp.broadcast_to(w, (d, n_heads, 2, dim)).reshape(d, n_heads * 2 * dim)


def _pad_heads(w, n_heads, dim, to):
    d = w.shape[0]
    w = w.reshape(d, n_heads, dim)
    return jnp.pad(w, ((0, 0), (0, 0), (0, to - dim))).reshape(d, n_heads * to)


def _swa_w_in(w):
    qw = SWA_HEADS * SWA_HEAD_DIM
    kw = SWA_KV_HEADS * SWA_HEAD_DIM
    q, k, v = w[:, :qw], w[:, qw:qw + kw], w[:, qw + kw:]
    q = q * (LOG2E * SWA_HEAD_DIM ** -0.5)
    return jnp.concatenate([q, _dup_heads(k, SWA_KV_HEADS, SWA_HEAD_DIM),
                            _dup_heads(v, SWA_KV_HEADS, SWA_HEAD_DIM)], axis=1).astype(BF16)


def _mla_w_in(w):
    d = w.shape[0]
    lat = MLA_Q_LORA + MLA_KV_LORA
    z = lambda c: jnp.zeros((d, c), w.dtype)
    return jnp.concatenate([w[:, :lat], z(MLA_NOPE), w[:, lat:], z(LANES - MLA_NOPE - MLA_ROPE)],
                           axis=1).astype(BF16)


def _sb_w_in(w):
    qw = SB_HEADS * SB_HEAD_DIM
    return jnp.concatenate([w[:, :qw] * (LOG2E * SB_HEAD_DIM ** -0.5), w[:, qw:]], axis=1).astype(BF16)


def _dsa_w_in(w):
    qw = DSA_HEADS * DSA_HEAD_DIM
    kw = DSA_KV_HEADS * DSA_HEAD_DIM
    iw = IDX_HEADS * IDX_DIM
    c = np.cumsum([qw, kw, kw, iw, IDX_DIM]).tolist()
    q, k, v, qi, ki, wi = (w[:, :c[0]], w[:, c[0]:c[1]], w[:, c[1]:c[2]], w[:, c[2]:c[3]],
                           w[:, c[3]:c[4]], w[:, c[4]:])
    q = q * (LOG2E * DSA_HEAD_DIM ** -0.5)
    qi = qi * (IDX_DIM ** -0.5)
    main = jnp.concatenate([q, _dup_heads(k, DSA_KV_HEADS, DSA_HEAD_DIM),
                            _dup_heads(v, DSA_KV_HEADS, DSA_HEAD_DIM), qi, _dup_heads(ki, 1, IDX_DIM)],
                           axis=1).astype(BF16)
    wi = jnp.pad(wi, ((0, 0), (0, LANES - IDX_HEADS))).astype(BF16)
    return main, wi


def kernel(x, ffn1_w_gu, ffn1_w_down, ffn2_w_gu, ffn2_w_down, ln_g, ln_b, a_w_in, a_sinks, a_w_o,
           b_w_in, b_q_norm, b_w_uq, b_kv_norm, b_w_ukv, b_w_o, c_w_in, c_w_o, d_w_in, d_w_o):
    batch, seq, d = x.shape
    n = batch * seq
    x = x.reshape(n, d)
    for i in range(DEPTH):
        mixer, j = i % N_MIXERS, i // N_MIXERS

        def ffn1(projs):
            return _ffn_proj(x, ffn1_w_gu[i].astype(BF16), ffn1_w_down[i].astype(BF16), ln_g[i, 0],
                             ln_b[i, 0], projs)

        if mixer == 0:
            x, qkv = ffn1([(_swa_w_in(a_w_in[j]), BF16)])
            o = _swa_attention(qkv.reshape(batch, seq, -1), a_sinks[j] * LOG2E, batch, seq)
            w_o = a_w_o[j]
        elif mixer == 1:
            x, c = ffn1([(_mla_w_in(b_w_in[j]), F32)])
            hd = MLA_NOPE + MLA_ROPE
            wq = _pad_heads(b_w_uq[j], MLA_HEADS, hd, LANES).astype(BF16)
            wkv = b_w_ukv[j].reshape(MLA_KV_LORA, MLA_HEADS, MLA_NOPE + MLA_V)
            wk = _pad_heads(wkv[:, :, :MLA_NOPE].reshape(MLA_KV_LORA, -1), MLA_HEADS, MLA_NOPE,
                            LANES).astype(BF16)
            wv = wkv[:, :, MLA_NOPE:].reshape(MLA_KV_LORA, -1).astype(BF16)
            q, k, v = _mla_up(c, b_q_norm[j], b_kv_norm[j], wq, wk, wv, seq)
            o = _mla_attention(q.reshape(batch, seq, -1), k.reshape(batch, seq, -1),
                               v.reshape(batch, seq, -1))
            w_o = b_w_o[j]
        elif mixer == 2:
            w_main, w_wi = _dsa_w_in(c_w_in[j])
            x, proj, wi = ffn1([(w_main, BF16), (w_wi, F32)])
            o = _dsa_attention(proj.reshape(batch, seq, -1), wi.reshape(batch, seq, -1), batch, seq)
            w_o = c_w_o[j]
        else:
            x, qkv = ffn1([(_sb_w_in(d_w_in[j]), BF16)])
            o = _sb_attention(qkv.reshape(batch, seq, -1), batch, seq)
            w_o = d_w_o[j]
        x = _proj_ffn(o.reshape(n, -1), w_o.astype(BF16), x, ln_g[i, 1], ln_b[i, 1],
                      ffn2_w_gu[i].astype(BF16), ffn2_w_down[i].astype(BF16), ln_g[i, 2], ln_b[i, 2])
    return x.reshape(batch, seq, d)
```

```python
import functools
import math

import numpy as np
import jax
import jax.numpy as jnp
from jax import lax
from jax.experimental import pallas as pl
from jax.experimental.pallas import tpu as pltpu

D_MODEL = 1024
DEPTH = 4
N_MIXERS = 4
D_FF = 2816
LN_EPS = 1e-5
RMS_EPS = 1e-6
NEG = -1e30

SWA_HEADS, SWA_KV_HEADS, SWA_HEAD_DIM, SWA_BLOCK = 16, 4, 64, 128
MLA_HEADS, MLA_Q_LORA, MLA_KV_LORA, MLA_NOPE, MLA_ROPE, MLA_V = 16, 768, 256, 64, 32, 64
ROPE_THETA = 10000.0
DSA_HEADS, DSA_KV_HEADS, DSA_HEAD_DIM = 16, 4, 64
IDX_HEADS, IDX_DIM, IDX_TOPK = 8, 64, 256
SB_HEADS, SB_HEAD_DIM = 16, 64

ALPHA = (2.0 * DEPTH) ** 0.25
LOG2E = math.log2(math.e)

LANES = 128
HALF = 64
VMEM_LIMIT = 56 * 1024 * 1024
BF16 = jnp.bfloat16
F32 = jnp.float32


def _params(*sem, flags=None):
    return pltpu.CompilerParams(dimension_semantics=sem, vmem_limit_bytes=VMEM_LIMIT, flags=flags)


def _dot(a, b):
    return jnp.dot(a, b, preferred_element_type=F32)


def _dot_t(a, b):
    return lax.dot_general(a, b, (((1,), (1,)), ((), ())), preferred_element_type=F32)


def _alibi_slopes(n_heads):
    return [2.0 ** (-8.0 * (i + 1) / n_heads) for i in range(n_heads)]


def _layer_norm(y, g, b):
    mu = jnp.mean(y, axis=-1, keepdims=True)
    yc = y - mu
    var = jnp.mean(yc * yc, axis=-1, keepdims=True)
    return yc * lax.rsqrt(var + LN_EPS) * g + b


def _lane_lo():
    return lax.broadcasted_iota(jnp.int32, (1, LANES), 1) < HALF


def _resident(shape):
    return pl.BlockSpec(shape, lambda i: (0,) * len(shape), pipeline_mode=pl.Buffered(1))


def _layer(stack, layer):
    return pl.BlockSpec((None,) + stack.shape[1:], lambda i: (layer, 0, 0), pipeline_mode=pl.Buffered(1))


def _rows(tm, width):
    return pl.BlockSpec((tm, width), lambda i: (i, 0))


def _ffn_ln_value(x, wgu_ref, wd_ref, g_ref, b_ref):
    f = wd_ref.shape[0]
    xb = x.astype(BF16)
    h = _dot(xb, wgu_ref[:, :f])
    u = _dot(xb, wgu_ref[:, f:])
    a = h * (1.0 / (1.0 + jnp.exp(-h))) * u
    y = ALPHA * x + 0.5 * _dot(a.astype(BF16), wd_ref[...])
    return _layer_norm(y, g_ref[...], b_ref[...])


def _ffn_proj_kernel(x_ref, wgu_ref, wd_ref, g_ref, b_ref, *refs, n_proj):
    w_refs, o_ref, p_refs = refs[:n_proj], refs[n_proj], refs[n_proj + 1:]
    y = _ffn_ln_value(x_ref[...], wgu_ref, wd_ref, g_ref, b_ref)
    o_ref[...] = y
    yb = y.astype(BF16)
    for w_ref, p_ref in zip(w_refs, p_refs):
        p_ref[...] = _dot(yb, w_ref[...]).astype(p_ref.dtype)


def _ffn_proj(x, w_gu, w_down, layer, g, b, projs, tm=512):
    n, d = x.shape
    tm = min(tm, n)
    ws = [w for w, _ in projs]
    return pl.pallas_call(
        functools.partial(_ffn_proj_kernel, n_proj=len(projs)),
        grid=(n // tm,),
        in_specs=[_rows(tm, d), _layer(w_gu, layer), _layer(w_down, layer), _resident((1, d)),
                  _resident((1, d))] + [_resident(w.shape) for w in ws],
        out_specs=[_rows(tm, d)] + [_rows(tm, w.shape[1]) for w in ws],
        out_shape=[jax.ShapeDtypeStruct((n, d), F32)]
        + [jax.ShapeDtypeStruct((n, w.shape[1]), dt) for w, dt in projs],
        compiler_params=_params("parallel"),
        name="ffn_in_proj",
    )(x, w_gu, w_down, g.reshape(1, d), b.reshape(1, d), *ws)


def _proj_ffn_kernel(o_ref, wo_ref, x_ref, g1_ref, b1_ref, wgu_ref, wd_ref, g2_ref, b2_ref, out_ref):
    y = ALPHA * x_ref[...] + _dot(o_ref[...], wo_ref[...])
    x2 = _layer_norm(y, g1_ref[...], b1_ref[...])
    out_ref[...] = _ffn_ln_value(x2, wgu_ref, wd_ref, g2_ref, b2_ref)


def _proj_ffn(o, w_o, x, g1, b1, w_gu, w_down, layer, g2, b2, tm=512):
    n, k = o.shape
    d = w_o.shape[1]
    tm = min(tm, n)
    vec = lambda a: a.reshape(1, d)
    return pl.pallas_call(
        _proj_ffn_kernel,
        grid=(n // tm,),
        in_specs=[_rows(tm, k), _resident(w_o.shape), _rows(tm, d), _resident((1, d)), _resident((1, d)),
                  _layer(w_gu, layer), _layer(w_down, layer), _resident((1, d)), _resident((1, d))],
        out_specs=_rows(tm, d),
        out_shape=jax.ShapeDtypeStruct((n, d), F32),
        compiler_params=_params("parallel"),
        name="out_proj_ffn",
    )(o, w_o, x, vec(g1), vec(b1), w_gu, w_down, vec(g2), vec(b2))


def _swa_kernel(sink_ref, q_ref, kc_ref, kp_ref, vc_ref, vp_ref, o_ref, *, tq):
    i = pl.program_id(1)
    t = SWA_BLOCK
    rep = SWA_HEADS // SWA_KV_HEADS
    kvs = range(SWA_KV_HEADS)
    lo = _lane_lo()
    rel = _tile_rel(t, 2 * t) + t
    col = lax.broadcasted_iota(jnp.int32, (t, 2 * t), 1)
    dist = rel.astype(F32)
    in_band = jnp.logical_and(rel >= 0, rel < SWA_BLOCK)
    slopes = [s * LOG2E for s in _alibi_slopes(SWA_HEADS)]
    lane = lambda g: slice(g * LANES, (g + 1) * LANES)
    for qb in range(tq // t):
        r0 = qb * t
        if qb == 0:
            ks = [jnp.concatenate([kp_ref[0, :, lane(g)], kc_ref[0, 0:t, lane(g)]], axis=0) for g in kvs]
            vs = [jnp.concatenate([vp_ref[0, :, lane(g)], vc_ref[0, 0:t, lane(g)]], axis=0) for g in kvs]
            valid = jnp.logical_and(in_band, col >= jnp.where(i == 0, t, 0))
        else:
            ks = [kc_ref[0, r0 - t:r0 + t, lane(g)] for g in kvs]
            vs = [vc_ref[0, r0 - t:r0 + t, lane(g)] for g in kvs]
            valid = in_band
        qst = []
        for g in kvs:
            parts = []
            for r in range(rep):
                h = g * rep + r
                qp = q_ref[0, r0:r0 + t, lane(h // 2)]
                parts.append(jnp.where(lo if h % 2 == 0 else jnp.logical_not(lo), qp, jnp.zeros_like(qp)))
            qst.append(jnp.concatenate(parts, axis=0))
        ss = [_dot_t(qst[g], ks[g]) for g in kvs]
        ps, dens = [], []
        for g in kvs:
            prow, drow = [], []
            for r in range(rep):
                h = g * rep + r
                s = jnp.where(valid, ss[g][r * t:(r + 1) * t] - slopes[h] * dist, NEG)
                sink = sink_ref[h]
                mx = jnp.maximum(jnp.max(s, axis=-1, keepdims=True), sink)
                p = jnp.exp2(s - mx)
                drow.append(jnp.sum(p, axis=-1, keepdims=True) + jnp.exp2(sink - mx))
                prow.append(p.astype(BF16))
            ps.append(jnp.concatenate(prow, axis=0))
            dens.append(drow)
        outs = [_dot(ps[g], vs[g]) for g in kvs]
        for g in kvs:
            for r in range(0, rep, 2):
                even = outs[g][r * t:(r + 1) * t] / dens[g][r]
                odd = outs[g][(r + 1) * t:(r + 2) * t] / dens[g][r + 1]
                o_ref[0, r0:r0 + t, lane((g * rep + r) // 2)] = jnp.where(lo, even, odd).astype(BF16)


def _swa_attention(qkv, sinks, batch, seq, tq=512):
    t = SWA_BLOCK
    tq = min(tq, seq)
    sub = tq // t
    qw = SWA_HEADS * SWA_HEAD_DIM
    kw = SWA_KV_HEADS * LANES
    qblk, kblk, vblk = 0, qw // kw, qw // kw + 1
    prev = lambda i: jnp.maximum(i * sub - 1, 0)
    return pl.pallas_call(
        functools.partial(_swa_kernel, tq=tq),
        grid=(batch, seq // tq),
        in_specs=[pl.BlockSpec(memory_space=pltpu.SMEM),
                  pl.BlockSpec((1, tq, qw), lambda b, i: (b, i, qblk)),
                  pl.BlockSpec((1, tq, kw), lambda b, i: (b, i, kblk)),
                  pl.BlockSpec((1, t, kw), lambda b, i: (b, prev(i), kblk)),
                  pl.BlockSpec((1, tq, kw), lambda b, i: (b, i, vblk)),
                  pl.BlockSpec((1, t, kw), lambda b, i: (b, prev(i), vblk))],
        out_specs=pl.BlockSpec((1, tq, qw), lambda b, i: (b, i, 0)),
        out_shape=jax.ShapeDtypeStruct((batch, seq, qw), BF16),
        compiler_params=_params("parallel", "arbitrary"),
        name="swa_attention",
    )(sinks, qkv, qkv, qkv, qkv, qkv)


def _mla_up_kernel(c_ref, gq_ref, gkv_ref, wq_ref, wk_ref, wv_ref, cq_ref, s1q_ref, s2q_ref,
                   ck_ref, s1k_ref, s2k_ref, q_ref, k_ref, v_ref):
    c = c_ref[...]
    cq = c[:, :MLA_Q_LORA]
    ckv = c[:, MLA_Q_LORA:MLA_Q_LORA + MLA_KV_LORA]
    kr = c[:, MLA_Q_LORA + MLA_KV_LORA:]
    cq = cq * lax.rsqrt(jnp.mean(cq * cq, axis=-1, keepdims=True) + RMS_EPS) * gq_ref[...]
    ckv = ckv * lax.rsqrt(jnp.mean(ckv * ckv, axis=-1, keepdims=True) + RMS_EPS) * gkv_ref[...]
    ckv_b = ckv.astype(BF16)
    q = _dot(cq.astype(BF16), wq_ref[...])
    kn = _dot(ckv_b, wk_ref[...])
    v_ref[...] = _dot(ckv_b, wv_ref[...]).astype(BF16)

    def rot(xh, cc, s1, s2):
        return xh * cc + pltpu.roll(xh, 16, 1) * s1 + pltpu.roll(xh, LANES - 16, 1) * s2

    kr = rot(kr, ck_ref[...], s1k_ref[...], s2k_ref[...])
    cq_t, s1q, s2q = cq_ref[...], s1q_ref[...], s2q_ref[...]
    for h in range(MLA_HEADS):
        sl = slice(h * LANES, (h + 1) * LANES)
        q_ref[:, sl] = rot(q[:, sl], cq_t, s1q, s2q).astype(BF16)
        k_ref[:, sl] = (kn[:, sl] + kr).astype(BF16)


def _mla_rope_tables(seq, scale):
    half = MLA_ROPE // 2
    inv = ROPE_THETA ** (-jnp.arange(0, MLA_ROPE, 2, dtype=F32) / MLA_ROPE)
    ang = jnp.arange(seq, dtype=F32)[:, None] * inv[None, :]
    cos, sin = jnp.cos(ang), jnp.sin(ang)
    z = lambda w: jnp.zeros((seq, w), F32)
    one = jnp.ones((seq, MLA_NOPE), F32)
    cc = jnp.concatenate([one, cos, cos, z(LANES - MLA_NOPE - MLA_ROPE)], axis=1) * scale
    s1 = jnp.concatenate([z(MLA_NOPE + half), sin, z(LANES - MLA_NOPE - MLA_ROPE)], axis=1) * scale
    s2 = jnp.concatenate([z(MLA_NOPE), -sin, z(LANES - MLA_NOPE - half)], axis=1) * scale
    return cc, s1, s2


def _mla_up(c, gq, gkv, wq, wk, wv, seq, tm=512):
    n, cw = c.shape
    tm = min(tm, seq)
    spt = seq // tm
    scale = LOG2E * (MLA_NOPE + MLA_ROPE) ** -0.5
    tq = _mla_rope_tables(seq, scale)
    tk = _mla_rope_tables(seq, 1.0)
    qk_w = MLA_HEADS * LANES
    v_w = MLA_HEADS * MLA_V
    full = lambda a: _resident(a.shape)
    tab = pl.BlockSpec((tm, LANES), lambda i: (i % spt, 0))
    gq2, gkv2 = gq.reshape(1, -1), gkv.reshape(1, -1)
    return pl.pallas_call(
        _mla_up_kernel,
        grid=(n // tm,),
        in_specs=[pl.BlockSpec((tm, cw), lambda i: (i, 0)), full(gq2), full(gkv2), full(wq), full(wk),
                  full(wv), tab, tab, tab, tab, tab, tab],
        out_specs=[pl.BlockSpec((tm, qk_w), lambda i: (i, 0)),
                   pl.BlockSpec((tm, qk_w), lambda i: (i, 0)),
                   pl.BlockSpec((tm, v_w), lambda i: (i, 0))],
        out_shape=[jax.ShapeDtypeStruct((n, qk_w), BF16), jax.ShapeDtypeStruct((n, qk_w), BF16),
                   jax.ShapeDtypeStruct((n, v_w), BF16)],
        compiler_params=_params("parallel"),
        name="mla_up_proj",
    )(c, gq2, gkv2, wq, wk, wv, *tq, *tk)


SUM_LANE = (HALF, 0)


def _with_sum_column(v):
    lane = lax.broadcasted_iota(jnp.int32, (1, LANES), 1)
    unit = lambda at: jnp.where(lane == at, 1.0, 0.0).astype(v.dtype)
    low = lane < HALF
    return [jnp.where(low, v, unit(SUM_LANE[0])), jnp.where(low, unit(SUM_LANE[1]), v)]


def _flash_finish(acc_ref):
    outs = [acc_ref[hh] / acc_ref[hh][:, SUM_LANE[hh]:SUM_LANE[hh] + 1] for hh in range(2)]
    return jnp.where(_lane_lo(), outs[0], outs[1])


def _row_max(s):
    mx = s[:, :LANES]
    for c0 in range(LANES, s.shape[1], LANES):
        mx = jnp.maximum(mx, s[:, c0:c0 + LANES])
    return jnp.max(mx, axis=-1, keepdims=True)


def _flash_init(tq):
    m = jnp.full((tq, LANES), NEG, F32)
    return (m, m)


def _flash_pair(ss, v, m_old, acc_ref):
    hs = range(2)
    vs = _with_sum_column(v)
    cols = [slice(c0, c0 + LANES) for c0 in range(0, ss[0].shape[1], LANES)]
    ms = [jnp.maximum(m_old[hh], _row_max(ss[hh])) for hh in hs]
    ps = [jnp.concatenate([jnp.exp2(ss[hh][:, c] - ms[hh]) for c in cols], axis=1).astype(BF16) for hh in hs]
    al = [jnp.exp2(m_old[hh] - ms[hh]) for hh in hs]
    for hh in hs:
        acc_ref[hh] = al[hh] * acc_ref[hh] + _dot(ps[hh], vs[hh])
    return tuple(ms)


SOFTMAX_ROWS = 32


def _flash_scratch(tq, tk):
    return [pltpu.VMEM((2, tq, tk), F32), pltpu.VMEM((2, tq, tk), BF16), pltpu.VMEM((2, tq, LANES), F32),
            pltpu.VMEM((2, tq, LANES), F32)]


def _flash_reset(m_ref, acc_ref):
    m_ref[...] = jnp.full(m_ref.shape, NEG, F32)
    acc_ref[...] = jnp.zeros(acc_ref.shape, F32)


def _flash_update(s_ref, p_ref, m_ref, acc_ref, v):
    _, tq, tk = s_ref.shape
    vs = _with_sum_column(v)
    for c in range(tq // SOFTMAX_ROWS):
        rows = slice(c * SOFTMAX_ROWS, (c + 1) * SOFTMAX_ROWS)
        for hh in range(2):
            s = s_ref[hh, rows, :]
            m_old = m_ref[hh, rows, :]
            m_new = jnp.maximum(m_old, _row_max(s))
            m_ref[hh, rows, :] = m_new
            acc_ref[hh, rows, :] = jnp.exp2(m_old - m_new) * acc_ref[hh, rows, :]
            for c0 in range(0, tk, LANES):
                p_ref[hh, rows, c0:c0 + LANES] = jnp.exp2(s[:, c0:c0 + LANES] - m_new).astype(BF16)
    for hh in range(2):
        acc_ref[hh] += _dot(p_ref[hh], vs[hh])


def _tile_rel(tq, tk):
    return (lax.broadcasted_iota(jnp.int32, (tq, tk), 0) - lax.broadcasted_iota(jnp.int32, (tq, tk), 1))


def _mla_attn_kernel(q_ref, k_ref, v_ref, o_ref, s_ref, p_ref, m_ref, acc_ref, *, tq, tk):
    r = tq // tk
    rel = _tile_rel(tq, tk)
    sls = [slice(hh * LANES, (hh + 1) * LANES) for hh in range(2)]
    for i in range(q_ref.shape[1] // tq):
        rows = slice(i * tq, (i + 1) * tq)

        def tile(j, diag, rows=rows):
            start = pl.multiple_of(j * tk, tk)
            for hh in range(2):
                s = _dot_t(q_ref[0, rows, sls[hh]], k_ref[0, pl.ds(start, tk), sls[hh]])
                if diag is not None:
                    s = jnp.where(rel >= diag * tk, s, NEG)
                s_ref[hh] = s
            _flash_update(s_ref, p_ref, m_ref, acc_ref, v_ref[0, pl.ds(start, tk), :])

        _flash_reset(m_ref, acc_ref)
        for d in range(r):
            tile(i * r + d, d)
        if i:
            pl.loop(0, i * r)(lambda j, tile=tile: tile(j, None))
        o_ref[0, rows, :] = _flash_finish(acc_ref).astype(BF16)


def _mla_attention(q, k, v, tq=512, tk=512):
    batch, seq, _ = q.shape
    tq, tk = min(tq, seq), min(tk, seq)
    return pl.pallas_call(
        functools.partial(_mla_attn_kernel, tq=tq, tk=tk),
        grid=(batch, MLA_HEADS // 2),
        in_specs=[pl.BlockSpec((1, seq, 2 * LANES), lambda b, m: (b, 0, m)),
                  pl.BlockSpec((1, seq, 2 * LANES), lambda b, m: (b, 0, m)),
                  pl.BlockSpec((1, seq, LANES), lambda b, m: (b, 0, m))],
        out_specs=pl.BlockSpec((1, seq, LANES), lambda b, m: (b, 0, m)),
        out_shape=jax.ShapeDtypeStruct(v.shape, BF16),
        scratch_shapes=_flash_scratch(tq, tk),
        compiler_params=_params("parallel", "parallel"),
        name="mla_attention",
    )(q, k, v)


def _order_key(x):
    bits = lax.bitcast_convert_type(x, jnp.int32)
    return bits ^ (lax.shift_right_arithmetic(bits, 31) & 0x7FFFFFFF)


def _dsa_kernel(slope_ref, pos_ref, qi_ref, ki_ref, wi_ref, q_ref, k_ref, v_ref, o_ref, key_ref, bias_ref,
                byte_ref, acc_ref, *, t, topk):
    i = pl.program_id(1)
    lo = _lane_lo()
    row = lax.broadcasted_iota(jnp.int32, (t, t), 0)
    col = lax.broadcasted_iota(jnp.int32, (t, t), 1)
    rel = row - col

    def _select():
        wi_t = jnp.transpose(wi_ref[0] * (IDX_HEADS ** -0.5))
        rel_t = col - row
        pack = 16

        def score_tile(j, carry):
            start = pl.multiple_of(j * t, t)
            ki = ki_ref[0, pl.ds(start, t), :]
            sc = jnp.zeros((t, t), F32)
            for p in range(IDX_HEADS // 2):
                qp = qi_ref[0, :, p * LANES:(p + 1) * LANES]
                for hh in range(2):
                    h = 2 * p + hh
                    qm = jnp.where(lo if hh == 0 else jnp.logical_not(lo), qp, jnp.zeros_like(qp))
                    dots = _dot_t(ki, qm)
                    sc = sc + wi_t[h:h + 1, :] * jnp.maximum(dots, 0.0)
            sc = sc + 0.0
            sc = jnp.where(rel_t >= (j - i) * t, sc, NEG)
            key_ref[j] = _order_key(sc)
            return carry

        lax.fori_loop(0, i + 1, score_tile, 0)

        one_b, zero_b = jnp.ones((), BF16), jnp.zeros((), BF16)

        def count_ge(cand):
            cand_b = cand.astype(BF16)

            def body(j, part):
                ge = jnp.where(byte_ref[j] >= cand_b, one_b, zero_b)
                slabs = [ge[r0:r0 + pack] for r0 in range(0, t, pack)]
                while len(slabs) > 1:
                    slabs = [a + b for a, b in zip(slabs[0::2], slabs[1::2])]
                return part + slabs[0]
            part = lax.fori_loop(0, i + 1, body, jnp.zeros((pack, t), BF16))
            return jnp.sum(part.astype(F32), axis=0, keepdims=True)

        prefix = jnp.zeros((1, t), jnp.int32)
        above = jnp.zeros((1, t), F32)
        for stage in range(4):
            shift = 24 - 8 * stage

            def fill(j, carry, stage=stage, shift=shift, prefix=prefix):
                key = key_ref[j]
                if stage == 0:
                    byte = lax.shift_right_arithmetic(key, shift) + 128
                else:
                    match = lax.shift_right_arithmetic(key, shift + 8) == prefix
                    byte = jnp.where(match, lax.shift_right_logical(key, shift) & 255, -1)
                byte_ref[j] = byte.astype(F32).astype(BF16)
                return carry

            lax.fori_loop(0, i + 1, fill, 0)
            val = jnp.zeros((1, t), F32)
            rej = jnp.zeros((1, t), F32)
            for bit in (128, 64, 32, 16, 8, 4, 2, 1):
                cand = val + float(bit)
                cnt = count_ge(cand)
                ok = above + cnt >= topk
                val = jnp.where(ok, cand, val)
                rej = jnp.where(ok, rej, cnt)
            above = above + rej
            byte_i = val.astype(jnp.int32) - (128 if stage == 0 else 0)
            prefix = byte_i if stage == 0 else lax.shift_left(prefix, 8) | byte_i
        thr = prefix
        need = topk - above
        before = jnp.where(row > col, 1.0, 0.0).astype(BF16)

        def select_tile(j, seen):
            key = key_ref[j]
            eq = key == thr
            eq_b = jnp.where(eq, 1.0, 0.0).astype(BF16)
            rank = seen + _dot(before, eq_b)
            sel = jnp.logical_or(key > thr, jnp.logical_and(eq, rank < need))
            sel = jnp.logical_and(sel, rel_t >= (j - i) * t)
            bias_ref[j] = jnp.transpose(jnp.where(sel, 0.0, NEG))
            return seen + jnp.sum(eq_b.astype(F32), axis=0, keepdims=True)

        lax.fori_loop(0, i + 1, select_tile, jnp.zeros((1, t), F32))

    _select()
    pairs = DSA_HEADS // 2
    for m in range(pairs):
        lanes = slice(m * LANES, (m + 1) * LANES)
        g = m // (pairs // DSA_KV_HEADS)
        kv_lanes = slice(g * LANES, (g + 1) * LANES)
        qp = q_ref[0, :, lanes]
        qs = [jnp.where(lo, qp, slope_ref[2 * m:2 * m + 1, :]),
              jnp.where(lo, slope_ref[2 * m + 1:2 * m + 2, :], qp)]

        def tile(j, carry, qs=qs, kv_lanes=kv_lanes):
            start = pl.multiple_of(j * t, t)
            k = k_ref[0, pl.ds(start, t), kv_lanes]
            v = v_ref[0, pl.ds(start, t), kv_lanes]
            ks = [jnp.where(lo, k, pos_ref[0, pl.ds(start, t), :]), jnp.where(lo, pos_ref[1, pl.ds(start, t), :], k)]
            bias = bias_ref[j]
            ss = [_dot_t(qs[hh], ks[hh]) + bias for hh in range(2)]
            return _flash_pair(ss, v, carry, acc_ref)

        acc_ref[...] = jnp.zeros(acc_ref.shape, F32)
        carry = tile(i, _flash_init(t))
        lax.fori_loop(0, i, tile, carry)
        o_ref[0, :, lanes] = _flash_finish(acc_ref).astype(BF16)


ALIBI_PIECES = 6


def _dsa_alibi_tables(seq):
    bf = lambda a: np.asarray(np.asarray(a, np.float32).astype(jnp.bfloat16), np.float32)
    slopes = np.asarray(_alibi_slopes(DSA_HEADS), np.float64) * LOG2E
    slopes = slopes.astype(np.float32)
    s1 = bf(slopes)
    s2 = bf(slopes - s1)
    s3 = bf(slopes - s1 - s2)
    slope_tab = np.zeros((DSA_HEADS, LANES), np.float32)
    for h in range(DSA_HEADS):
        off = HALF if h % 2 == 0 else 0
        slope_tab[h, off:off + ALIBI_PIECES] = [s1[h], s2[h], s3[h], s1[h], s2[h], s3[h]]
    pos = np.arange(seq)
    hi, lw = (pos // 256 * 256).astype(np.float32), (pos % 256).astype(np.float32)
    pieces = np.stack([hi, hi, hi, lw, lw, lw], axis=1)
    pos_tab = np.zeros((2, seq, LANES), np.float32)
    pos_tab[0, :, HALF:HALF + ALIBI_PIECES] = pieces
    pos_tab[1, :, :ALIBI_PIECES] = pieces
    return jnp.asarray(slope_tab, BF16), jnp.asarray(pos_tab, BF16)


def _dsa_attention(proj, wi, batch, seq, t=512):
    t = min(t, seq)
    nt = seq // t
    topk = min(IDX_TOPK, seq // 4)
    qw = DSA_HEADS * DSA_HEAD_DIM
    q0 = 0
    k0 = qw // LANES
    v0 = k0 + DSA_KV_HEADS
    qi0 = (v0 + DSA_KV_HEADS) * LANES // (IDX_HEADS * IDX_DIM)
    ki0 = v0 + DSA_KV_HEADS + IDX_HEADS * IDX_DIM // LANES
    kvw = DSA_KV_HEADS * LANES
    slope_tab, pos_tab = _dsa_alibi_tables(seq)
    return pl.pallas_call(
        functools.partial(_dsa_kernel, t=t, topk=topk),
        grid=(batch, nt),
        in_specs=[pl.BlockSpec(slope_tab.shape, lambda b, i: (0, 0)),
                  pl.BlockSpec(pos_tab.shape, lambda b, i: (0, 0, 0)),
                  pl.BlockSpec((1, t, IDX_HEADS * IDX_DIM), lambda b, i: (b, i, qi0)),
                  pl.BlockSpec((1, seq, LANES), lambda b, i: (b, 0, ki0)),
                  pl.BlockSpec((1, t, LANES), lambda b, i: (b, i, 0)),
                  pl.BlockSpec((1, t, qw), lambda b, i: (b, i, q0)),
                  pl.BlockSpec((1, seq, kvw), lambda b, i: (b, 0, k0 * LANES // kvw)),
                  pl.BlockSpec((1, seq, kvw), lambda b, i: (b, 0, v0 * LANES // kvw))],
        out_specs=pl.BlockSpec((1, t, qw), lambda b, i: (b, i, 0)),
        out_shape=jax.ShapeDtypeStruct((batch, seq, qw), BF16),
        scratch_shapes=[pltpu.VMEM((nt, t, t), jnp.int32), pltpu.VMEM((nt, t, t), F32),
                        pltpu.VMEM((nt, t, t), BF16), pltpu.VMEM((2, t, LANES), F32)],
        compiler_params=_params("parallel", "arbitrary"),
        name="dsa_attention",
    )(slope_tab, pos_tab, proj, proj, wi, proj, proj, proj)


def _sb_kernel(q_ref, k_ref, v_ref, o_ref, later_ref, acc_ref, *, tq, tk):
    r = tq // tk
    lo = _lane_lo()
    rel = _tile_rel(tq, tk)
    blk = min(tk, 2 * LANES)
    suffix = jnp.where(_tile_rel(blk, blk) >= 0, 1.0, 0.0).astype(BF16)
    cols = [slice(c0, c0 + LANES) for c0 in range(0, tk, LANES)]
    hs = range(2)
    for i in range(q_ref.shape[1] // tq):
        rows = slice(i * tq, (i + 1) * tq)
        qp = q_ref[0, rows, :]
        qs = [jnp.where(lo if hh == 0 else jnp.logical_not(lo), qp, jnp.zeros_like(qp)) for hh in hs]

        def tile(j, diag, qs=qs):
            start = pl.multiple_of(j * tk, tk)
            k = k_ref[0, pl.ds(start, tk), :]
            v = v_ref[0, pl.ds(start, tk), :]
            zs = [_dot_t(qs[hh], k) for hh in hs]
            lgs = []
            for z in zs:
                nz = -z
                lgs.append(jnp.minimum(nz, 0.0) - jnp.log2(1.0 + jnp.exp2(jnp.minimum(z, nz))))
            if diag is not None:
                strict = rel > diag * tk
                lgs = [jnp.where(strict, lg, 0.0) for lg in lgs]
            lbs = [lg.astype(BF16) for lg in lgs]
            runs, tots = [], []
            for hh in hs:
                parts, after = [], later_ref[hh]
                for c0 in reversed(range(0, tk, blk)):
                    part = _dot(lbs[hh][:, c0:c0 + blk], suffix) + jnp.concatenate([after] * (blk // LANES), axis=1)
                    after = jnp.broadcast_to(part[:, 0:1], (tq, LANES))
                    parts.insert(0, part)
                runs.append(jnp.concatenate(parts, axis=1))
                tots.append(after)
            a_s = [jnp.exp2(zs[hh] + runs[hh]) for hh in hs]
            if diag is not None:
                a_s = [jnp.where(strict, a, 0.0) for a in a_s]
            for hh in hs:
                later_ref[hh] = tots[hh]
            for hh in hs:
                acc_ref[hh] += _dot(a_s[hh].astype(BF16), v)

        later_ref[...] = jnp.zeros(later_ref.shape, F32)
        acc_ref[...] = jnp.zeros(acc_ref.shape, F32)
        for d in reversed(range(r)):
            tile(i * r + d, d)
        if i:
            pl.loop(0, i * r)(lambda jj, tile=tile, i=i: tile(i * r - 1 - jj, None))
        o_ref[0, rows, :] = jnp.where(lo, acc_ref[0], acc_ref[1]).astype(BF16)


def _sb_attention(qkv, batch, seq, tq=512, tk=512):
    tq, tk = min(tq, seq), min(tk, seq)
    w = SB_HEADS * SB_HEAD_DIM
    pairs = SB_HEADS // 2
    return pl.pallas_call(
        functools.partial(_sb_kernel, tq=tq, tk=tk),
        grid=(batch, pairs),
        in_specs=[pl.BlockSpec((1, seq, LANES), lambda b, m: (b, 0, m)),
                  pl.BlockSpec((1, seq, LANES), lambda b, m: (b, 0, pairs + m)),
                  pl.BlockSpec((1, seq, LANES), lambda b, m: (b, 0, 2 * pairs + m))],
        out_specs=pl.BlockSpec((1, seq, LANES), lambda b, m: (b, 0, m)),
        out_shape=jax.ShapeDtypeStruct((batch, seq, w), BF16),
        scratch_shapes=[pltpu.VMEM((2, tq, LANES), F32), pltpu.VMEM((2, tq, LANES), F32)],
        compiler_params=_params("parallel", "parallel"),
        name="sb_attention",
    )(qkv, qkv, qkv)


def _dup_heads(w, n_heads, dim):
    d = w.shape[0]
    w = w.reshape(d, n_heads, 1, dim)
    return jnp.broadcast_to(w, (d, n_heads, 2, dim)).reshape(d, n_heads * 2 * dim)


def _pad_heads(w, n_heads, dim, to):
    d = w.shape[0]
    w = w.reshape(d, n_heads, dim)
    return jnp.pad(w, ((0, 0), (0, 0), (0, to - dim))).reshape(d, n_heads * to)


def _swa_w_in(w):
    qw = SWA_HEADS * SWA_HEAD_DIM
    kw = SWA_KV_HEADS * SWA_HEAD_DIM
    q, k, v = w[:, :qw], w[:, qw:qw + kw], w[:, qw + kw:]
    q = q * (LOG2E * SWA_HEAD_DIM ** -0.5)
    return jnp.concatenate([q, _dup_heads(k, SWA_KV_HEADS, SWA_HEAD_DIM),
                            _dup_heads(v, SWA_KV_HEADS, SWA_HEAD_DIM)], axis=1).astype(BF16)


def _mla_w_in(w):
    d = w.shape[0]
    lat = MLA_Q_LORA + MLA_KV_LORA
    z = lambda c: jnp.zeros((d, c), w.dtype)
    return jnp.concatenate([w[:, :lat], z(MLA_NOPE), w[:, lat:], z(LANES - MLA_NOPE - MLA_ROPE)],
                           axis=1).astype(BF16)


def _sb_w_in(w):
    qw = SB_HEADS * SB_HEAD_DIM
    return jnp.concatenate([w[:, :qw] * (LOG2E * SB_HEAD_DIM ** -0.5), w[:, qw:]], axis=1).astype(BF16)


def _dsa_w_in(w):
    qw = DSA_HEADS * DSA_HEAD_DIM
    kw = DSA_KV_HEADS * DSA_HEAD_DIM
    iw = IDX_HEADS * IDX_DIM
    c = np.cumsum([qw, kw, kw, iw, IDX_DIM]).tolist()
    q, k, v, qi, ki, wi = (w[:, :c[0]], w[:, c[0]:c[1]], w[:, c[1]:c[2]], w[:, c[2]:c[3]],
                           w[:, c[3]:c[4]], w[:, c[4]:])
    q = q * (LOG2E * DSA_HEAD_DIM ** -0.5)
    qi = qi * (IDX_DIM ** -0.5)
    main = jnp.concatenate([q, _dup_heads(k, DSA_KV_HEADS, DSA_HEAD_DIM),
                            _dup_heads(v, DSA_KV_HEADS, DSA_HEAD_DIM), qi, _dup_heads(ki, 1, IDX_DIM)],
                           axis=1).astype(BF16)
    wi = jnp.pad(wi, ((0, 0), (0, LANES - IDX_HEADS))).astype(BF16)
    return main, wi


def kernel(x, ffn1_w_gu, ffn1_w_down, ffn2_w_gu, ffn2_w_down, ln_g, ln_b, a_w_in, a_sinks, a_w_o,
           b_w_in, b_q_norm, b_w_uq, b_kv_norm, b_w_ukv, b_w_o, c_w_in, c_w_o, d_w_in, d_w_o):
    batch, seq, d = x.shape
    n = batch * seq
    x = x.reshape(n, d)
    ffn1_gu, ffn1_down = ffn1_w_gu.astype(BF16), ffn1_w_down.astype(BF16)
    ffn2_gu, ffn2_down = ffn2_w_gu.astype(BF16), ffn2_w_down.astype(BF16)
    for i in range(DEPTH):
        mixer, j = i % N_MIXERS, i // N_MIXERS

        def ffn1(projs):
            return _ffn_proj(x, ffn1_gu, ffn1_down, i, ln_g[i, 0], ln_b[i, 0], projs)

        if mixer == 0:
            x, qkv = ffn1([(_swa_w_in(a_w_in[j]), BF16)])
            o = _swa_attention(qkv.reshape(batch, seq, -1), a_sinks[j] * LOG2E, batch, seq)
            w_o = a_w_o[j]
        elif mixer == 1:
            x, c = ffn1([(_mla_w_in(b_w_in[j]), F32)])
            hd = MLA_NOPE + MLA_ROPE
            wq = _pad_heads(b_w_uq[j], MLA_HEADS, hd, LANES).astype(BF16)
            wkv = b_w_ukv[j].reshape(MLA_KV_LORA, MLA_HEADS, MLA_NOPE + MLA_V)
            wk = _pad_heads(wkv[:, :, :MLA_NOPE].reshape(MLA_KV_LORA, -1), MLA_HEADS, MLA_NOPE,
                            LANES).astype(BF16)
            wv = wkv[:, :, MLA_NOPE:].reshape(MLA_KV_LORA, -1).astype(BF16)
            q, k, v = _mla_up(c, b_q_norm[j], b_kv_norm[j], wq, wk, wv, seq)
            o = _mla_attention(q.reshape(batch, seq, -1), k.reshape(batch, seq, -1),
                               v.reshape(batch, seq, -1))
            w_o = b_w_o[j]
        elif mixer == 2:
            w_main, w_wi = _dsa_w_in(c_w_in[j])
            x, proj, wi = ffn1([(w_main, BF16), (w_wi, F32)])
            o = _dsa_attention(proj.reshape(batch, seq, -1), wi.reshape(batch, seq, -1), batch, seq)
            w_o = c_w_o[j]
        else:
            x, qkv = ffn1([(_sb_w_in(d_w_in[j]), BF16)])
            o = _sb_attention(qkv.reshape(batch, seq, -1), batch, seq)
            w_o = d_w_o[j]
        x = _proj_ffn(o.reshape(n, -1), w_o.astype(BF16), x, ln_g[i, 1], ln_b[i, 1], ffn2_gu, ffn2_down, i,
                      ln_g[i, 2], ln_b[i, 2])
    return x.reshape(batch, seq, d)
```

```python
import functools
import math

import numpy as np
import jax
import jax.numpy as jnp
from jax import lax
from jax.experimental import pallas as pl
from jax.experimental.pallas import tpu as pltpu

D_MODEL = 1024
DEPTH = 4
N_MIXERS = 4
D_FF = 2816
LN_EPS = 1e-5
RMS_EPS = 1e-6
NEG = -1e30

SWA_HEADS, SWA_KV_HEADS, SWA_HEAD_DIM, SWA_BLOCK = 16, 4, 64, 128
MLA_HEADS, MLA_Q_LORA, MLA_KV_LORA, MLA_NOPE, MLA_ROPE, MLA_V = 16, 768, 256, 64, 32, 64
ROPE_THETA = 10000.0
DSA_HEADS, DSA_KV_HEADS, DSA_HEAD_DIM = 16, 4, 64
IDX_HEADS, IDX_DIM, IDX_TOPK = 8, 64, 256
SB_HEADS, SB_HEAD_DIM = 16, 64

ALPHA = (2.0 * DEPTH) ** 0.25
LOG2E = math.log2(math.e)

LANES = 128
HALF = 64
VMEM_LIMIT = 56 * 1024 * 1024
BF16 = jnp.bfloat16
F32 = jnp.float32


def _params(*sem, flags=None):
    return pltpu.CompilerParams(dimension_semantics=sem, vmem_limit_bytes=VMEM_LIMIT, flags=flags)


def _dot(a, b):
    return jnp.dot(a, b, preferred_element_type=F32)


def _dot_t(a, b):
    return lax.dot_general(a, b, (((1,), (1,)), ((), ())), preferred_element_type=F32)


def _alibi_slopes(n_heads):
    return [2.0 ** (-8.0 * (i + 1) / n_heads) for i in range(n_heads)]


def _layer_norm(y, g, b):
    mu = jnp.mean(y, axis=-1, keepdims=True)
    yc = y - mu
    var = jnp.mean(yc * yc, axis=-1, keepdims=True)
    return yc * lax.rsqrt(var + LN_EPS) * g + b


def _lane_lo():
    return lax.broadcasted_iota(jnp.int32, (1, LANES), 1) < HALF


def _resident(shape):
    return pl.BlockSpec(shape, lambda i: (0,) * len(shape), pipeline_mode=pl.Buffered(1))


def _layer(stack, layer):
    return pl.BlockSpec((None,) + stack.shape[1:], lambda i: (layer, 0, 0), pipeline_mode=pl.Buffered(1))


def _rows(tm, width):
    return pl.BlockSpec((tm, width), lambda i: (i, 0))


def _ffn_ln_value(x, wgu_ref, wd_ref, g_ref, b_ref):
    f = wd_ref.shape[0]
    xb = x.astype(BF16)
    h = _dot(xb, wgu_ref[:, :f])
    u = _dot(xb, wgu_ref[:, f:])
    a = h * (1.0 / (1.0 + jnp.exp(-h))) * u
    y = ALPHA * x + 0.5 * _dot(a.astype(BF16), wd_ref[...])
    return _layer_norm(y, g_ref[...], b_ref[...])


def _ffn_proj_kernel(x_ref, wgu_ref, wd_ref, g_ref, b_ref, *refs, n_proj):
    w_refs, o_ref, p_refs = refs[:n_proj], refs[n_proj], refs[n_proj + 1:]
    y = _ffn_ln_value(x_ref[...], wgu_ref, wd_ref, g_ref, b_ref)
    o_ref[...] = y
    yb = y.astype(BF16)
    for w_ref, p_ref in zip(w_refs, p_refs):
        p_ref[...] = _dot(yb, w_ref[...]).astype(p_ref.dtype)


def _ffn_proj(x, w_gu, w_down, layer, g, b, projs, tm=512):
    n, d = x.shape
    tm = min(tm, n)
    ws = [w for w, _ in projs]
    return pl.pallas_call(
        functools.partial(_ffn_proj_kernel, n_proj=len(projs)),
        grid=(n // tm,),
        in_specs=[_rows(tm, d), _layer(w_gu, layer), _layer(w_down, layer), _resident((1, d)),
                  _resident((1, d))] + [_resident(w.shape) for w in ws],
        out_specs=[_rows(tm, d)] + [_rows(tm, w.shape[1]) for w in ws],
        out_shape=[jax.ShapeDtypeStruct((n, d), F32)]
        + [jax.ShapeDtypeStruct((n, w.shape[1]), dt) for w, dt in projs],
        compiler_params=_params("parallel"),
        name="ffn_in_proj",
    )(x, w_gu, w_down, g.reshape(1, d), b.reshape(1, d), *ws)


def _proj_ffn_kernel(o_ref, wo_ref, x_ref, g1_ref, b1_ref, wgu_ref, wd_ref, g2_ref, b2_ref, out_ref):
    y = ALPHA * x_ref[...] + _dot(o_ref[...], wo_ref[...])
    x2 = _layer_norm(y, g1_ref[...], b1_ref[...])
    out_ref[...] = _ffn_ln_value(x2, wgu_ref, wd_ref, g2_ref, b2_ref)


def _proj_ffn(o, w_o, x, g1, b1, w_gu, w_down, layer, g2, b2, tm=512):
    n, k = o.shape
    d = w_o.shape[1]
    tm = min(tm, n)
    vec = lambda a: a.reshape(1, d)
    return pl.pallas_call(
        _proj_ffn_kernel,
        grid=(n // tm,),
        in_specs=[_rows(tm, k), _resident(w_o.shape), _rows(tm, d), _resident((1, d)), _resident((1, d)),
                  _layer(w_gu, layer), _layer(w_down, layer), _resident((1, d)), _resident((1, d))],
        out_specs=_rows(tm, d),
        out_shape=jax.ShapeDtypeStruct((n, d), F32),
        compiler_params=_params("parallel"),
        name="out_proj_ffn",
    )(o, w_o, x, vec(g1), vec(b1), w_gu, w_down, vec(g2), vec(b2))


def _swa_kernel(sink_ref, q_ref, kc_ref, kp_ref, vc_ref, vp_ref, o_ref, *, tq):
    i = pl.program_id(1)
    t = SWA_BLOCK
    rep = SWA_HEADS // SWA_KV_HEADS
    kvs = range(SWA_KV_HEADS)
    lo = _lane_lo()
    rel = _tile_rel(t, 2 * t) + t
    col = lax.broadcasted_iota(jnp.int32, (t, 2 * t), 1)
    dist = rel.astype(F32)
    in_band = jnp.logical_and(rel >= 0, rel < SWA_BLOCK)
    slopes = [s * LOG2E for s in _alibi_slopes(SWA_HEADS)]
    lane = lambda g: slice(g * LANES, (g + 1) * LANES)
    for qb in range(tq // t):
        r0 = qb * t
        if qb == 0:
            ks = [jnp.concatenate([kp_ref[0, :, lane(g)], kc_ref[0, 0:t, lane(g)]], axis=0) for g in kvs]
            vs = [jnp.concatenate([vp_ref[0, :, lane(g)], vc_ref[0, 0:t, lane(g)]], axis=0) for g in kvs]
            valid = jnp.logical_and(in_band, col >= jnp.where(i == 0, t, 0))
        else:
            ks = [kc_ref[0, r0 - t:r0 + t, lane(g)] for g in kvs]
            vs = [vc_ref[0, r0 - t:r0 + t, lane(g)] for g in kvs]
            valid = in_band
        qst = []
        for g in kvs:
            parts = []
            for r in range(rep):
                h = g * rep + r
                qp = q_ref[0, r0:r0 + t, lane(h // 2)]
                parts.append(jnp.where(lo if h % 2 == 0 else jnp.logical_not(lo), qp, jnp.zeros_like(qp)))
            qst.append(jnp.concatenate(parts, axis=0))
        ss = [_dot_t(qst[g], ks[g]) for g in kvs]
        ps, dens = [], []
        for g in kvs:
            prow, drow = [], []
            for r in range(rep):
                h = g * rep + r
                s = jnp.where(valid, ss[g][r * t:(r + 1) * t] - slopes[h] * dist, NEG)
                sink = sink_ref[h]
                mx = jnp.maximum(jnp.max(s, axis=-1, keepdims=True), sink)
                p = jnp.exp2(s - mx)
                drow.append(jnp.sum(p, axis=-1, keepdims=True) + jnp.exp2(sink - mx))
                prow.append(p.astype(BF16))
            ps.append(jnp.concatenate(prow, axis=0))
            dens.append(drow)
        outs = [_dot(ps[g], vs[g]) for g in kvs]
        for g in kvs:
            for r in range(0, rep, 2):
                even = outs[g][r * t:(r + 1) * t] / dens[g][r]
                odd = outs[g][(r + 1) * t:(r + 2) * t] / dens[g][r + 1]
                o_ref[0, r0:r0 + t, lane((g * rep + r) // 2)] = jnp.where(lo, even, odd).astype(BF16)


def _swa_attention(qkv, sinks, batch, seq, tq=512):
    t = SWA_BLOCK
    tq = min(tq, seq)
    sub = tq // t
    qw = SWA_HEADS * SWA_HEAD_DIM
    kw = SWA_KV_HEADS * LANES
    qblk, kblk, vblk = 0, qw // kw, qw // kw + 1
    prev = lambda i: jnp.maximum(i * sub - 1, 0)
    return pl.pallas_call(
        functools.partial(_swa_kernel, tq=tq),
        grid=(batch, seq // tq),
        in_specs=[pl.BlockSpec(memory_space=pltpu.SMEM),
                  pl.BlockSpec((1, tq, qw), lambda b, i: (b, i, qblk)),
                  pl.BlockSpec((1, tq, kw), lambda b, i: (b, i, kblk)),
                  pl.BlockSpec((1, t, kw), lambda b, i: (b, prev(i), kblk)),
                  pl.BlockSpec((1, tq, kw), lambda b, i: (b, i, vblk)),
                  pl.BlockSpec((1, t, kw), lambda b, i: (b, prev(i), vblk))],
        out_specs=pl.BlockSpec((1, tq, qw), lambda b, i: (b, i, 0)),
        out_shape=jax.ShapeDtypeStruct((batch, seq, qw), BF16),
        compiler_params=_params("parallel", "arbitrary"),
        name="swa_attention",
    )(sinks, qkv, qkv, qkv, qkv, qkv)


def _mla_up_kernel(c_ref, gq_ref, gkv_ref, wq_ref, wk_ref, wv_ref, cq_ref, s1q_ref, s2q_ref,
                   ck_ref, s1k_ref, s2k_ref, q_ref, k_ref, v_ref):
    c = c_ref[...]
    cq = c[:, :MLA_Q_LORA]
    ckv = c[:, MLA_Q_LORA:MLA_Q_LORA + MLA_KV_LORA]
    kr = c[:, MLA_Q_LORA + MLA_KV_LORA:]
    cq = cq * lax.rsqrt(jnp.mean(cq * cq, axis=-1, keepdims=True) + RMS_EPS) * gq_ref[...]
    ckv = ckv * lax.rsqrt(jnp.mean(ckv * ckv, axis=-1, keepdims=True) + RMS_EPS) * gkv_ref[...]
    ckv_b = ckv.astype(BF16)
    q = _dot(cq.astype(BF16), wq_ref[...])
    kn = _dot(ckv_b, wk_ref[...])
    v_ref[...] = _dot(ckv_b, wv_ref[...]).astype(BF16)

    def rot(xh, cc, s1, s2):
        return xh * cc + pltpu.roll(xh, 16, 1) * s1 + pltpu.roll(xh, LANES - 16, 1) * s2

    kr = rot(kr, ck_ref[...], s1k_ref[...], s2k_ref[...])
    cq_t, s1q, s2q = cq_ref[...], s1q_ref[...], s2q_ref[...]
    for h in range(MLA_HEADS):
        sl = slice(h * LANES, (h + 1) * LANES)
        q_ref[:, sl] = rot(q[:, sl], cq_t, s1q, s2q).astype(BF16)
        k_ref[:, sl] = (kn[:, sl] + kr).astype(BF16)


def _mla_rope_tables(seq, scale):
    half = MLA_ROPE // 2
    inv = ROPE_THETA ** (-jnp.arange(0, MLA_ROPE, 2, dtype=F32) / MLA_ROPE)
    ang = jnp.arange(seq, dtype=F32)[:, None] * inv[None, :]
    cos, sin = jnp.cos(ang), jnp.sin(ang)
    z = lambda w: jnp.zeros((seq, w), F32)
    one = jnp.ones((seq, MLA_NOPE), F32)
    cc = jnp.concatenate([one, cos, cos, z(LANES - MLA_NOPE - MLA_ROPE)], axis=1) * scale
    s1 = jnp.concatenate([z(MLA_NOPE + half), sin, z(LANES - MLA_NOPE - MLA_ROPE)], axis=1) * scale
    s2 = jnp.concatenate([z(MLA_NOPE), -sin, z(LANES - MLA_NOPE - half)], axis=1) * scale
    return cc, s1, s2


def _mla_up(c, gq, gkv, wq, wk, wv, seq, tm=512):
    n, cw = c.shape
    tm = min(tm, seq)
    spt = seq // tm
    scale = LOG2E * (MLA_NOPE + MLA_ROPE) ** -0.5
    tq = _mla_rope_tables(seq, scale)
    tk = _mla_rope_tables(seq, 1.0)
    qk_w = MLA_HEADS * LANES
    v_w = MLA_HEADS * MLA_V
    full = lambda a: _resident(a.shape)
    tab = pl.BlockSpec((tm, LANES), lambda i: (i % spt, 0))
    gq2, gkv2 = gq.reshape(1, -1), gkv.reshape(1, -1)
    return pl.pallas_call(
        _mla_up_kernel,
        grid=(n // tm,),
        in_specs=[pl.BlockSpec((tm, cw), lambda i: (i, 0)), full(gq2), full(gkv2), full(wq), full(wk),
                  full(wv), tab, tab, tab, tab, tab, tab],
        out_specs=[pl.BlockSpec((tm, qk_w), lambda i: (i, 0)),
                   pl.BlockSpec((tm, qk_w), lambda i: (i, 0)),
                   pl.BlockSpec((tm, v_w), lambda i: (i, 0))],
        out_shape=[jax.ShapeDtypeStruct((n, qk_w), BF16), jax.ShapeDtypeStruct((n, qk_w), BF16),
                   jax.ShapeDtypeStruct((n, v_w), BF16)],
        compiler_params=_params("parallel"),
        name="mla_up_proj",
    )(c, gq2, gkv2, wq, wk, wv, *tq, *tk)


SUM_LANE = (HALF, 0)


def _with_sum_column(v):
    lane = lax.broadcasted_iota(jnp.int32, (1, LANES), 1)
    unit = lambda at: jnp.where(lane == at, 1.0, 0.0).astype(v.dtype)
    low = lane < HALF
    return [jnp.where(low, v, unit(SUM_LANE[0])), jnp.where(low, unit(SUM_LANE[1]), v)]


def _flash_finish(acc_ref):
    outs = [acc_ref[hh] / acc_ref[hh][:, SUM_LANE[hh]:SUM_LANE[hh] + 1] for hh in range(2)]
    return jnp.where(_lane_lo(), outs[0], outs[1])


def _row_max(s):
    mx = s[:, :LANES]
    for c0 in range(LANES, s.shape[1], LANES):
        mx = jnp.maximum(mx, s[:, c0:c0 + LANES])
    return jnp.max(mx, axis=-1, keepdims=True)


def _flash_init(tq):
    m = jnp.full((tq, LANES), NEG, F32)
    return (m, m)


def _flash_pair(ss, v, m_old, acc_ref):
    hs = range(2)
    vs = _with_sum_column(v)
    cols = [slice(c0, c0 + LANES) for c0 in range(0, ss[0].shape[1], LANES)]
    ms = [jnp.maximum(m_old[hh], _row_max(ss[hh])) for hh in hs]
    ps = [jnp.concatenate([jnp.exp2(ss[hh][:, c] - ms[hh]) for c in cols], axis=1).astype(BF16) for hh in hs]
    al = [jnp.exp2(m_old[hh] - ms[hh]) for hh in hs]
    for hh in hs:
        acc_ref[hh] = al[hh] * acc_ref[hh] + _dot(ps[hh], vs[hh])
    return tuple(ms)


SOFTMAX_ROWS = 32


def _flash_scratch(tq, tk):
    return [pltpu.VMEM((2, tq, tk), F32), pltpu.VMEM((2, tq, tk), BF16), pltpu.VMEM((2, tq, LANES), F32),
            pltpu.VMEM((2, tq, LANES), F32)]


def _flash_reset(m_ref, acc_ref):
    m_ref[...] = jnp.full(m_ref.shape, NEG, F32)
    acc_ref[...] = jnp.zeros(acc_ref.shape, F32)


def _flash_update(s_ref, p_ref, m_ref, acc_ref, v):
    _, tq, tk = s_ref.shape
    vs = _with_sum_column(v)
    for c in range(tq // SOFTMAX_ROWS):
        rows = slice(c * SOFTMAX_ROWS, (c + 1) * SOFTMAX_ROWS)
        for hh in range(2):
            s = s_ref[hh, rows, :]
            m_old = m_ref[hh, rows, :]
            m_new = jnp.maximum(m_old, _row_max(s))
            m_ref[hh, rows, :] = m_new
            acc_ref[hh, rows, :] = jnp.exp2(m_old - m_new) * acc_ref[hh, rows, :]
            for c0 in range(0, tk, LANES):
                p_ref[hh, rows, c0:c0 + LANES] = jnp.exp2(s[:, c0:c0 + LANES] - m_new).astype(BF16)
    for hh in range(2):
        acc_ref[hh] += _dot(p_ref[hh], vs[hh])


def _tile_rel(tq, tk):
    return (lax.broadcasted_iota(jnp.int32, (tq, tk), 0) - lax.broadcasted_iota(jnp.int32, (tq, tk), 1))


def _mla_attn_kernel(q_ref, k_ref, v_ref, o_ref, s_ref, p_ref, m_ref, acc_ref, *, tq, tk):
    r = tq // tk
    rel = _tile_rel(tq, tk)
    sls = [slice(hh * LANES, (hh + 1) * LANES) for hh in range(2)]
    for i in range(q_ref.shape[1] // tq):
        rows = slice(i * tq, (i + 1) * tq)

        def tile(j, diag, rows=rows):
            start = pl.multiple_of(j * tk, tk)
            for hh in range(2):
                s = _dot_t(q_ref[0, rows, sls[hh]], k_ref[0, pl.ds(start, tk), sls[hh]])
                if diag is not None:
                    s = jnp.where(rel >= diag * tk, s, NEG)
                s_ref[hh] = s
            _flash_update(s_ref, p_ref, m_ref, acc_ref, v_ref[0, pl.ds(start, tk), :])

        _flash_reset(m_ref, acc_ref)
        for d in range(r):
            tile(i * r + d, d)
        for j in range(i * r):
            tile(j, None)
        o_ref[0, rows, :] = _flash_finish(acc_ref).astype(BF16)


def _mla_attention(q, k, v, tq=512, tk=512):
    batch, seq, _ = q.shape
    tq, tk = min(tq, seq), min(tk, seq)
    return pl.pallas_call(
        functools.partial(_mla_attn_kernel, tq=tq, tk=tk),
        grid=(batch, MLA_HEADS // 2),
        in_specs=[pl.BlockSpec((1, seq, 2 * LANES), lambda b, m: (b, 0, m)),
                  pl.BlockSpec((1, seq, 2 * LANES), lambda b, m: (b, 0, m)),
                  pl.BlockSpec((1, seq, LANES), lambda b, m: (b, 0, m))],
        out_specs=pl.BlockSpec((1, seq, LANES), lambda b, m: (b, 0, m)),
        out_shape=jax.ShapeDtypeStruct(v.shape, BF16),
        scratch_shapes=_flash_scratch(tq, tk),
        compiler_params=_params("parallel", "parallel"),
        name="mla_attention",
    )(q, k, v)


def _order_key(x):
    bits = lax.bitcast_convert_type(x, jnp.int32)
    return bits ^ (lax.shift_right_arithmetic(bits, 31) & 0x7FFFFFFF)


def _dsa_kernel(slope_ref, pos_ref, qi_ref, ki_ref, wi_ref, q_ref, k_ref, v_ref, o_ref, key_ref, bias_ref,
                byte_ref, acc_ref, *, t, topk):
    i = pl.program_id(1)
    lo = _lane_lo()
    row = lax.broadcasted_iota(jnp.int32, (t, t), 0)
    col = lax.broadcasted_iota(jnp.int32, (t, t), 1)
    rel = row - col

    def _select():
        wi_t = jnp.transpose(wi_ref[0] * (IDX_HEADS ** -0.5))
        rel_t = col - row
        pack = 16

        def score_tile(j, carry):
            start = pl.multiple_of(j * t, t)
            ki = ki_ref[0, pl.ds(start, t), :]
            sc = jnp.zeros((t, t), F32)
            for p in range(IDX_HEADS // 2):
                qp = qi_ref[0, :, p * LANES:(p + 1) * LANES]
                for hh in range(2):
                    h = 2 * p + hh
                    qm = jnp.where(lo if hh == 0 else jnp.logical_not(lo), qp, jnp.zeros_like(qp))
                    dots = _dot_t(ki, qm)
                    sc = sc + wi_t[h:h + 1, :] * jnp.maximum(dots, 0.0)
            sc = sc + 0.0
            sc = jnp.where(rel_t >= (j - i) * t, sc, NEG)
            key_ref[j] = _order_key(sc)
            return carry

        lax.fori_loop(0, i + 1, score_tile, 0)

        one_b, zero_b = jnp.ones((), BF16), jnp.zeros((), BF16)

        def count_ge(cand):
            cand_b = cand.astype(BF16)

            def body(j, part):
                ge = jnp.where(byte_ref[j] >= cand_b, one_b, zero_b)
                slabs = [ge[r0:r0 + pack] for r0 in range(0, t, pack)]
                while len(slabs) > 1:
                    slabs = [a + b for a, b in zip(slabs[0::2], slabs[1::2])]
                return part + slabs[0]
            part = lax.fori_loop(0, i + 1, body, jnp.zeros((pack, t), BF16))
            return jnp.sum(part.astype(F32), axis=0, keepdims=True)

        prefix = jnp.zeros((1, t), jnp.int32)
        above = jnp.zeros((1, t), F32)
        for stage in range(4):
            shift = 24 - 8 * stage

            def fill(j, carry, stage=stage, shift=shift, prefix=prefix):
                key = key_ref[j]
                if stage == 0:
                    byte = lax.shift_right_arithmetic(key, shift) + 128
                else:
                    match = lax.shift_right_arithmetic(key, shift + 8) == prefix
                    byte = jnp.where(match, lax.shift_right_logical(key, shift) & 255, -1)
                byte_ref[j] = byte.astype(F32).astype(BF16)
                return carry

            lax.fori_loop(0, i + 1, fill, 0)
            val = jnp.zeros((1, t), F32)
            rej = jnp.zeros((1, t), F32)
            for bit in (128, 64, 32, 16, 8, 4, 2, 1):
                cand = val + float(bit)
                cnt = count_ge(cand)
                ok = above + cnt >= topk
                val = jnp.where(ok, cand, val)
                rej = jnp.where(ok, rej, cnt)
            above = above + rej
            byte_i = val.astype(jnp.int32) - (128 if stage == 0 else 0)
            prefix = byte_i if stage == 0 else lax.shift_left(prefix, 8) | byte_i
        thr = prefix
        need = topk - above
        before = jnp.where(row > col, 1.0, 0.0).astype(BF16)

        def select_tile(j, seen):
            key = key_ref[j]
            eq = key == thr
            eq_b = jnp.where(eq, 1.0, 0.0).astype(BF16)
            rank = seen + _dot(before, eq_b)
            sel = jnp.logical_or(key > thr, jnp.logical_and(eq, rank < need))
            sel = jnp.logical_and(sel, rel_t >= (j - i) * t)
            bias_ref[j] = jnp.transpose(jnp.where(sel, 0.0, NEG))
            return seen + jnp.sum(eq_b.astype(F32), axis=0, keepdims=True)

        lax.fori_loop(0, i + 1, select_tile, jnp.zeros((1, t), F32))

    _select()
    pairs = DSA_HEADS // 2
    for m in range(pairs):
        lanes = slice(m * LANES, (m + 1) * LANES)
        g = m // (pairs // DSA_KV_HEADS)
        kv_lanes = slice(g * LANES, (g + 1) * LANES)
        qp = q_ref[0, :, lanes]
        qs = [jnp.where(lo, qp, slope_ref[2 * m:2 * m + 1, :]),
              jnp.where(lo, slope_ref[2 * m + 1:2 * m + 2, :], qp)]

        def tile(j, carry, qs=qs, kv_lanes=kv_lanes):
            start = pl.multiple_of(j * t, t)
            k = k_ref[0, pl.ds(start, t), kv_lanes]
            v = v_ref[0, pl.ds(start, t), kv_lanes]
            ks = [jnp.where(lo, k, pos_ref[0, pl.ds(start, t), :]), jnp.where(lo, pos_ref[1, pl.ds(start, t), :], k)]
            bias = bias_ref[j]
            ss = [_dot_t(qs[hh], ks[hh]) + bias for hh in range(2)]
            return _flash_pair(ss, v, carry, acc_ref)

        acc_ref[...] = jnp.zeros(acc_ref.shape, F32)
        carry = tile(i, _flash_init(t))
        lax.fori_loop(0, i, tile, carry)
        o_ref[0, :, lanes] = _flash_finish(acc_ref).astype(BF16)


ALIBI_PIECES = 6


def _dsa_alibi_tables(seq):
    bf = lambda a: np.asarray(np.asarray(a, np.float32).astype(jnp.bfloat16), np.float32)
    slopes = np.asarray(_alibi_slopes(DSA_HEADS), np.float64) * LOG2E
    slopes = slopes.astype(np.float32)
    s1 = bf(slopes)
    s2 = bf(slopes - s1)
    s3 = bf(slopes - s1 - s2)
    slope_tab = np.zeros((DSA_HEADS, LANES), np.float32)
    for h in range(DSA_HEADS):
        off = HALF if h % 2 == 0 else 0
        slope_tab[h, off:off + ALIBI_PIECES] = [s1[h], s2[h], s3[h], s1[h], s2[h], s3[h]]
    pos = np.arange(seq)
    hi, lw = (pos // 256 * 256).astype(np.float32), (pos % 256).astype(np.float32)
    pieces = np.stack([hi, hi, hi, lw, lw, lw], axis=1)
    pos_tab = np.zeros((2, seq, LANES), np.float32)
    pos_tab[0, :, HALF:HALF + ALIBI_PIECES] = pieces
    pos_tab[1, :, :ALIBI_PIECES] = pieces
    return jnp.asarray(slope_tab, BF16), jnp.asarray(pos_tab, BF16)


def _dsa_attention(proj, wi, batch, seq, t=512):
    t = min(t, seq)
    nt = seq // t
    topk = min(IDX_TOPK, seq // 4)
    qw = DSA_HEADS * DSA_HEAD_DIM
    q0 = 0
    k0 = qw // LANES
    v0 = k0 + DSA_KV_HEADS
    qi0 = (v0 + DSA_KV_HEADS) * LANES // (IDX_HEADS * IDX_DIM)
    ki0 = v0 + DSA_KV_HEADS + IDX_HEADS * IDX_DIM // LANES
    kvw = DSA_KV_HEADS * LANES
    slope_tab, pos_tab = _dsa_alibi_tables(seq)
    return pl.pallas_call(
        functools.partial(_dsa_kernel, t=t, topk=topk),
        grid=(batch, nt),
        in_specs=[pl.BlockSpec(slope_tab.shape, lambda b, i: (0, 0)),
                  pl.BlockSpec(pos_tab.shape, lambda b, i: (0, 0, 0)),
                  pl.BlockSpec((1, t, IDX_HEADS * IDX_DIM), lambda b, i: (b, i, qi0)),
                  pl.BlockSpec((1, seq, LANES), lambda b, i: (b, 0, ki0)),
                  pl.BlockSpec((1, t, LANES), lambda b, i: (b, i, 0)),
                  pl.BlockSpec((1, t, qw), lambda b, i: (b, i, q0)),
                  pl.BlockSpec((1, seq, kvw), lambda b, i: (b, 0, k0 * LANES // kvw)),
                  pl.BlockSpec((1, seq, kvw), lambda b, i: (b, 0, v0 * LANES // kvw))],
        out_specs=pl.BlockSpec((1, t, qw), lambda b, i: (b, i, 0)),
        out_shape=jax.ShapeDtypeStruct((batch, seq, qw), BF16),
        scratch_shapes=[pltpu.VMEM((nt, t, t), jnp.int32), pltpu.VMEM((nt, t, t), F32),
                        pltpu.VMEM((nt, t, t), BF16), pltpu.VMEM((2, t, LANES), F32)],
        compiler_params=_params("parallel", "arbitrary"),
        name="dsa_attention",
    )(slope_tab, pos_tab, proj, proj, wi, proj, proj, proj)


def _sb_kernel(q_ref, k_ref, v_ref, o_ref, later_ref, acc_ref, *, tq, tk):
    r = tq // tk
    lo = _lane_lo()
    rel = _tile_rel(tq, tk)
    blk = min(tk, 2 * LANES)
    suffix = jnp.where(_tile_rel(blk, blk) >= 0, 1.0, 0.0).astype(BF16)
    cols = [slice(c0, c0 + LANES) for c0 in range(0, tk, LANES)]
    hs = range(2)
    for i in range(q_ref.shape[1] // tq):
        rows = slice(i * tq, (i + 1) * tq)
        qp = q_ref[0, rows, :]
        qs = [jnp.where(lo if hh == 0 else jnp.logical_not(lo), qp, jnp.zeros_like(qp)) for hh in hs]

        def tile(j, diag, qs=qs):
            start = pl.multiple_of(j * tk, tk)
            k = k_ref[0, pl.ds(start, tk), :]
            v = v_ref[0, pl.ds(start, tk), :]
            zs = [_dot_t(qs[hh], k) for hh in hs]
            lgs = []
            for z in zs:
                nz = -z
                lgs.append(jnp.minimum(nz, 0.0) - jnp.log2(1.0 + jnp.exp2(jnp.minimum(z, nz))))
            if diag is not None:
                strict = rel > diag * tk
                lgs = [jnp.where(strict, lg, 0.0) for lg in lgs]
            lbs = [lg.astype(BF16) for lg in lgs]
            runs, tots = [], []
            for hh in hs:
                parts, after = [], later_ref[hh]
                for c0 in reversed(range(0, tk, blk)):
                    part = _dot(lbs[hh][:, c0:c0 + blk], suffix) + jnp.concatenate([after] * (blk // LANES), axis=1)
                    after = jnp.broadcast_to(part[:, 0:1], (tq, LANES))
                    parts.insert(0, part)
                runs.append(jnp.concatenate(parts, axis=1))
                tots.append(after)
            a_s = [jnp.exp2(zs[hh] + runs[hh]) for hh in hs]
            if diag is not None:
                a_s = [jnp.where(strict, a, 0.0) for a in a_s]
            for hh in hs:
                later_ref[hh] = tots[hh]
            for hh in hs:
                acc_ref[hh] += _dot(a_s[hh].astype(BF16), v)

        later_ref[...] = jnp.zeros(later_ref.shape, F32)
        acc_ref[...] = jnp.zeros(acc_ref.shape, F32)
        for d in reversed(range(r)):
            tile(i * r + d, d)
        for j in reversed(range(i * r)):
            tile(j, None)
        o_ref[0, rows, :] = jnp.where(lo, acc_ref[0], acc_ref[1]).astype(BF16)


def _sb_attention(qkv, batch, seq, tq=512, tk=512):
    tq, tk = min(tq, seq), min(tk, seq)
    w = SB_HEADS * SB_HEAD_DIM
    pairs = SB_HEADS // 2
    return pl.pallas_call(
        functools.partial(_sb_kernel, tq=tq, tk=tk),
        grid=(batch, pairs),
        in_specs=[pl.BlockSpec((1, seq, LANES), lambda b, m: (b, 0, m)),
                  pl.BlockSpec((1, seq, LANES), lambda b, m: (b, 0, pairs + m)),
                  pl.BlockSpec((1, seq, LANES), lambda b, m: (b, 0, 2 * pairs + m))],
        out_specs=pl.BlockSpec((1, seq, LANES), lambda b, m: (b, 0, m)),
        out_shape=jax.ShapeDtypeStruct((batch, seq, w), BF16),
        scratch_shapes=[pltpu.VMEM((2, tq, LANES), F32), pltpu.VMEM((2, tq, LANES), F32)],
        compiler_params=_params("parallel", "parallel"),
        name="sb_attention",
    )(qkv, qkv, qkv)


def _dup_heads(w, n_heads, dim):
    d = w.shape[0]
    w = w.reshape(d, n_heads, 1, dim)
    return jnp.broadcast_to(w, (d, n_heads, 2, dim)).reshape(d, n_heads * 2 * dim)


def _pad_heads(w, n_heads, dim, to):
    d = w.shape[0]
    w = w.reshape(d, n_heads, dim)
    return jnp.pad(w, ((0, 0), (0, 0), (0, to - dim))).reshape(d, n_heads * to)


def _swa_w_in(w):
    qw = SWA_HEADS * SWA_HEAD_DIM
    kw = SWA_KV_HEADS * SWA_HEAD_DIM
    q, k, v = w[:, :qw], w[:, qw:qw + kw], w[:, qw + kw:]
    q = q * (LOG2E * SWA_HEAD_DIM ** -0.5)
    return jnp.concatenate([q, _dup_heads(k, SWA_KV_HEADS, SWA_HEAD_DIM),
                            _dup_heads(v, SWA_KV_HEADS, SWA_HEAD_DIM)], axis=1).astype(BF16)


def _mla_w_in(w):
    d = w.shape[0]
    lat = MLA_Q_LORA + MLA_KV_LORA
    z = lambda c: jnp.zeros((d, c), w.dtype)
    return jnp.concatenate([w[:, :lat], z(MLA_NOPE), w[:, lat:], z(LANES - MLA_NOPE - MLA_ROPE)],
                           axis=1).astype(BF16)


def _sb_w_in(w):
    qw = SB_HEADS * SB_HEAD_DIM
    return jnp.concatenate([w[:, :qw] * (LOG2E * SB_HEAD_DIM ** -0.5), w[:, qw:]], axis=1).astype(BF16)


def _dsa_w_in(w):
    qw = DSA_HEADS * DSA_HEAD_DIM
    kw = DSA_KV_HEADS * DSA_HEAD_DIM
    iw = IDX_HEADS * IDX_DIM
    c = np.cumsum([qw, kw, kw, iw, IDX_DIM]).tolist()
    q, k, v, qi, ki, wi = (w[:, :c[0]], w[:, c[0]:c[1]], w[:, c[1]:c[2]], w[:, c[2]:c[3]],
                           w[:, c[3]:c[4]], w[:, c[4]:])
    q = q * (LOG2E * DSA_HEAD_DIM ** -0.5)
    qi = qi * (IDX_DIM ** -0.5)
    main = jnp.concatenate([q, _dup_heads(k, DSA_KV_HEADS, DSA_HEAD_DIM),
                            _dup_heads(v, DSA_KV_HEADS, DSA_HEAD_DIM), qi, _dup_heads(ki, 1, IDX_DIM)],
                           axis=1).astype(BF16)
    wi = jnp.pad(wi, ((0, 0), (0, LANES - IDX_HEADS))).astype(BF16)
    return main, wi


def kernel(x, ffn1_w_gu, ffn1_w_down, ffn2_w_gu, ffn2_w_down, ln_g, ln_b, a_w_in, a_sinks, a_w_o,
           b_w_in, b_q_norm, b_w_uq, b_kv_norm, b_w_ukv, b_w_o, c_w_in, c_w_o, d_w_in, d_w_o):
    batch, seq, d = x.shape
    n = batch * seq
    x = x.reshape(n, d)
    ffn1_gu, ffn1_down = ffn1_w_gu.astype(BF16), ffn1_w_down.astype(BF16)
    ffn2_gu, ffn2_down = ffn2_w_gu.astype(BF16), ffn2_w_down.astype(BF16)
    for i in range(DEPTH):
        mixer, j = i % N_MIXERS, i // N_MIXERS

        def ffn1(projs):
            return _ffn_proj(x, ffn1_gu, ffn1_down, i, ln_g[i, 0], ln_b[i, 0], projs)

        if mixer == 0:
            x, qkv = ffn1([(_swa_w_in(a_w_in[j]), BF16)])
            o = _swa_attention(qkv.reshape(batch, seq, -1), a_sinks[j] * LOG2E, batch, seq)
            w_o = a_w_o[j]
        elif mixer == 1:
            x, c = ffn1([(_mla_w_in(b_w_in[j]), F32)])
            hd = MLA_NOPE + MLA_ROPE
            wq = _pad_heads(b_w_uq[j], MLA_HEADS, hd, LANES).astype(BF16)
            wkv = b_w_ukv[j].reshape(MLA_KV_LORA, MLA_HEADS, MLA_NOPE + MLA_V)
            wk = _pad_heads(wkv[:, :, :MLA_NOPE].reshape(MLA_KV_LORA, -1), MLA_HEADS, MLA_NOPE,
                            LANES).astype(BF16)
            wv = wkv[:, :, MLA_NOPE:].reshape(MLA_KV_LORA, -1).astype(BF16)
            q, k, v = _mla_up(c, b_q_norm[j], b_kv_norm[j], wq, wk, wv, seq)
            o = _mla_attention(q.reshape(batch, seq, -1), k.reshape(batch, seq, -1),
                               v.reshape(batch, seq, -1))
            w_o = b_w_o[j]
        elif mixer == 2:
            w_main, w_wi = _dsa_w_in(c_w_in[j])
            x, proj, wi = ffn1([(w_main, BF16), (w_wi, F32)])
            o = _dsa_attention(proj.reshape(batch, seq, -1), wi.reshape(batch, seq, -1), batch, seq)
            w_o = c_w_o[j]
        else:
            x, qkv = ffn1([(_sb_w_in(d_w_in[j]), BF16)])
            o = _sb_attention(qkv.reshape(batch, seq, -1), batch, seq)
            w_o = d_w_o[j]
        x = _proj_ffn(o.reshape(n, -1), w_o.astype(BF16), x, ln_g[i, 1], ln_b[i, 1], ffn2_gu, ffn2_down, i,
                      ln_g[i, 2], ln_b[i, 2])
    return x.reshape(batch, seq, d)
```

```python
import functools
import math

import numpy as np
import jax
import jax.numpy as jnp
from jax import lax
from jax.experimental import pallas as pl
from jax.experimental.pallas import tpu as pltpu

D_MODEL = 1024
DEPTH = 4
N_MIXERS = 4
D_FF = 2816
LN_EPS = 1e-5
RMS_EPS = 1e-6
NEG = -1e30

SWA_HEADS, SWA_KV_HEADS, SWA_HEAD_DIM, SWA_BLOCK = 16, 4, 64, 128
MLA_HEADS, MLA_Q_LORA, MLA_KV_LORA, MLA_NOPE, MLA_ROPE, MLA_V = 16, 768, 256, 64, 32, 64
ROPE_THETA = 10000.0
DSA_HEADS, DSA_KV_HEADS, DSA_HEAD_DIM = 16, 4, 64
IDX_HEADS, IDX_DIM, IDX_TOPK = 8, 64, 256
SB_HEADS, SB_HEAD_DIM = 16, 64

ALPHA = (2.0 * DEPTH) ** 0.25
LOG2E = math.log2(math.e)

LANES = 128
HALF = 64
VMEM_LIMIT = 56 * 1024 * 1024
BF16 = jnp.bfloat16
F32 = jnp.float32


def _params(*sem, flags=None):
    return pltpu.CompilerParams(dimension_semantics=sem, vmem_limit_bytes=VMEM_LIMIT, flags=flags)


def _dot(a, b):
    return jnp.dot(a, b, preferred_element_type=F32)


def _dot_t(a, b):
    return lax.dot_general(a, b, (((1,), (1,)), ((), ())), preferred_element_type=F32)


def _alibi_slopes(n_heads):
    return [2.0 ** (-8.0 * (i + 1) / n_heads) for i in range(n_heads)]


def _layer_norm(y, g, b):
    mu = jnp.mean(y, axis=-1, keepdims=True)
    yc = y - mu
    var = jnp.mean(yc * yc, axis=-1, keepdims=True)
    return yc * lax.rsqrt(var + LN_EPS) * g + b


def _lane_lo():
    return lax.broadcasted_iota(jnp.int32, (1, LANES), 1) < HALF


def _resident(shape):
    return pl.BlockSpec(shape, lambda i: (0,) * len(shape), pipeline_mode=pl.Buffered(1))


def _layer(stack, layer):
    return pl.BlockSpec((None,) + stack.shape[1:], lambda i: (layer, 0, 0), pipeline_mode=pl.Buffered(1))


def _rows(tm, width):
    return pl.BlockSpec((tm, width), lambda i: (i, 0))


def _ffn_ln_value(x, wgu_ref, wd_ref, g_ref, b_ref):
    f = wd_ref.shape[0]
    xb = x.astype(BF16)
    h = _dot(xb, wgu_ref[:, :f])
    u = _dot(xb, wgu_ref[:, f:])
    a = h * (1.0 / (1.0 + jnp.exp(-h))) * u
    y = ALPHA * x + 0.5 * _dot(a.astype(BF16), wd_ref[...])
    return _layer_norm(y, g_ref[...], b_ref[...])


def _ffn_proj_kernel(x_ref, wgu_ref, wd_ref, g_ref, b_ref, *refs, n_proj):
    w_refs, o_ref, p_refs = refs[:n_proj], refs[n_proj], refs[n_proj + 1:]
    y = _ffn_ln_value(x_ref[...], wgu_ref, wd_ref, g_ref, b_ref)
    o_ref[...] = y
    yb = y.astype(BF16)
    for w_ref, p_ref in zip(w_refs, p_refs):
        p_ref[...] = _dot(yb, w_ref[...]).astype(p_ref.dtype)


def _ffn_proj(x, w_gu, w_down, layer, g, b, projs, tm=512):
    n, d = x.shape
    tm = min(tm, n)
    ws = [w for w, _ in projs]
    return pl.pallas_call(
        functools.partial(_ffn_proj_kernel, n_proj=len(projs)),
        grid=(n // tm,),
        in_specs=[_rows(tm, d), _layer(w_gu, layer), _layer(w_down, layer), _resident((1, d)),
                  _resident((1, d))] + [_resident(w.shape) for w in ws],
        out_specs=[_rows(tm, d)] + [_rows(tm, w.shape[1]) for w in ws],
        out_shape=[jax.ShapeDtypeStruct((n, d), F32)]
        + [jax.ShapeDtypeStruct((n, w.shape[1]), dt) for w, dt in projs],
        compiler_params=_params("parallel"),
        name="ffn_in_proj",
    )(x, w_gu, w_down, g.reshape(1, d), b.reshape(1, d), *ws)


def _proj_ffn_kernel(o_ref, wo_ref, x_ref, g1_ref, b1_ref, wgu_ref, wd_ref, g2_ref, b2_ref, out_ref):
    y = ALPHA * x_ref[...] + _dot(o_ref[...], wo_ref[...])
    x2 = _layer_norm(y, g1_ref[...], b1_ref[...])
    out_ref[...] = _ffn_ln_value(x2, wgu_ref, wd_ref, g2_ref, b2_ref)


def _proj_ffn(o, w_o, x, g1, b1, w_gu, w_down, layer, g2, b2, tm=512):
    n, k = o.shape
    d = w_o.shape[1]
    tm = min(tm, n)
    vec = lambda a: a.reshape(1, d)
    return pl.pallas_call(
        _proj_ffn_kernel,
        grid=(n // tm,),
        in_specs=[_rows(tm, k), _resident(w_o.shape), _rows(tm, d), _resident((1, d)), _resident((1, d)),
                  _layer(w_gu, layer), _layer(w_down, layer), _resident((1, d)), _resident((1, d))],
        out_specs=_rows(tm, d),
        out_shape=jax.ShapeDtypeStruct((n, d), F32),
        compiler_params=_params("parallel"),
        name="out_proj_ffn",
    )(o, w_o, x, vec(g1), vec(b1), w_gu, w_down, vec(g2), vec(b2))


def _swa_kernel(sink_ref, q_ref, kc_ref, kp_ref, vc_ref, vp_ref, o_ref, *, tq):
    i = pl.program_id(1)
    t = SWA_BLOCK
    rep = SWA_HEADS // SWA_KV_HEADS
    kvs = range(SWA_KV_HEADS)
    lo = _lane_lo()
    rel = _tile_rel(t, 2 * t) + t
    col = lax.broadcasted_iota(jnp.int32, (t, 2 * t), 1)
    dist = rel.astype(F32)
    in_band = jnp.logical_and(rel >= 0, rel < SWA_BLOCK)
    slopes = [s * LOG2E for s in _alibi_slopes(SWA_HEADS)]
    lane = lambda g: slice(g * LANES, (g + 1) * LANES)
    for qb in range(tq // t):
        r0 = qb * t
        if qb == 0:
            ks = [jnp.concatenate([kp_ref[0, :, lane(g)], kc_ref[0, 0:t, lane(g)]], axis=0) for g in kvs]
            vs = [jnp.concatenate([vp_ref[0, :, lane(g)], vc_ref[0, 0:t, lane(g)]], axis=0) for g in kvs]
            valid = jnp.logical_and(in_band, col >= jnp.where(i == 0, t, 0))
        else:
            ks = [kc_ref[0, r0 - t:r0 + t, lane(g)] for g in kvs]
            vs = [vc_ref[0, r0 - t:r0 + t, lane(g)] for g in kvs]
            valid = in_band
        qst = []
        for g in kvs:
            parts = []
            for r in range(rep):
                h = g * rep + r
                qp = q_ref[0, r0:r0 + t, lane(h // 2)]
                parts.append(jnp.where(lo if h % 2 == 0 else jnp.logical_not(lo), qp, jnp.zeros_like(qp)))
            qst.append(jnp.concatenate(parts, axis=0))
        ss = [_dot_t(qst[g], ks[g]) for g in kvs]
        ps, dens = [], []
        for g in kvs:
            prow, drow = [], []
            for r in range(rep):
                h = g * rep + r
                s = jnp.where(valid, ss[g][r * t:(r + 1) * t] - slopes[h] * dist, NEG)
                sink = sink_ref[h]
                mx = jnp.maximum(jnp.max(s, axis=-1, keepdims=True), sink)
                p = jnp.exp2(s - mx)
                drow.append(jnp.sum(p, axis=-1, keepdims=True) + jnp.exp2(sink - mx))
                prow.append(p.astype(BF16))
            ps.append(jnp.concatenate(prow, axis=0))
            dens.append(drow)
        outs = [_dot(ps[g], vs[g]) for g in kvs]
        for g in kvs:
            for r in range(0, rep, 2):
                even = outs[g][r * t:(r + 1) * t] / dens[g][r]
                odd = outs[g][(r + 1) * t:(r + 2) * t] / dens[g][r + 1]
                o_ref[0, r0:r0 + t, lane((g * rep + r) // 2)] = jnp.where(lo, even, odd).astype(BF16)


def _swa_attention(qkv, sinks, batch, seq, tq=512):
    t = SWA_BLOCK
    tq = min(tq, seq)
    sub = tq // t
    qw = SWA_HEADS * SWA_HEAD_DIM
    kw = SWA_KV_HEADS * LANES
    qblk, kblk, vblk = 0, qw // kw, qw // kw + 1
    prev = lambda i: jnp.maximum(i * sub - 1, 0)
    return pl.pallas_call(
        functools.partial(_swa_kernel, tq=tq),
        grid=(batch, seq // tq),
        in_specs=[pl.BlockSpec(memory_space=pltpu.SMEM),
                  pl.BlockSpec((1, tq, qw), lambda b, i: (b, i, qblk)),
                  pl.BlockSpec((1, tq, kw), lambda b, i: (b, i, kblk)),
                  pl.BlockSpec((1, t, kw), lambda b, i: (b, prev(i), kblk)),
                  pl.BlockSpec((1, tq, kw), lambda b, i: (b, i, vblk)),
                  pl.BlockSpec((1, t, kw), lambda b, i: (b, prev(i), vblk))],
        out_specs=pl.BlockSpec((1, tq, qw), lambda b, i: (b, i, 0)),
        out_shape=jax.ShapeDtypeStruct((batch, seq, qw), BF16),
        compiler_params=_params("parallel", "arbitrary"),
        name="swa_attention",
    )(sinks, qkv, qkv, qkv, qkv, qkv)


def _mla_up_kernel(c_ref, gq_ref, gkv_ref, wq_ref, wk_ref, wv_ref, cq_ref, s1q_ref, s2q_ref,
                   ck_ref, s1k_ref, s2k_ref, q_ref, k_ref, v_ref):
    c = c_ref[...]
    cq = c[:, :MLA_Q_LORA]
    ckv = c[:, MLA_Q_LORA:MLA_Q_LORA + MLA_KV_LORA]
    kr = c[:, MLA_Q_LORA + MLA_KV_LORA:]
    cq = cq * lax.rsqrt(jnp.mean(cq * cq, axis=-1, keepdims=True) + RMS_EPS) * gq_ref[...]
    ckv = ckv * lax.rsqrt(jnp.mean(ckv * ckv, axis=-1, keepdims=True) + RMS_EPS) * gkv_ref[...]
    ckv_b = ckv.astype(BF16)
    q = _dot(cq.astype(BF16), wq_ref[...])
    kn = _dot(ckv_b, wk_ref[...])
    v_ref[...] = _dot(ckv_b, wv_ref[...]).astype(BF16)

    def rot(xh, cc, s1, s2):
        return xh * cc + pltpu.roll(xh, 16, 1) * s1 + pltpu.roll(xh, LANES - 16, 1) * s2

    kr = rot(kr, ck_ref[...], s1k_ref[...], s2k_ref[...])
    cq_t, s1q, s2q = cq_ref[...], s1q_ref[...], s2q_ref[...]
    for h in range(MLA_HEADS):
        sl = slice(h * LANES, (h + 1) * LANES)
        q_ref[:, sl] = rot(q[:, sl], cq_t, s1q, s2q).astype(BF16)
        k_ref[:, sl] = (kn[:, sl] + kr).astype(BF16)


def _mla_rope_tables(seq, scale):
    half = MLA_ROPE // 2
    inv = ROPE_THETA ** (-jnp.arange(0, MLA_ROPE, 2, dtype=F32) / MLA_ROPE)
    ang = jnp.arange(seq, dtype=F32)[:, None] * inv[None, :]
    cos, sin = jnp.cos(ang), jnp.sin(ang)
    z = lambda w: jnp.zeros((seq, w), F32)
    one = jnp.ones((seq, MLA_NOPE), F32)
    cc = jnp.concatenate([one, cos, cos, z(LANES - MLA_NOPE - MLA_ROPE)], axis=1) * scale
    s1 = jnp.concatenate([z(MLA_NOPE + half), sin, z(LANES - MLA_NOPE - MLA_ROPE)], axis=1) * scale
    s2 = jnp.concatenate([z(MLA_NOPE), -sin, z(LANES - MLA_NOPE - half)], axis=1) * scale
    return cc, s1, s2


def _mla_up(c, gq, gkv, wq, wk, wv, seq, tm=512):
    n, cw = c.shape
    tm = min(tm, seq)
    spt = seq // tm
    scale = LOG2E * (MLA_NOPE + MLA_ROPE) ** -0.5
    tq = _mla_rope_tables(seq, scale)
    tk = _mla_rope_tables(seq, 1.0)
    qk_w = MLA_HEADS * LANES
    v_w = MLA_HEADS * MLA_V
    full = lambda a: _resident(a.shape)
    tab = pl.BlockSpec((tm, LANES), lambda i: (i % spt, 0))
    gq2, gkv2 = gq.reshape(1, -1), gkv.reshape(1, -1)
    return pl.pallas_call(
        _mla_up_kernel,
        grid=(n // tm,),
        in_specs=[pl.BlockSpec((tm, cw), lambda i: (i, 0)), full(gq2), full(gkv2), full(wq), full(wk),
                  full(wv), tab, tab, tab, tab, tab, tab],
        out_specs=[pl.BlockSpec((tm, qk_w), lambda i: (i, 0)),
                   pl.BlockSpec((tm, qk_w), lambda i: (i, 0)),
                   pl.BlockSpec((tm, v_w), lambda i: (i, 0))],
        out_shape=[jax.ShapeDtypeStruct((n, qk_w), BF16), jax.ShapeDtypeStruct((n, qk_w), BF16),
                   jax.ShapeDtypeStruct((n, v_w), BF16)],
        compiler_params=_params("parallel"),
        name="mla_up_proj",
    )(c, gq2, gkv2, wq, wk, wv, *tq, *tk)


SUM_LANE = (HALF, 0)


def _with_sum_column(v):
    lane = lax.broadcasted_iota(jnp.int32, (1, LANES), 1)
    unit = lambda at: jnp.where(lane == at, 1.0, 0.0).astype(v.dtype)
    low = lane < HALF
    return [jnp.where(low, v, unit(SUM_LANE[0])), jnp.where(low, unit(SUM_LANE[1]), v)]


def _flash_finish(acc_ref, first=0):
    outs = [acc_ref[first + hh] / acc_ref[first + hh][:, SUM_LANE[hh]:SUM_LANE[hh] + 1] for hh in range(2)]
    return jnp.where(_lane_lo(), outs[0], outs[1])


def _row_max(s):
    mx = s[:, :LANES]
    for c0 in range(LANES, s.shape[1], LANES):
        mx = jnp.maximum(mx, s[:, c0:c0 + LANES])
    return jnp.max(mx, axis=-1, keepdims=True)


def _flash_init(tq, heads=2):
    return (jnp.full((tq, LANES), NEG, F32),) * heads


def _flash_pair(ss, v, m_old, acc_ref):
    hs = range(len(ss))
    vs = _with_sum_column(v)
    cols = [slice(c0, c0 + LANES) for c0 in range(0, ss[0].shape[1], LANES)]
    ms = [jnp.maximum(m_old[hh], _row_max(ss[hh])) for hh in hs]
    ps = [jnp.concatenate([jnp.exp2(ss[hh][:, c] - ms[hh]) for c in cols], axis=1).astype(BF16) for hh in hs]
    al = [jnp.exp2(m_old[hh] - ms[hh]) for hh in hs]
    for hh in hs:
        acc_ref[hh] = al[hh] * acc_ref[hh] + _dot(ps[hh], vs[hh % 2])
    return tuple(ms)


SOFTMAX_ROWS = 32


def _flash_scratch(tq, tk):
    return [pltpu.VMEM((2, tq, tk), F32), pltpu.VMEM((2, tq, tk), BF16), pltpu.VMEM((2, tq, LANES), F32),
            pltpu.VMEM((2, tq, LANES), F32)]


def _flash_reset(m_ref, acc_ref):
    m_ref[...] = jnp.full(m_ref.shape, NEG, F32)
    acc_ref[...] = jnp.zeros(acc_ref.shape, F32)


def _flash_update(s_ref, p_ref, m_ref, acc_ref, v):
    _, tq, tk = s_ref.shape
    vs = _with_sum_column(v)
    for c in range(tq // SOFTMAX_ROWS):
        rows = slice(c * SOFTMAX_ROWS, (c + 1) * SOFTMAX_ROWS)
        for hh in range(2):
            s = s_ref[hh, rows, :]
            m_old = m_ref[hh, rows, :]
            m_new = jnp.maximum(m_old, _row_max(s))
            m_ref[hh, rows, :] = m_new
            acc_ref[hh, rows, :] = jnp.exp2(m_old - m_new) * acc_ref[hh, rows, :]
            for c0 in range(0, tk, LANES):
                p_ref[hh, rows, c0:c0 + LANES] = jnp.exp2(s[:, c0:c0 + LANES] - m_new).astype(BF16)
    for hh in range(2):
        acc_ref[hh] += _dot(p_ref[hh], vs[hh])


def _tile_rel(tq, tk):
    return (lax.broadcasted_iota(jnp.int32, (tq, tk), 0) - lax.broadcasted_iota(jnp.int32, (tq, tk), 1))


def _mla_attn_kernel(q_ref, k_ref, v_ref, o_ref, s_ref, p_ref, m_ref, acc_ref, *, tq, tk):
    r = tq // tk
    rel = _tile_rel(tq, tk)
    sls = [slice(hh * LANES, (hh + 1) * LANES) for hh in range(2)]
    for i in range(q_ref.shape[1] // tq):
        rows = slice(i * tq, (i + 1) * tq)

        def tile(j, diag, rows=rows):
            start = pl.multiple_of(j * tk, tk)
            for hh in range(2):
                s = _dot_t(q_ref[0, rows, sls[hh]], k_ref[0, pl.ds(start, tk), sls[hh]])
                if diag is not None:
                    s = jnp.where(rel >= diag * tk, s, NEG)
                s_ref[hh] = s
            _flash_update(s_ref, p_ref, m_ref, acc_ref, v_ref[0, pl.ds(start, tk), :])

        _flash_reset(m_ref, acc_ref)
        for d in range(r):
            tile(i * r + d, d)
        for j in range(i * r):
            tile(j, None)
        o_ref[0, rows, :] = _flash_finish(acc_ref).astype(BF16)


def _mla_attention(q, k, v, tq=512, tk=512):
    batch, seq, _ = q.shape
    tq, tk = min(tq, seq), min(tk, seq)
    return pl.pallas_call(
        functools.partial(_mla_attn_kernel, tq=tq, tk=tk),
        grid=(batch, MLA_HEADS // 2),
        in_specs=[pl.BlockSpec((1, seq, 2 * LANES), lambda b, m: (b, 0, m)),
                  pl.BlockSpec((1, seq, 2 * LANES), lambda b, m: (b, 0, m)),
                  pl.BlockSpec((1, seq, LANES), lambda b, m: (b, 0, m))],
        out_specs=pl.BlockSpec((1, seq, LANES), lambda b, m: (b, 0, m)),
        out_shape=jax.ShapeDtypeStruct(v.shape, BF16),
        scratch_shapes=_flash_scratch(tq, tk),
        compiler_params=_params("parallel", "parallel"),
        name="mla_attention",
    )(q, k, v)


def _order_key(x):
    bits = lax.bitcast_convert_type(x, jnp.int32)
    return bits ^ (lax.shift_right_arithmetic(bits, 31) & 0x7FFFFFFF)


def _dsa_kernel(slope_ref, pos_ref, qi_ref, ki_ref, wi_ref, q_ref, k_ref, v_ref, o_ref, key_ref, bias_ref,
                byte_ref, acc_ref, *, t, topk):
    i = pl.program_id(1)
    lo = _lane_lo()
    row = lax.broadcasted_iota(jnp.int32, (t, t), 0)
    col = lax.broadcasted_iota(jnp.int32, (t, t), 1)
    rel = row - col

    def _select():
        wi_t = jnp.transpose(wi_ref[0] * (IDX_HEADS ** -0.5))
        rel_t = col - row
        pack = 16

        def score_tile(j, carry):
            start = pl.multiple_of(j * t, t)
            ki = ki_ref[0, pl.ds(start, t), :]
            sc = jnp.zeros((t, t), F32)
            for p in range(IDX_HEADS // 2):
                qp = qi_ref[0, :, p * LANES:(p + 1) * LANES]
                for hh in range(2):
                    h = 2 * p + hh
                    qm = jnp.where(lo if hh == 0 else jnp.logical_not(lo), qp, jnp.zeros_like(qp))
                    dots = _dot_t(ki, qm)
                    sc = sc + wi_t[h:h + 1, :] * jnp.maximum(dots, 0.0)
            sc = sc + 0.0
            sc = jnp.where(rel_t >= (j - i) * t, sc, NEG)
            key_ref[j] = _order_key(sc)
            return carry

        lax.fori_loop(0, i + 1, score_tile, 0)

        one_b, zero_b = jnp.ones((), BF16), jnp.zeros((), BF16)

        def count_ge(cand):
            cand_b = cand.astype(BF16)

            def body(j, part):
                ge = jnp.where(byte_ref[j] >= cand_b, one_b, zero_b)
                slabs = [ge[r0:r0 + pack] for r0 in range(0, t, pack)]
                while len(slabs) > 1:
                    slabs = [a + b for a, b in zip(slabs[0::2], slabs[1::2])]
                return part + slabs[0]
            part = lax.fori_loop(0, i + 1, body, jnp.zeros((pack, t), BF16))
            return jnp.sum(part.astype(F32), axis=0, keepdims=True)

        prefix = jnp.zeros((1, t), jnp.int32)
        above = jnp.zeros((1, t), F32)
        for stage in range(4):
            shift = 24 - 8 * stage

            def fill(j, carry, stage=stage, shift=shift, prefix=prefix):
                key = key_ref[j]
                if stage == 0:
                    byte = lax.shift_right_arithmetic(key, shift) + 128
                else:
                    match = lax.shift_right_arithmetic(key, shift + 8) == prefix
                    byte = jnp.where(match, lax.shift_right_logical(key, shift) & 255, -1)
                byte_ref[j] = byte.astype(F32).astype(BF16)
                return carry

            lax.fori_loop(0, i + 1, fill, 0)
            val = jnp.zeros((1, t), F32)
            rej = jnp.zeros((1, t), F32)
            for bit in (128, 64, 32, 16, 8, 4, 2, 1):
                cand = val + float(bit)
                cnt = count_ge(cand)
                ok = above + cnt >= topk
                val = jnp.where(ok, cand, val)
                rej = jnp.where(ok, rej, cnt)
            above = above + rej
            byte_i = val.astype(jnp.int32) - (128 if stage == 0 else 0)
            prefix = byte_i if stage == 0 else lax.shift_left(prefix, 8) | byte_i
        thr = prefix
        need = topk - above
        before = jnp.where(row > col, 1.0, 0.0).astype(BF16)

        def select_tile(j, seen):
            key = key_ref[j]
            eq = key == thr
            eq_b = jnp.where(eq, 1.0, 0.0).astype(BF16)
            rank = seen + _dot(before, eq_b)
            sel = jnp.logical_or(key > thr, jnp.logical_and(eq, rank < need))
            sel = jnp.logical_and(sel, rel_t >= (j - i) * t)
            bias_ref[j] = jnp.transpose(jnp.where(sel, 0.0, NEG))
            return seen + jnp.sum(eq_b.astype(F32), axis=0, keepdims=True)

        lax.fori_loop(0, i + 1, select_tile, jnp.zeros((1, t), F32))

    _select()
    rep = DSA_HEADS // DSA_KV_HEADS
    for g in range(DSA_KV_HEADS):
        kv_lanes = slice(g * LANES, (g + 1) * LANES)
        qs = []
        for h in range(g * rep, (g + 1) * rep):
            qp = q_ref[0, :, (h // 2) * LANES:(h // 2 + 1) * LANES]
            extra = slope_ref[h:h + 1, :]
            qs.append(jnp.where(lo, qp, extra) if h % 2 == 0 else jnp.where(lo, extra, qp))

        def tile(j, carry, qs=qs, kv_lanes=kv_lanes):
            start = pl.multiple_of(j * t, t)
            k = k_ref[0, pl.ds(start, t), kv_lanes]
            v = v_ref[0, pl.ds(start, t), kv_lanes]
            ks = [jnp.where(lo, k, pos_ref[0, pl.ds(start, t), :]), jnp.where(lo, pos_ref[1, pl.ds(start, t), :], k)]
            bias = bias_ref[j]
            ss = [_dot_t(qs[n], ks[n % 2]) + bias for n in range(rep)]
            return _flash_pair(ss, v, carry, acc_ref)

        acc_ref[...] = jnp.zeros(acc_ref.shape, F32)
        carry = tile(i, _flash_init(t, rep))
        lax.fori_loop(0, i, tile, carry)
        for n in range(0, rep, 2):
            pair = (g * rep + n) // 2
            o_ref[0, :, pair * LANES:(pair + 1) * LANES] = _flash_finish(acc_ref, n).astype(BF16)


ALIBI_PIECES = 6


def _dsa_alibi_tables(seq):
    bf = lambda a: np.asarray(np.asarray(a, np.float32).astype(jnp.bfloat16), np.float32)
    slopes = np.asarray(_alibi_slopes(DSA_HEADS), np.float64) * LOG2E
    slopes = slopes.astype(np.float32)
    s1 = bf(slopes)
    s2 = bf(slopes - s1)
    s3 = bf(slopes - s1 - s2)
    slope_tab = np.zeros((DSA_HEADS, LANES), np.float32)
    for h in range(DSA_HEADS):
        off = HALF if h % 2 == 0 else 0
        slope_tab[h, off:off + ALIBI_PIECES] = [s1[h], s2[h], s3[h], s1[h], s2[h], s3[h]]
    pos = np.arange(seq)
    hi, lw = (pos // 256 * 256).astype(np.float32), (pos % 256).astype(np.float32)
    pieces = np.stack([hi, hi, hi, lw, lw, lw], axis=1)
    pos_tab = np.zeros((2, seq, LANES), np.float32)
    pos_tab[0, :, HALF:HALF + ALIBI_PIECES] = pieces
    pos_tab[1, :, :ALIBI_PIECES] = pieces
    return jnp.asarray(slope_tab, BF16), jnp.asarray(pos_tab, BF16)


def _dsa_attention(proj, wi, batch, seq, t=512):
    t = min(t, seq)
    nt = seq // t
    topk = min(IDX_TOPK, seq // 4)
    qw = DSA_HEADS * DSA_HEAD_DIM
    q0 = 0
    k0 = qw // LANES
    v0 = k0 + DSA_KV_HEADS
    qi0 = (v0 + DSA_KV_HEADS) * LANES // (IDX_HEADS * IDX_DIM)
    ki0 = v0 + DSA_KV_HEADS + IDX_HEADS * IDX_DIM // LANES
    kvw = DSA_KV_HEADS * LANES
    slope_tab, pos_tab = _dsa_alibi_tables(seq)
    return pl.pallas_call(
        functools.partial(_dsa_kernel, t=t, topk=topk),
        grid=(batch, nt),
        in_specs=[pl.BlockSpec(slope_tab.shape, lambda b, i: (0, 0)),
                  pl.BlockSpec(pos_tab.shape, lambda b, i: (0, 0, 0)),
                  pl.BlockSpec((1, t, IDX_HEADS * IDX_DIM), lambda b, i: (b, i, qi0)),
                  pl.BlockSpec((1, seq, LANES), lambda b, i: (b, 0, ki0)),
                  pl.BlockSpec((1, t, LANES), lambda b, i: (b, i, 0)),
                  pl.BlockSpec((1, t, qw), lambda b, i: (b, i, q0)),
                  pl.BlockSpec((1, seq, kvw), lambda b, i: (b, 0, k0 * LANES // kvw)),
                  pl.BlockSpec((1, seq, kvw), lambda b, i: (b, 0, v0 * LANES // kvw))],
        out_specs=pl.BlockSpec((1, t, qw), lambda b, i: (b, i, 0)),
        out_shape=jax.ShapeDtypeStruct((batch, seq, qw), BF16),
        scratch_shapes=[pltpu.VMEM((nt, t, t), jnp.int32), pltpu.VMEM((nt, t, t), F32),
                        pltpu.VMEM((nt, t, t), BF16),
                        pltpu.VMEM((DSA_HEADS // DSA_KV_HEADS, t, LANES), F32)],
        compiler_params=_params("parallel", "arbitrary"),
        name="dsa_attention",
    )(slope_tab, pos_tab, proj, proj, wi, proj, proj, proj)


def _sb_kernel(q_ref, k_ref, v_ref, o_ref, later_ref, acc_ref, *, tq, tk):
    r = tq // tk
    lo = _lane_lo()
    rel = _tile_rel(tq, tk)
    blk = min(tk, 2 * LANES)
    suffix = jnp.where(_tile_rel(blk, blk) >= 0, 1.0, 0.0).astype(BF16)
    cols = [slice(c0, c0 + LANES) for c0 in range(0, tk, LANES)]
    hs = range(2)
    for i in range(q_ref.shape[1] // tq):
        rows = slice(i * tq, (i + 1) * tq)
        qp = q_ref[0, rows, :]
        qs = [jnp.where(lo if hh == 0 else jnp.logical_not(lo), qp, jnp.zeros_like(qp)) for hh in hs]

        def tile(j, diag, qs=qs):
            start = pl.multiple_of(j * tk, tk)
            k = k_ref[0, pl.ds(start, tk), :]
            v = v_ref[0, pl.ds(start, tk), :]
            zs = [_dot_t(qs[hh], k) for hh in hs]
            lgs = []
            for z in zs:
                nz = -z
                lgs.append(jnp.minimum(nz, 0.0) - jnp.log2(1.0 + jnp.exp2(jnp.minimum(z, nz))))
            if diag is not None:
                strict = rel > diag * tk
                lgs = [jnp.where(strict, lg, 0.0) for lg in lgs]
            lbs = [lg.astype(BF16) for lg in lgs]
            runs, tots = [], []
            for hh in hs:
                parts, after = [], later_ref[hh]
                for c0 in reversed(range(0, tk, blk)):
                    part = _dot(lbs[hh][:, c0:c0 + blk], suffix) + jnp.concatenate([after] * (blk // LANES), axis=1)
                    after = jnp.broadcast_to(part[:, 0:1], (tq, LANES))
                    parts.insert(0, part)
                runs.append(jnp.concatenate(parts, axis=1))
                tots.append(after)
            a_s = [jnp.exp2(zs[hh] + runs[hh]) for hh in hs]
            if diag is not None:
                a_s = [jnp.where(strict, a, 0.0) for a in a_s]
            for hh in hs:
                later_ref[hh] = tots[hh]
            for hh in hs:
                acc_ref[hh] += _dot(a_s[hh].astype(BF16), v)

        later_ref[...] = jnp.zeros(later_ref.shape, F32)
        acc_ref[...] = jnp.zeros(acc_ref.shape, F32)
        for d in reversed(range(r)):
            tile(i * r + d, d)
        for j in reversed(range(i * r)):
            tile(j, None)
        o_ref[0, rows, :] = jnp.where(lo, acc_ref[0], acc_ref[1]).astype(BF16)


def _sb_attention(qkv, batch, seq, tq=512, tk=512):
    tq, tk = min(tq, seq), min(tk, seq)
    w = SB_HEADS * SB_HEAD_DIM
    pairs = SB_HEADS // 2
    return pl.pallas_call(
        functools.partial(_sb_kernel, tq=tq, tk=tk),
        grid=(batch, pairs),
        in_specs=[pl.BlockSpec((1, seq, LANES), lambda b, m: (b, 0, m)),
                  pl.BlockSpec((1, seq, LANES), lambda b, m: (b, 0, pairs + m)),
                  pl.BlockSpec((1, seq, LANES), lambda b, m: (b, 0, 2 * pairs + m))],
        out_specs=pl.BlockSpec((1, seq, LANES), lambda b, m: (b, 0, m)),
        out_shape=jax.ShapeDtypeStruct((batch, seq, w), BF16),
        scratch_shapes=[pltpu.VMEM((2, tq, LANES), F32), pltpu.VMEM((2, tq, LANES), F32)],
        compiler_params=_params("parallel", "parallel"),
        name="sb_attention",
    )(qkv, qkv, qkv)


def _dup_heads(w, n_heads, dim):
    d = w.shape[0]
    w = w.reshape(d, n_heads, 1, dim)
    return jnp.broadcast_to(w, (d, n_heads, 2, dim)).reshape(d, n_heads * 2 * dim)


def _pad_heads(w, n_heads, dim, to):
    d = w.shape[0]
    w = w.reshape(d, n_heads, dim)
    return jnp.pad(w, ((0, 0), (0, 0), (0, to - dim))).reshape(d, n_heads * to)


def _swa_w_in(w):
    qw = SWA_HEADS * SWA_HEAD_DIM
    kw = SWA_KV_HEADS * SWA_HEAD_DIM
    q, k, v = w[:, :qw], w[:, qw:qw + kw], w[:, qw + kw:]
    q = q * (LOG2E * SWA_HEAD_DIM ** -0.5)
    return jnp.concatenate([q, _dup_heads(k, SWA_KV_HEADS, SWA_HEAD_DIM),
                            _dup_heads(v, SWA_KV_HEADS, SWA_HEAD_DIM)], axis=1).astype(BF16)


def _mla_w_in(w):
    d = w.shape[0]
    lat = MLA_Q_LORA + MLA_KV_LORA
    z = lambda c: jnp.zeros((d, c), w.dtype)
    return jnp.concatenate([w[:, :lat], z(MLA_NOPE), w[:, lat:], z(LANES - MLA_NOPE - MLA_ROPE)],
                           axis=1).astype(BF16)


def _sb_w_in(w):
    qw = SB_HEADS * SB_HEAD_DIM
    return jnp.concatenate([w[:, :qw] * (LOG2E * SB_HEAD_DIM ** -0.5), w[:, qw:]], axis=1).astype(BF16)


def _dsa_w_in(w):
    qw = DSA_HEADS * DSA_HEAD_DIM
    kw = DSA_KV_HEADS * DSA_HEAD_DIM
    iw = IDX_HEADS * IDX_DIM
    c = np.cumsum([qw, kw, kw, iw, IDX_DIM]).tolist()
    q, k, v, qi, ki, wi = (w[:, :c[0]], w[:, c[0]:c[1]], w[:, c[1]:c[2]], w[:, c[2]:c[3]],
                           w[:, c[3]:c[4]], w[:, c[4]:])
    q = q * (LOG2E * DSA_HEAD_DIM ** -0.5)
    qi = qi * (IDX_DIM ** -0.5)
    main = jnp.concatenate([q, _dup_heads(k, DSA_KV_HEADS, DSA_HEAD_DIM),
                            _dup_heads(v, DSA_KV_HEADS, DSA_HEAD_DIM), qi, _dup_heads(ki, 1, IDX_DIM)],
                           axis=1).astype(BF16)
    wi = jnp.pad(wi, ((0, 0), (0, LANES - IDX_HEADS))).astype(BF16)
    return main, wi


def kernel(x, ffn1_w_gu, ffn1_w_down, ffn2_w_gu, ffn2_w_down, ln_g, ln_b, a_w_in, a_sinks, a_w_o,
           b_w_in, b_q_norm, b_w_uq, b_kv_norm, b_w_ukv, b_w_o, c_w_in, c_w_o, d_w_in, d_w_o):
    batch, seq, d = x.shape
    n = batch * seq
    x = x.reshape(n, d)
    ffn1_gu, ffn1_down = ffn1_w_gu.astype(BF16), ffn1_w_down.astype(BF16)
    ffn2_gu, ffn2_down = ffn2_w_gu.astype(BF16), ffn2_w_down.astype(BF16)
    for i in range(DEPTH):
        mixer, j = i % N_MIXERS, i // N_MIXERS

        def ffn1(projs):
            return _ffn_proj(x, ffn1_gu, ffn1_down, i, ln_g[i, 0], ln_b[i, 0], projs)

        if mixer == 0:
            x, qkv = ffn1([(_swa_w_in(a_w_in[j]), BF16)])
            o = _swa_attention(qkv.reshape(batch, seq, -1), a_sinks[j] * LOG2E, batch, seq)
            w_o = a_w_o[j]
        elif mixer == 1:
            x, c = ffn1([(_mla_w_in(b_w_in[j]), F32)])
            hd = MLA_NOPE + MLA_ROPE
            wq = _pad_heads(b_w_uq[j], MLA_HEADS, hd, LANES).astype(BF16)
            wkv = b_w_ukv[j].reshape(MLA_KV_LORA, MLA_HEADS, MLA_NOPE + MLA_V)
            wk = _pad_heads(wkv[:, :, :MLA_NOPE].reshape(MLA_KV_LORA, -1), MLA_HEADS, MLA_NOPE,
                            LANES).astype(BF16)
            wv = wkv[:, :, MLA_NOPE:].reshape(MLA_KV_LORA, -1).astype(BF16)
            q, k, v = _mla_up(c, b_q_norm[j], b_kv_norm[j], wq, wk, wv, seq)
            o = _mla_attention(q.reshape(batch, seq, -1), k.reshape(batch, seq, -1),
                               v.reshape(batch, seq, -1))
            w_o = b_w_o[j]
        elif mixer == 2:
            w_main, w_wi = _dsa_w_in(c_w_in[j])
            x, proj, wi = ffn1([(w_main, BF16), (w_wi, F32)])
            o = _dsa_attention(proj.reshape(batch, seq, -1), wi.reshape(batch, seq, -1), batch, seq)
            w_o = c_w_o[j]
        else:
            x, qkv = ffn1([(_sb_w_in(d_w_in[j]), BF16)])
            o = _sb_attention(qkv.reshape(batch, seq, -1), batch, seq)
            w_o = d_w_o[j]
        x = _proj_ffn(o.reshape(n, -1), w_o.astype(BF16), x, ln_g[i, 1], ln_b[i, 1], ffn2_gu, ffn2_down, i,
                      ln_g[i, 2], ln_b[i, 2])
    return x.reshape(batch, seq, d)
```

```python
import functools
import math

import numpy as np
import jax
import jax.numpy as jnp
from jax import lax
from jax.experimental import pallas as pl
from jax.experimental.pallas import tpu as pltpu

DEPTH = 4
N_MIXERS = 4
LN_EPS = 1e-5
RMS_EPS = 1e-6
NEG = -1e30

SWA_HEADS, SWA_KV_HEADS, SWA_HEAD_DIM, SWA_BLOCK = 16, 4, 64, 128
MLA_HEADS, MLA_Q_LORA, MLA_KV_LORA, MLA_NOPE, MLA_ROPE, MLA_V = 16, 768, 256, 64, 32, 64
ROPE_THETA = 10000.0
DSA_HEADS, DSA_KV_HEADS, DSA_HEAD_DIM = 16, 4, 64
IDX_HEADS, IDX_DIM, IDX_TOPK = 8, 64, 256
SB_HEADS, SB_HEAD_DIM = 16, 64

ALPHA = (2.0 * DEPTH) ** 0.25
LOG2E = math.log2(math.e)

LANES = 128
HALF = 64
VMEM_LIMIT = 56 * 1024 * 1024
BF16 = jnp.bfloat16
F32 = jnp.float32


def _params(*sem):
    return pltpu.CompilerParams(dimension_semantics=sem, vmem_limit_bytes=VMEM_LIMIT)


def _dot(a, b):
    return jnp.dot(a, b, preferred_element_type=F32)


def _dot_t(a, b):
    return lax.dot_general(a, b, (((1,), (1,)), ((), ())), preferred_element_type=F32)


def _alibi_slopes(n_heads):
    return [2.0 ** (-8.0 * (i + 1) / n_heads) for i in range(n_heads)]


def _layer_norm(y, g, b):
    mu = jnp.mean(y, axis=-1, keepdims=True)
    yc = y - mu
    var = jnp.mean(yc * yc, axis=-1, keepdims=True)
    return yc * lax.rsqrt(var + LN_EPS) * g + b


def _lane_lo():
    return lax.broadcasted_iota(jnp.int32, (1, LANES), 1) < HALF


def _resident(shape):
    return pl.BlockSpec(shape, lambda i: (0,) * len(shape), pipeline_mode=pl.Buffered(1))


def _layer(stack, layer):
    return pl.BlockSpec((None,) + stack.shape[1:], lambda i: (layer, 0, 0), pipeline_mode=pl.Buffered(1))


def _rows(tm, width):
    return pl.BlockSpec((tm, width), lambda i: (i, 0))


def _ffn_ln_value(x, wgu_ref, wd_ref, g_ref, b_ref):
    f = wd_ref.shape[0]
    xb = x.astype(BF16)
    h = _dot(xb, wgu_ref[:, :f])
    u = _dot(xb, wgu_ref[:, f:])
    a = h * (1.0 / (1.0 + jnp.exp(-h))) * u
    y = ALPHA * x + 0.5 * _dot(a.astype(BF16), wd_ref[...])
    return _layer_norm(y, g_ref[...], b_ref[...])


def _ffn_proj_kernel(x_ref, wgu_ref, wd_ref, g_ref, b_ref, *refs, n_proj):
    w_refs, o_ref, p_refs = refs[:n_proj], refs[n_proj], refs[n_proj + 1:]
    y = _ffn_ln_value(x_ref[...], wgu_ref, wd_ref, g_ref, b_ref)
    o_ref[...] = y
    yb = y.astype(BF16)
    for w_ref, p_ref in zip(w_refs, p_refs):
        p_ref[...] = _dot(yb, w_ref[...]).astype(p_ref.dtype)


def _ffn_proj(x, w_gu, w_down, layer, g, b, projs, tm=512):
    n, d = x.shape
    tm = min(tm, n)
    ws = [w for w, _ in projs]
    return pl.pallas_call(
        functools.partial(_ffn_proj_kernel, n_proj=len(projs)),
        grid=(n // tm,),
        in_specs=[_rows(tm, d), _layer(w_gu, layer), _layer(w_down, layer), _resident((1, d)),
                  _resident((1, d))] + [_resident(w.shape) for w in ws],
        out_specs=[_rows(tm, d)] + [_rows(tm, w.shape[1]) for w in ws],
        out_shape=[jax.ShapeDtypeStruct((n, d), F32)]
        + [jax.ShapeDtypeStruct((n, w.shape[1]), dt) for w, dt in projs],
        compiler_params=_params("parallel"),
        name="ffn_in_proj",
    )(x, w_gu, w_down, g.reshape(1, d), b.reshape(1, d), *ws)


def _proj_ffn_kernel(o_ref, wo_ref, x_ref, g1_ref, b1_ref, wgu_ref, wd_ref, g2_ref, b2_ref, out_ref):
    y = ALPHA * x_ref[...] + _dot(o_ref[...], wo_ref[...])
    x2 = _layer_norm(y, g1_ref[...], b1_ref[...])
    out_ref[...] = _ffn_ln_value(x2, wgu_ref, wd_ref, g2_ref, b2_ref)


def _proj_ffn(o, w_o, x, g1, b1, w_gu, w_down, layer, g2, b2, tm=512):
    n, k = o.shape
    d = w_o.shape[1]
    tm = min(tm, n)
    vec = lambda a: a.reshape(1, d)
    return pl.pallas_call(
        _proj_ffn_kernel,
        grid=(n // tm,),
        in_specs=[_rows(tm, k), _resident(w_o.shape), _rows(tm, d), _resident((1, d)), _resident((1, d)),
                  _layer(w_gu, layer), _layer(w_down, layer), _resident((1, d)), _resident((1, d))],
        out_specs=_rows(tm, d),
        out_shape=jax.ShapeDtypeStruct((n, d), F32),
        compiler_params=_params("parallel"),
        name="out_proj_ffn",
    )(o, w_o, x, vec(g1), vec(b1), w_gu, w_down, vec(g2), vec(b2))


def _swa_kernel(sink_ref, q_ref, kc_ref, kp_ref, vc_ref, vp_ref, o_ref, *, tq):
    i = pl.program_id(1)
    t = SWA_BLOCK
    rep = SWA_HEADS // SWA_KV_HEADS
    kvs = range(SWA_KV_HEADS)
    lo = _lane_lo()
    rel = _tile_rel(t, 2 * t) + t
    col = lax.broadcasted_iota(jnp.int32, (t, 2 * t), 1)
    dist = rel.astype(F32)
    in_band = jnp.logical_and(rel >= 0, rel < SWA_BLOCK)
    slopes = [s * LOG2E for s in _alibi_slopes(SWA_HEADS)]
    lane = lambda g: slice(g * LANES, (g + 1) * LANES)
    for qb in range(tq // t):
        r0 = qb * t
        if qb == 0:
            ks = [jnp.concatenate([kp_ref[0, :, lane(g)], kc_ref[0, 0:t, lane(g)]], axis=0) for g in kvs]
            vs = [jnp.concatenate([vp_ref[0, :, lane(g)], vc_ref[0, 0:t, lane(g)]], axis=0) for g in kvs]
            valid = jnp.logical_and(in_band, col >= jnp.where(i == 0, t, 0))
        else:
            ks = [kc_ref[0, r0 - t:r0 + t, lane(g)] for g in kvs]
            vs = [vc_ref[0, r0 - t:r0 + t, lane(g)] for g in kvs]
            valid = in_band
        qst = []
        for g in kvs:
            parts = []
            for r in range(rep):
                h = g * rep + r
                qp = q_ref[0, r0:r0 + t, lane(h // 2)]
                parts.append(jnp.where(lo if h % 2 == 0 else jnp.logical_not(lo), qp, jnp.zeros_like(qp)))
            qst.append(jnp.concatenate(parts, axis=0))
        ss = [_dot_t(qst[g], ks[g]) for g in kvs]
        ps, dens = [], []
        for g in kvs:
            prow, drow = [], []
            for r in range(rep):
                h = g * rep + r
                s = jnp.where(valid, ss[g][r * t:(r + 1) * t] - slopes[h] * dist, NEG)
                sink = sink_ref[h]
                mx = jnp.maximum(jnp.max(s, axis=-1, keepdims=True), sink)
                p = jnp.exp2(s - mx)
                drow.append(jnp.sum(p, axis=-1, keepdims=True) + jnp.exp2(sink - mx))
                prow.append(p.astype(BF16))
            ps.append(jnp.concatenate(prow, axis=0))
            dens.append(drow)
        outs = [_dot(ps[g], vs[g]) for g in kvs]
        for g in kvs:
            for r in range(0, rep, 2):
                even = outs[g][r * t:(r + 1) * t] / dens[g][r]
                odd = outs[g][(r + 1) * t:(r + 2) * t] / dens[g][r + 1]
                o_ref[0, r0:r0 + t, lane((g * rep + r) // 2)] = jnp.where(lo, even, odd).astype(BF16)


def _swa_attention(qkv, sinks, batch, seq, tq=512):
    t = SWA_BLOCK
    tq = min(tq, seq)
    sub = tq // t
    qw = SWA_HEADS * SWA_HEAD_DIM
    kw = SWA_KV_HEADS * LANES
    qblk, kblk, vblk = 0, qw // kw, qw // kw + 1
    prev = lambda i: jnp.maximum(i * sub - 1, 0)
    return pl.pallas_call(
        functools.partial(_swa_kernel, tq=tq),
        grid=(batch, seq // tq),
        in_specs=[pl.BlockSpec(memory_space=pltpu.SMEM),
                  pl.BlockSpec((1, tq, qw), lambda b, i: (b, i, qblk)),
                  pl.BlockSpec((1, tq, kw), lambda b, i: (b, i, kblk)),
                  pl.BlockSpec((1, t, kw), lambda b, i: (b, prev(i), kblk)),
                  pl.BlockSpec((1, tq, kw), lambda b, i: (b, i, vblk)),
                  pl.BlockSpec((1, t, kw), lambda b, i: (b, prev(i), vblk))],
        out_specs=pl.BlockSpec((1, tq, qw), lambda b, i: (b, i, 0)),
        out_shape=jax.ShapeDtypeStruct((batch, seq, qw), BF16),
        compiler_params=_params("parallel", "arbitrary"),
        name="swa_attention",
    )(sinks, qkv, qkv, qkv, qkv, qkv)


def _mla_up_kernel(c_ref, gq_ref, gkv_ref, wq_ref, wk_ref, wv_ref, cq_ref, s1q_ref, s2q_ref,
                   ck_ref, s1k_ref, s2k_ref, q_ref, k_ref, v_ref):
    c = c_ref[...]
    cq = c[:, :MLA_Q_LORA]
    ckv = c[:, MLA_Q_LORA:MLA_Q_LORA + MLA_KV_LORA]
    kr = c[:, MLA_Q_LORA + MLA_KV_LORA:]
    cq = cq * lax.rsqrt(jnp.mean(cq * cq, axis=-1, keepdims=True) + RMS_EPS) * gq_ref[...]
    ckv = ckv * lax.rsqrt(jnp.mean(ckv * ckv, axis=-1, keepdims=True) + RMS_EPS) * gkv_ref[...]
    ckv_b = ckv.astype(BF16)
    q = _dot(cq.astype(BF16), wq_ref[...])
    kn = _dot(ckv_b, wk_ref[...])
    v_ref[...] = _dot(ckv_b, wv_ref[...]).astype(BF16)

    def rot(xh, cc, s1, s2):
        return xh * cc + pltpu.roll(xh, 16, 1) * s1 + pltpu.roll(xh, LANES - 16, 1) * s2

    kr = rot(kr, ck_ref[...], s1k_ref[...], s2k_ref[...])
    cq_t, s1q, s2q = cq_ref[...], s1q_ref[...], s2q_ref[...]
    for h in range(MLA_HEADS):
        sl = slice(h * LANES, (h + 1) * LANES)
        q_ref[:, sl] = rot(q[:, sl], cq_t, s1q, s2q).astype(BF16)
        k_ref[:, sl] = (kn[:, sl] + kr).astype(BF16)


def _mla_rope_tables(seq, scale):
    half = MLA_ROPE // 2
    inv = ROPE_THETA ** (-jnp.arange(0, MLA_ROPE, 2, dtype=F32) / MLA_ROPE)
    ang = jnp.arange(seq, dtype=F32)[:, None] * inv[None, :]
    cos, sin = jnp.cos(ang), jnp.sin(ang)
    z = lambda w: jnp.zeros((seq, w), F32)
    one = jnp.ones((seq, MLA_NOPE), F32)
    cc = jnp.concatenate([one, cos, cos, z(LANES - MLA_NOPE - MLA_ROPE)], axis=1) * scale
    s1 = jnp.concatenate([z(MLA_NOPE + half), sin, z(LANES - MLA_NOPE - MLA_ROPE)], axis=1) * scale
    s2 = jnp.concatenate([z(MLA_NOPE), -sin, z(LANES - MLA_NOPE - half)], axis=1) * scale
    return cc, s1, s2


def _mla_up(c, gq, gkv, wq, wk, wv, seq, tm=512):
    n, cw = c.shape
    tm = min(tm, seq)
    spt = seq // tm
    scale = LOG2E * (MLA_NOPE + MLA_ROPE) ** -0.5
    tq = _mla_rope_tables(seq, scale)
    tk = _mla_rope_tables(seq, 1.0)
    qk_w = MLA_HEADS * LANES
    v_w = MLA_HEADS * MLA_V
    full = lambda a: _resident(a.shape)
    tab = pl.BlockSpec((tm, LANES), lambda i: (i % spt, 0))
    gq2, gkv2 = gq.reshape(1, -1), gkv.reshape(1, -1)
    return pl.pallas_call(
        _mla_up_kernel,
        grid=(n // tm,),
        in_specs=[pl.BlockSpec((tm, cw), lambda i: (i, 0)), full(gq2), full(gkv2), full(wq), full(wk),
                  full(wv), tab, tab, tab, tab, tab, tab],
        out_specs=[pl.BlockSpec((tm, qk_w), lambda i: (i, 0)),
                   pl.BlockSpec((tm, qk_w), lambda i: (i, 0)),
                   pl.BlockSpec((tm, v_w), lambda i: (i, 0))],
        out_shape=[jax.ShapeDtypeStruct((n, qk_w), BF16), jax.ShapeDtypeStruct((n, qk_w), BF16),
                   jax.ShapeDtypeStruct((n, v_w), BF16)],
        compiler_params=_params("parallel"),
        name="mla_up_proj",
    )(c, gq2, gkv2, wq, wk, wv, *tq, *tk)


SUM_LANE = (HALF, 0)


def _with_sum_column(v):
    lane = lax.broadcasted_iota(jnp.int32, (1, LANES), 1)
    unit = lambda at: jnp.where(lane == at, 1.0, 0.0).astype(v.dtype)
    low = lane < HALF
    return [jnp.where(low, v, unit(SUM_LANE[0])), jnp.where(low, unit(SUM_LANE[1]), v)]


def _flash_finish(acc_ref, first=0):
    outs = [acc_ref[first + hh] / acc_ref[first + hh][:, SUM_LANE[hh]:SUM_LANE[hh] + 1] for hh in range(2)]
    return jnp.where(_lane_lo(), outs[0], outs[1])


def _row_max(s):
    mx = s[:, :LANES]
    for c0 in range(LANES, s.shape[1], LANES):
        mx = jnp.maximum(mx, s[:, c0:c0 + LANES])
    return jnp.max(mx, axis=-1, keepdims=True)


def _flash_init(tq, heads=2):
    return (jnp.full((tq, LANES), NEG, F32),) * heads


def _flash_pair(ss, v, m_old, acc_ref):
    hs = range(len(ss))
    vs = _with_sum_column(v)
    cols = [slice(c0, c0 + LANES) for c0 in range(0, ss[0].shape[1], LANES)]
    ms = [jnp.maximum(m_old[hh], _row_max(ss[hh])) for hh in hs]
    ps = [jnp.concatenate([jnp.exp2(ss[hh][:, c] - ms[hh]) for c in cols], axis=1).astype(BF16) for hh in hs]
    al = [jnp.exp2(m_old[hh] - ms[hh]) for hh in hs]
    for hh in hs:
        acc_ref[hh] = al[hh] * acc_ref[hh] + _dot(ps[hh], vs[hh % 2])
    return tuple(ms)


SOFTMAX_ROWS = 32


def _flash_scratch(tq, tk):
    return [pltpu.VMEM((2, tq, tk), F32), pltpu.VMEM((2, tq, tk), BF16), pltpu.VMEM((2, tq, LANES), F32),
            pltpu.VMEM((2, tq, LANES), F32)]


def _flash_reset(m_ref, acc_ref):
    m_ref[...] = jnp.full(m_ref.shape, NEG, F32)
    acc_ref[...] = jnp.zeros(acc_ref.shape, F32)


def _flash_update(s_ref, p_ref, m_ref, acc_ref, v):
    _, tq, tk = s_ref.shape
    vs = _with_sum_column(v)
    for c in range(tq // SOFTMAX_ROWS):
        rows = slice(c * SOFTMAX_ROWS, (c + 1) * SOFTMAX_ROWS)
        for hh in range(2):
            s = s_ref[hh, rows, :]
            m_old = m_ref[hh, rows, :]
            m_new = jnp.maximum(m_old, _row_max(s))
            m_ref[hh, rows, :] = m_new
            acc_ref[hh, rows, :] = jnp.exp2(m_old - m_new) * acc_ref[hh, rows, :]
            for c0 in range(0, tk, LANES):
                p_ref[hh, rows, c0:c0 + LANES] = jnp.exp2(s[:, c0:c0 + LANES] - m_new).astype(BF16)
    for hh in range(2):
        acc_ref[hh] += _dot(p_ref[hh], vs[hh])


def _tile_rel(tq, tk):
    return (lax.broadcasted_iota(jnp.int32, (tq, tk), 0) - lax.broadcasted_iota(jnp.int32, (tq, tk), 1))


def _mla_attn_kernel(q_ref, k_ref, v_ref, o_ref, s_ref, p_ref, m_ref, acc_ref, *, tq, tk):
    r = tq // tk
    rel = _tile_rel(tq, tk)
    sls = [slice(hh * LANES, (hh + 1) * LANES) for hh in range(2)]
    for i in range(q_ref.shape[1] // tq):
        rows = slice(i * tq, (i + 1) * tq)

        def tile(j, diag, rows=rows):
            start = pl.multiple_of(j * tk, tk)
            for hh in range(2):
                s = _dot_t(q_ref[0, rows, sls[hh]], k_ref[0, pl.ds(start, tk), sls[hh]])
                if diag is not None:
                    s = jnp.where(rel >= diag * tk, s, NEG)
                s_ref[hh] = s
            _flash_update(s_ref, p_ref, m_ref, acc_ref, v_ref[0, pl.ds(start, tk), :])

        _flash_reset(m_ref, acc_ref)
        for d in range(r):
            tile(i * r + d, d)
        for j in range(i * r):
            tile(j, None)
        o_ref[0, rows, :] = _flash_finish(acc_ref).astype(BF16)


def _mla_attention(q, k, v, tq=512, tk=512):
    batch, seq, _ = q.shape
    tq, tk = min(tq, seq), min(tk, seq)
    return pl.pallas_call(
        functools.partial(_mla_attn_kernel, tq=tq, tk=tk),
        grid=(batch, MLA_HEADS // 2),
        in_specs=[pl.BlockSpec((1, seq, 2 * LANES), lambda b, m: (b, 0, m)),
                  pl.BlockSpec((1, seq, 2 * LANES), lambda b, m: (b, 0, m)),
                  pl.BlockSpec((1, seq, LANES), lambda b, m: (b, 0, m))],
        out_specs=pl.BlockSpec((1, seq, LANES), lambda b, m: (b, 0, m)),
        out_shape=jax.ShapeDtypeStruct(v.shape, BF16),
        scratch_shapes=_flash_scratch(tq, tk),
        compiler_params=_params("parallel", "parallel"),
        name="mla_attention",
    )(q, k, v)


def _order_key(x):
    bits = lax.bitcast_convert_type(x, jnp.int32)
    return bits ^ (lax.shift_right_arithmetic(bits, 31) & 0x7FFFFFFF)


def _dsa_kernel(slope_ref, pos_ref, qi_ref, ki_ref, wi_ref, q_ref, k_ref, v_ref, o_ref, key_ref, bias_ref,
                byte_ref, acc_ref, *, t, topk):
    i = pl.program_id(1)
    lo = _lane_lo()
    row = lax.broadcasted_iota(jnp.int32, (t, t), 0)
    col = lax.broadcasted_iota(jnp.int32, (t, t), 1)
    rel = row - col

    def _select():
        wi_t = jnp.transpose(wi_ref[0] * (IDX_HEADS ** -0.5))
        rel_t = col - row
        pack = 16

        def score_tile(j, carry):
            start = pl.multiple_of(j * t, t)
            ki = ki_ref[0, pl.ds(start, t), :]
            sc = jnp.zeros((t, t), F32)
            for p in range(IDX_HEADS // 2):
                qp = qi_ref[0, :, p * LANES:(p + 1) * LANES]
                for hh in range(2):
                    h = 2 * p + hh
                    qm = jnp.where(lo if hh == 0 else jnp.logical_not(lo), qp, jnp.zeros_like(qp))
                    dots = _dot_t(ki, qm)
                    sc = sc + wi_t[h:h + 1, :] * jnp.maximum(dots, 0.0)
            sc = sc + 0.0
            sc = jnp.where(rel_t >= (j - i) * t, sc, NEG)
            key_ref[j] = _order_key(sc)
            return carry

        lax.fori_loop(0, i + 1, score_tile, 0)

        one_b, zero_b = jnp.ones((), BF16), jnp.zeros((), BF16)

        def count_ge(cand):
            cand_b = cand.astype(BF16)

            def body(j, part):
                ge = jnp.where(byte_ref[j] >= cand_b, one_b, zero_b)
                slabs = [ge[r0:r0 + pack] for r0 in range(0, t, pack)]
                while len(slabs) > 1:
                    slabs = [a + b for a, b in zip(slabs[0::2], slabs[1::2])]
                return part + slabs[0]
            part = lax.fori_loop(0, i + 1, body, jnp.zeros((pack, t), BF16))
            return jnp.sum(part.astype(F32), axis=0, keepdims=True)

        prefix = jnp.zeros((1, t), jnp.int32)
        above = jnp.zeros((1, t), F32)
        for stage in range(4):
            shift = 24 - 8 * stage

            def fill(j, carry, stage=stage, shift=shift, prefix=prefix):
                key = key_ref[j]
                if stage == 0:
                    byte = lax.shift_right_arithmetic(key, shift) + 128
                else:
                    match = lax.shift_right_arithmetic(key, shift + 8) == prefix
                    byte = jnp.where(match, lax.shift_right_logical(key, shift) & 255, -1)
                byte_ref[j] = byte.astype(F32).astype(BF16)
                return carry

            lax.fori_loop(0, i + 1, fill, 0)
            val = jnp.zeros((1, t), F32)
            rej = jnp.zeros((1, t), F32)
            for bit in (128, 64, 32, 16, 8, 4, 2, 1):
                cand = val + float(bit)
                cnt = count_ge(cand)
                ok = above + cnt >= topk
                val = jnp.where(ok, cand, val)
                rej = jnp.where(ok, rej, cnt)
            above = above + rej
            byte_i = val.astype(jnp.int32) - (128 if stage == 0 else 0)
            prefix = byte_i if stage == 0 else lax.shift_left(prefix, 8) | byte_i
        thr = prefix
        need = topk - above
        before = jnp.where(row > col, 1.0, 0.0).astype(BF16)

        def select_tile(j, seen):
            key = key_ref[j]
            eq = key == thr
            eq_b = jnp.where(eq, 1.0, 0.0).astype(BF16)
            rank = seen + _dot(before, eq_b)
            sel = jnp.logical_or(key > thr, jnp.logical_and(eq, rank < need))
            sel = jnp.logical_and(sel, rel_t >= (j - i) * t)
            bias_ref[j] = jnp.transpose(jnp.where(sel, 0.0, NEG))
            return seen + jnp.sum(eq_b.astype(F32), axis=0, keepdims=True)

        lax.fori_loop(0, i + 1, select_tile, jnp.zeros((1, t), F32))

    _select()
    rep = DSA_HEADS // DSA_KV_HEADS
    for g in range(DSA_KV_HEADS):
        kv_lanes = slice(g * LANES, (g + 1) * LANES)
        qs = []
        for h in range(g * rep, (g + 1) * rep):
            qp = q_ref[0, :, (h // 2) * LANES:(h // 2 + 1) * LANES]
            extra = slope_ref[h:h + 1, :]
            qs.append(jnp.where(lo, qp, extra) if h % 2 == 0 else jnp.where(lo, extra, qp))

        def tile(j, carry, qs=qs, kv_lanes=kv_lanes):
            start = pl.multiple_of(j * t, t)
            k = k_ref[0, pl.ds(start, t), kv_lanes]
            v = v_ref[0, pl.ds(start, t), kv_lanes]
            ks = [jnp.where(lo, k, pos_ref[0, pl.ds(start, t), :]), jnp.where(lo, pos_ref[1, pl.ds(start, t), :], k)]
            bias = bias_ref[j]
            ss = [_dot_t(qs[n], ks[n % 2]) + bias for n in range(rep)]
            return _flash_pair(ss, v, carry, acc_ref)

        acc_ref[...] = jnp.zeros(acc_ref.shape, F32)
        carry = tile(i, _flash_init(t, rep))
        lax.fori_loop(0, i, tile, carry)
        for n in range(0, rep, 2):
            pair = (g * rep + n) // 2
            o_ref[0, :, pair * LANES:(pair + 1) * LANES] = _flash_finish(acc_ref, n).astype(BF16)


ALIBI_PIECES = 6


def _dsa_alibi_tables(seq):
    bf = lambda a: np.asarray(np.asarray(a, np.float32).astype(jnp.bfloat16), np.float32)
    slopes = np.asarray(_alibi_slopes(DSA_HEADS), np.float64) * LOG2E
    slopes = slopes.astype(np.float32)
    s1 = bf(slopes)
    s2 = bf(slopes - s1)
    s3 = bf(slopes - s1 - s2)
    slope_tab = np.zeros((DSA_HEADS, LANES), np.float32)
    for h in range(DSA_HEADS):
        off = HALF if h % 2 == 0 else 0
        slope_tab[h, off:off + ALIBI_PIECES] = [s1[h], s2[h], s3[h], s1[h], s2[h], s3[h]]
    pos = np.arange(seq)
    hi, lw = (pos // 256 * 256).astype(np.float32), (pos % 256).astype(np.float32)
    pieces = np.stack([hi, hi, hi, lw, lw, lw], axis=1)
    pos_tab = np.zeros((2, seq, LANES), np.float32)
    pos_tab[0, :, HALF:HALF + ALIBI_PIECES] = pieces
    pos_tab[1, :, :ALIBI_PIECES] = pieces
    return jnp.asarray(slope_tab, BF16), jnp.asarray(pos_tab, BF16)


def _dsa_attention(proj, wi, batch, seq, t=512):
    t = min(t, seq)
    nt = seq // t
    topk = min(IDX_TOPK, seq // 4)
    qw = DSA_HEADS * DSA_HEAD_DIM
    q0 = 0
    k0 = qw // LANES
    v0 = k0 + DSA_KV_HEADS
    qi0 = (v0 + DSA_KV_HEADS) * LANES // (IDX_HEADS * IDX_DIM)
    ki0 = v0 + DSA_KV_HEADS + IDX_HEADS * IDX_DIM // LANES
    kvw = DSA_KV_HEADS * LANES
    slope_tab, pos_tab = _dsa_alibi_tables(seq)
    return pl.pallas_call(
        functools.partial(_dsa_kernel, t=t, topk=topk),
        grid=(batch, nt),
        in_specs=[pl.BlockSpec(slope_tab.shape, lambda b, i: (0, 0)),
                  pl.BlockSpec(pos_tab.shape, lambda b, i: (0, 0, 0)),
                  pl.BlockSpec((1, t, IDX_HEADS * IDX_DIM), lambda b, i: (b, i, qi0)),
                  pl.BlockSpec((1, seq, LANES), lambda b, i: (b, 0, ki0)),
                  pl.BlockSpec((1, t, LANES), lambda b, i: (b, i, 0)),
                  pl.BlockSpec((1, t, qw), lambda b, i: (b, i, q0)),
                  pl.BlockSpec((1, seq, kvw), lambda b, i: (b, 0, k0 * LANES // kvw)),
                  pl.BlockSpec((1, seq, kvw), lambda b, i: (b, 0, v0 * LANES // kvw))],
        out_specs=pl.BlockSpec((1, t, qw), lambda b, i: (b, i, 0)),
        out_shape=jax.ShapeDtypeStruct((batch, seq, qw), BF16),
        scratch_shapes=[pltpu.VMEM((nt, t, t), jnp.int32), pltpu.VMEM((nt, t, t), F32),
                        pltpu.VMEM((nt, t, t), BF16),
                        pltpu.VMEM((DSA_HEADS // DSA_KV_HEADS, t, LANES), F32)],
        compiler_params=_params("parallel", "arbitrary"),
        name="dsa_attention",
    )(slope_tab, pos_tab, proj, proj, wi, proj, proj, proj)


def _sb_kernel(q_ref, k_ref, v_ref, o_ref, later_ref, acc_ref, *, tq, tk):
    r = tq // tk
    lo = _lane_lo()
    rel = _tile_rel(tq, tk)
    blk = min(tk, 2 * LANES)
    suffix = jnp.where(_tile_rel(blk, blk) >= 0, 1.0, 0.0).astype(BF16)
    cols = [slice(c0, c0 + LANES) for c0 in range(0, tk, LANES)]
    hs = range(2)
    for i in range(q_ref.shape[1] // tq):
        rows = slice(i * tq, (i + 1) * tq)
        qp = q_ref[0, rows, :]
        qs = [jnp.where(lo if hh == 0 else jnp.logical_not(lo), qp, jnp.zeros_like(qp)) for hh in hs]

        def tile(j, diag, qs=qs):
            start = pl.multiple_of(j * tk, tk)
            k = k_ref[0, pl.ds(start, tk), :]
            v = v_ref[0, pl.ds(start, tk), :]
            zs = [_dot_t(qs[hh], k) for hh in hs]
            lgs = []
            for z in zs:
                nz = -z
                lgs.append(jnp.minimum(nz, 0.0) - jnp.log2(1.0 + jnp.exp2(jnp.minimum(z, nz))))
            if diag is not None:
                strict = rel > diag * tk
                lgs = [jnp.where(strict, lg, 0.0) for lg in lgs]
            lbs = [lg.astype(BF16) for lg in lgs]
            runs, tots = [], []
            for hh in hs:
                parts, after = [], later_ref[hh]
                for c0 in reversed(range(0, tk, blk)):
                    part = _dot(lbs[hh][:, c0:c0 + blk], suffix) + jnp.concatenate([after] * (blk // LANES), axis=1)
                    after = jnp.broadcast_to(part[:, 0:1], (tq, LANES))
                    parts.insert(0, part)
                runs.append(jnp.concatenate(parts, axis=1))
                tots.append(after)
            a_s = [jnp.exp2(zs[hh] + runs[hh]) for hh in hs]
            if diag is not None:
                a_s = [jnp.where(strict, a, 0.0) for a in a_s]
            for hh in hs:
                later_ref[hh] = tots[hh]
            for hh in hs:
                acc_ref[hh] += _dot(a_s[hh].astype(BF16), v)

        later_ref[...] = jnp.zeros(later_ref.shape, F32)
        acc_ref[...] = jnp.zeros(acc_ref.shape, F32)
        for d in reversed(range(r)):
            tile(i * r + d, d)
        for j in reversed(range(i * r)):
            tile(j, None)
        o_ref[0, rows, :] = jnp.where(lo, acc_ref[0], acc_ref[1]).astype(BF16)


def _sb_attention(qkv, batch, seq, tq=512, tk=512):
    tq, tk = min(tq, seq), min(tk, seq)
    w = SB_HEADS * SB_HEAD_DIM
    pairs = SB_HEADS // 2
    return pl.pallas_call(
        functools.partial(_sb_kernel, tq=tq, tk=tk),
        grid=(batch, pairs),
        in_specs=[pl.BlockSpec((1, seq, LANES), lambda b, m: (b, 0, m)),
                  pl.BlockSpec((1, seq, LANES), lambda b, m: (b, 0, pairs + m)),
                  pl.BlockSpec((1, seq, LANES), lambda b, m: (b, 0, 2 * pairs + m))],
        out_specs=pl.BlockSpec((1, seq, LANES), lambda b, m: (b, 0, m)),
        out_shape=jax.ShapeDtypeStruct((batch, seq, w), BF16),
        scratch_shapes=[pltpu.VMEM((2, tq, LANES), F32), pltpu.VMEM((2, tq, LANES), F32)],
        compiler_params=_params("parallel", "parallel"),
        name="sb_attention",
    )(qkv, qkv, qkv)


def _dup_heads(w, n_heads, dim):
    d = w.shape[0]
    w = w.reshape(d, n_heads, 1, dim)
    return jnp.broadcast_to(w, (d, n_heads, 2, dim)).reshape(d, n_heads * 2 * dim)


def _pad_heads(w, n_heads, dim, to):
    d = w.shape[0]
    w = w.reshape(d, n_heads, dim)
    return jnp.pad(w, ((0, 0), (0, 0), (0, to - dim))).reshape(d, n_heads * to)


def _swa_w_in(w):
    qw = SWA_HEADS * SWA_HEAD_DIM
    kw = SWA_KV_HEADS * SWA_HEAD_DIM
    q, k, v = w[:, :qw], w[:, qw:qw + kw], w[:, qw + kw:]
    q = q * (LOG2E * SWA_HEAD_DIM ** -0.5)
    return jnp.concatenate([q, _dup_heads(k, SWA_KV_HEADS, SWA_HEAD_DIM),
                            _dup_heads(v, SWA_KV_HEADS, SWA_HEAD_DIM)], axis=1).astype(BF16)


def _mla_w_in(w):
    d = w.shape[0]
    lat = MLA_Q_LORA + MLA_KV_LORA
    z = lambda c: jnp.zeros((d, c), w.dtype)
    return jnp.concatenate([w[:, :lat], z(MLA_NOPE), w[:, lat:], z(LANES - MLA_NOPE - MLA_ROPE)],
                           axis=1).astype(BF16)


def _sb_w_in(w):
    qw = SB_HEADS * SB_HEAD_DIM
    return jnp.concatenate([w[:, :qw] * (LOG2E * SB_HEAD_DIM ** -0.5), w[:, qw:]], axis=1).astype(BF16)


def _dsa_w_in(w):
    qw = DSA_HEADS * DSA_HEAD_DIM
    kw = DSA_KV_HEADS * DSA_HEAD_DIM
    iw = IDX_HEADS * IDX_DIM
    c = np.cumsum([qw, kw, kw, iw, IDX_DIM]).tolist()
    q, k, v, qi, ki, wi = (w[:, :c[0]], w[:, c[0]:c[1]], w[:, c[1]:c[2]], w[:, c[2]:c[3]],
                           w[:, c[3]:c[4]], w[:, c[4]:])
    q = q * (LOG2E * DSA_HEAD_DIM ** -0.5)
    qi = qi * (IDX_DIM ** -0.5)
    main = jnp.concatenate([q, _dup_heads(k, DSA_KV_HEADS, DSA_HEAD_DIM),
                            _dup_heads(v, DSA_KV_HEADS, DSA_HEAD_DIM), qi, _dup_heads(ki, 1, IDX_DIM)],
                           axis=1).astype(BF16)
    wi = jnp.pad(wi, ((0, 0), (0, LANES - IDX_HEADS))).astype(BF16)
    return main, wi


def kernel(x, ffn1_w_gu, ffn1_w_down, ffn2_w_gu, ffn2_w_down, ln_g, ln_b, a_w_in, a_sinks, a_w_o,
           b_w_in, b_q_norm, b_w_uq, b_kv_norm, b_w_ukv, b_w_o, c_w_in, c_w_o, d_w_in, d_w_o):
    batch, seq, d = x.shape
    n = batch * seq
    x = x.reshape(n, d)
    ffn1_gu, ffn1_down = ffn1_w_gu.astype(BF16), ffn1_w_down.astype(BF16)
    ffn2_gu, ffn2_down = ffn2_w_gu.astype(BF16), ffn2_w_down.astype(BF16)
    for i in range(DEPTH):
        mixer, j = i % N_MIXERS, i // N_MIXERS

        def ffn1(projs):
            return _ffn_proj(x, ffn1_gu, ffn1_down, i, ln_g[i, 0], ln_b[i, 0], projs)

        if mixer == 0:
            x, qkv = ffn1([(_swa_w_in(a_w_in[j]), BF16)])
            o = _swa_attention(qkv.reshape(batch, seq, -1), a_sinks[j] * LOG2E, batch, seq)
            w_o = a_w_o[j]
        elif mixer == 1:
            x, c = ffn1([(_mla_w_in(b_w_in[j]), F32)])
            hd = MLA_NOPE + MLA_ROPE
            wq = _pad_heads(b_w_uq[j], MLA_HEADS, hd, LANES).astype(BF16)
            wkv = b_w_ukv[j].reshape(MLA_KV_LORA, MLA_HEADS, MLA_NOPE + MLA_V)
            wk = _pad_heads(wkv[:, :, :MLA_NOPE].reshape(MLA_KV_LORA, -1), MLA_HEADS, MLA_NOPE,
                            LANES).astype(BF16)
            wv = wkv[:, :, MLA_NOPE:].reshape(MLA_KV_LORA, -1).astype(BF16)
            q, k, v = _mla_up(c, b_q_norm[j], b_kv_norm[j], wq, wk, wv, seq)
            o = _mla_attention(q.reshape(batch, seq, -1), k.reshape(batch, seq, -1),
                               v.reshape(batch, seq, -1))
            w_o = b_w_o[j]
        elif mixer == 2:
            w_main, w_wi = _dsa_w_in(c_w_in[j])
            x, proj, wi = ffn1([(w_main, BF16), (w_wi, F32)])
            o = _dsa_attention(proj.reshape(batch, seq, -1), wi.reshape(batch, seq, -1), batch, seq)
            w_o = c_w_o[j]
        else:
            x, qkv = ffn1([(_sb_w_in(d_w_in[j]), BF16)])
            o = _sb_attention(qkv.reshape(batch, seq, -1), batch, seq)
            w_o = d_w_o[j]
        x = _proj_ffn(o.reshape(n, -1), w_o.astype(BF16), x, ln_g[i, 1], ln_b[i, 1], ffn2_gu, ffn2_down, i,
                      ln_g[i, 2], ln_b[i, 2])
    return x.reshape(batch, seq, d)
```

```python
import functools
import math

import numpy as np
import jax
import jax.numpy as jnp
from jax import lax
from jax.experimental import pallas as pl
from jax.experimental.pallas import tpu as pltpu

DEPTH = 4
N_MIXERS = 4
LN_EPS = 1e-5
RMS_EPS = 1e-6
NEG = -1e30

SWA_HEADS, SWA_KV_HEADS, SWA_HEAD_DIM, SWA_BLOCK = 16, 4, 64, 128
MLA_HEADS, MLA_Q_LORA, MLA_KV_LORA, MLA_NOPE, MLA_ROPE, MLA_V = 16, 768, 256, 64, 32, 64
ROPE_THETA = 10000.0
DSA_HEADS, DSA_KV_HEADS, DSA_HEAD_DIM = 16, 4, 64
IDX_HEADS, IDX_DIM, IDX_TOPK = 8, 64, 256
SB_HEADS, SB_HEAD_DIM = 16, 64

ALPHA = (2.0 * DEPTH) ** 0.25
LOG2E = math.log2(math.e)

LANES = 128
HALF = 64
VMEM_LIMIT = 56 * 1024 * 1024
BF16 = jnp.bfloat16
F32 = jnp.float32


def _params(*sem):
    return pltpu.CompilerParams(dimension_semantics=sem, vmem_limit_bytes=VMEM_LIMIT)


def _dot(a, b):
    return jnp.dot(a, b, preferred_element_type=F32)


def _dot_t(a, b):
    return lax.dot_general(a, b, (((1,), (1,)), ((), ())), preferred_element_type=F32)


def _alibi_slopes(n_heads):
    return [2.0 ** (-8.0 * (i + 1) / n_heads) for i in range(n_heads)]


def _layer_norm(y, g, b):
    mu = jnp.mean(y, axis=-1, keepdims=True)
    yc = y - mu
    var = jnp.mean(yc * yc, axis=-1, keepdims=True)
    return yc * lax.rsqrt(var + LN_EPS) * g + b


def _lane_lo():
    return lax.broadcasted_iota(jnp.int32, (1, LANES), 1) < HALF


def _resident(shape):
    return pl.BlockSpec(shape, lambda i: (0,) * len(shape), pipeline_mode=pl.Buffered(1))


def _layer(stack, layer):
    return pl.BlockSpec((None,) + stack.shape[1:], lambda i: (layer, 0, 0), pipeline_mode=pl.Buffered(1))


def _rows(tm, width):
    return pl.BlockSpec((tm, width), lambda i: (i, 0))


def _ffn_ln_value(x, wgu_ref, wd_ref, g_ref, b_ref):
    f = wd_ref.shape[0]
    xb = x.astype(BF16)
    h = _dot(xb, wgu_ref[:, :f])
    u = _dot(xb, wgu_ref[:, f:])
    a = h * (1.0 / (1.0 + jnp.exp(-h))) * u
    y = ALPHA * x + 0.5 * _dot(a.astype(BF16), wd_ref[...])
    return _layer_norm(y, g_ref[...], b_ref[...])


def _ffn_proj_kernel(x_ref, wgu_ref, wd_ref, g_ref, b_ref, *refs, n_proj):
    w_refs, o_ref, p_refs = refs[:n_proj], refs[n_proj], refs[n_proj + 1:]
    y = _ffn_ln_value(x_ref[...], wgu_ref, wd_ref, g_ref, b_ref)
    o_ref[...] = y
    yb = y.astype(BF16)
    for w_ref, p_ref in zip(w_refs, p_refs):
        p_ref[...] = _dot(yb, w_ref[...]).astype(p_ref.dtype)


def _ffn_proj(x, w_gu, w_down, layer, g, b, projs, tm=512):
    n, d = x.shape
    tm = min(tm, n)
    ws = [w for w, _ in projs]
    return pl.pallas_call(
        functools.partial(_ffn_proj_kernel, n_proj=len(projs)),
        grid=(n // tm,),
        in_specs=[_rows(tm, d), _layer(w_gu, layer), _layer(w_down, layer), _resident((1, d)),
                  _resident((1, d))] + [_resident(w.shape) for w in ws],
        out_specs=[_rows(tm, d)] + [_rows(tm, w.shape[1]) for w in ws],
        out_shape=[jax.ShapeDtypeStruct((n, d), F32)]
        + [jax.ShapeDtypeStruct((n, w.shape[1]), dt) for w, dt in projs],
        compiler_params=_params("parallel"),
        name="ffn_in_proj",
    )(x, w_gu, w_down, g.reshape(1, d), b.reshape(1, d), *ws)


def _proj_ffn_kernel(o_ref, wo_ref, x_ref, g1_ref, b1_ref, wgu_ref, wd_ref, g2_ref, b2_ref, out_ref):
    y = ALPHA * x_ref[...] + _dot(o_ref[...], wo_ref[...])
    x2 = _layer_norm(y, g1_ref[...], b1_ref[...])
    out_ref[...] = _ffn_ln_value(x2, wgu_ref, wd_ref, g2_ref, b2_ref)


def _proj_ffn(o, w_o, x, g1, b1, w_gu, w_down, layer, g2, b2, tm=512):
    n, k = o.shape
    d = w_o.shape[1]
    tm = min(tm, n)
    vec = lambda a: a.reshape(1, d)
    return pl.pallas_call(
        _proj_ffn_kernel,
        grid=(n // tm,),
        in_specs=[_rows(tm, k), _resident(w_o.shape), _rows(tm, d), _resident((1, d)), _resident((1, d)),
                  _layer(w_gu, layer), _layer(w_down, layer), _resident((1, d)), _resident((1, d))],
        out_specs=_rows(tm, d),
        out_shape=jax.ShapeDtypeStruct((n, d), F32),
        compiler_params=_params("parallel"),
        name="out_proj_ffn",
    )(o, w_o, x, vec(g1), vec(b1), w_gu, w_down, vec(g2), vec(b2))


def _swa_kernel(sink_ref, q_ref, kc_ref, kp_ref, vc_ref, vp_ref, o_ref, *, tq):
    i = pl.program_id(1)
    t = SWA_BLOCK
    rep = SWA_HEADS // SWA_KV_HEADS
    kvs = range(SWA_KV_HEADS)
    lo = _lane_lo()
    rel = _tile_rel(t, 2 * t) + t
    col = lax.broadcasted_iota(jnp.int32, (t, 2 * t), 1)
    dist = rel.astype(F32)
    in_band = jnp.logical_and(rel >= 0, rel < SWA_BLOCK)
    slopes = [s * LOG2E for s in _alibi_slopes(SWA_HEADS)]
    lane = lambda g: slice(g * LANES, (g + 1) * LANES)
    for qb in range(tq // t):
        r0 = qb * t
        if qb == 0:
            ks = [jnp.concatenate([kp_ref[0, :, lane(g)], kc_ref[0, 0:t, lane(g)]], axis=0) for g in kvs]
            vs = [jnp.concatenate([vp_ref[0, :, lane(g)], vc_ref[0, 0:t, lane(g)]], axis=0) for g in kvs]
            valid = jnp.logical_and(in_band, col >= jnp.where(i == 0, t, 0))
        else:
            ks = [kc_ref[0, r0 - t:r0 + t, lane(g)] for g in kvs]
            vs = [vc_ref[0, r0 - t:r0 + t, lane(g)] for g in kvs]
            valid = in_band
        qst = []
        for g in kvs:
            parts = []
            for r in range(rep):
                h = g * rep + r
                qp = q_ref[0, r0:r0 + t, lane(h // 2)]
                parts.append(jnp.where(lo if h % 2 == 0 else jnp.logical_not(lo), qp, jnp.zeros_like(qp)))
            qst.append(jnp.concatenate(parts, axis=0))
        ss = [_dot_t(qst[g], ks[g]) for g in kvs]
        ps, dens = [], []
        for g in kvs:
            prow, drow = [], []
            for r in range(rep):
                h = g * rep + r
                s = jnp.where(valid, ss[g][r * t:(r + 1) * t] - slopes[h] * dist, NEG)
                sink = sink_ref[h]
                mx = jnp.maximum(jnp.max(s, axis=-1, keepdims=True), sink)
                p = jnp.exp2(s - mx)
                drow.append(jnp.sum(p, axis=-1, keepdims=True) + jnp.exp2(sink - mx))
                prow.append(p.astype(BF16))
            ps.append(jnp.concatenate(prow, axis=0))
            dens.append(drow)
        outs = [_dot(ps[g], vs[g]) for g in kvs]
        for g in kvs:
            for r in range(0, rep, 2):
                even = outs[g][r * t:(r + 1) * t] / dens[g][r]
                odd = outs[g][(r + 1) * t:(r + 2) * t] / dens[g][r + 1]
                o_ref[0, r0:r0 + t, lane((g * rep + r) // 2)] = jnp.where(lo, even, odd).astype(BF16)


def _swa_attention(qkv, sinks, batch, seq, tq=512):
    t = SWA_BLOCK
    tq = min(tq, seq)
    sub = tq // t
    qw = SWA_HEADS * SWA_HEAD_DIM
    kw = SWA_KV_HEADS * LANES
    qblk, kblk, vblk = 0, qw // kw, qw // kw + 1
    prev = lambda i: jnp.maximum(i * sub - 1, 0)
    return pl.pallas_call(
        functools.partial(_swa_kernel, tq=tq),
        grid=(batch, seq // tq),
        in_specs=[pl.BlockSpec(memory_space=pltpu.SMEM),
                  pl.BlockSpec((1, tq, qw), lambda b, i: (b, i, qblk)),
                  pl.BlockSpec((1, tq, kw), lambda b, i: (b, i, kblk)),
                  pl.BlockSpec((1, t, kw), lambda b, i: (b, prev(i), kblk)),
                  pl.BlockSpec((1, tq, kw), lambda b, i: (b, i, vblk)),
                  pl.BlockSpec((1, t, kw), lambda b, i: (b, prev(i), vblk))],
        out_specs=pl.BlockSpec((1, tq, qw), lambda b, i: (b, i, 0)),
        out_shape=jax.ShapeDtypeStruct((batch, seq, qw), BF16),
        compiler_params=_params("parallel", "arbitrary"),
        name="swa_attention",
    )(sinks, qkv, qkv, qkv, qkv, qkv)


def _mla_up_kernel(c_ref, gq_ref, gkv_ref, wq_ref, wk_ref, wv_ref, cq_ref, s1q_ref, s2q_ref,
                   ck_ref, s1k_ref, s2k_ref, q_ref, k_ref, v_ref):
    c = c_ref[...]
    cq = c[:, :MLA_Q_LORA]
    ckv = c[:, MLA_Q_LORA:MLA_Q_LORA + MLA_KV_LORA]
    kr = c[:, MLA_Q_LORA + MLA_KV_LORA:]
    cq = cq * lax.rsqrt(jnp.mean(cq * cq, axis=-1, keepdims=True) + RMS_EPS) * gq_ref[...]
    ckv = ckv * lax.rsqrt(jnp.mean(ckv * ckv, axis=-1, keepdims=True) + RMS_EPS) * gkv_ref[...]
    ckv_b = ckv.astype(BF16)
    q = _dot(cq.astype(BF16), wq_ref[...])
    kn = _dot(ckv_b, wk_ref[...])
    v_ref[...] = _dot(ckv_b, wv_ref[...]).astype(BF16)

    def rot(xh, cc, s1, s2):
        return xh * cc + pltpu.roll(xh, 16, 1) * s1 + pltpu.roll(xh, LANES - 16, 1) * s2

    kr = rot(kr, ck_ref[...], s1k_ref[...], s2k_ref[...])
    cq_t, s1q, s2q = cq_ref[...], s1q_ref[...], s2q_ref[...]
    for h in range(MLA_HEADS):
        sl = slice(h * LANES, (h + 1) * LANES)
        q_ref[:, sl] = rot(q[:, sl], cq_t, s1q, s2q).astype(BF16)
        k_ref[:, sl] = (kn[:, sl] + kr).astype(BF16)


def _mla_rope_tables(seq, scale):
    half = MLA_ROPE // 2
    inv = ROPE_THETA ** (-jnp.arange(0, MLA_ROPE, 2, dtype=F32) / MLA_ROPE)
    ang = jnp.arange(seq, dtype=F32)[:, None] * inv[None, :]
    cos, sin = jnp.cos(ang), jnp.sin(ang)
    z = lambda w: jnp.zeros((seq, w), F32)
    one = jnp.ones((seq, MLA_NOPE), F32)
    cc = jnp.concatenate([one, cos, cos, z(LANES - MLA_NOPE - MLA_ROPE)], axis=1) * scale
    s1 = jnp.concatenate([z(MLA_NOPE + half), sin, z(LANES - MLA_NOPE - MLA_ROPE)], axis=1) * scale
    s2 = jnp.concatenate([z(MLA_NOPE), -sin, z(LANES - MLA_NOPE - half)], axis=1) * scale
    return cc, s1, s2


def _mla_up(c, gq, gkv, wq, wk, wv, seq, tm=512):
    n, cw = c.shape
    tm = min(tm, seq)
    spt = seq // tm
    scale = LOG2E * (MLA_NOPE + MLA_ROPE) ** -0.5
    tq = _mla_rope_tables(seq, scale)
    tk = _mla_rope_tables(seq, 1.0)
    qk_w = MLA_HEADS * LANES
    v_w = MLA_HEADS * MLA_V
    full = lambda a: _resident(a.shape)
    tab = pl.BlockSpec((tm, LANES), lambda i: (i % spt, 0))
    gq2, gkv2 = gq.reshape(1, -1), gkv.reshape(1, -1)
    return pl.pallas_call(
        _mla_up_kernel,
        grid=(n // tm,),
        in_specs=[pl.BlockSpec((tm, cw), lambda i: (i, 0)), full(gq2), full(gkv2), full(wq), full(wk),
                  full(wv), tab, tab, tab, tab, tab, tab],
        out_specs=[pl.BlockSpec((tm, qk_w), lambda i: (i, 0)),
                   pl.BlockSpec((tm, qk_w), lambda i: (i, 0)),
                   pl.BlockSpec((tm, v_w), lambda i: (i, 0))],
        out_shape=[jax.ShapeDtypeStruct((n, qk_w), BF16), jax.ShapeDtypeStruct((n, qk_w), BF16),
                   jax.ShapeDtypeStruct((n, v_w), BF16)],
        compiler_params=_params("parallel"),
        name="mla_up_proj",
    )(c, gq2, gkv2, wq, wk, wv, *tq, *tk)


SUM_LANE = (HALF, 0)


def _with_sum_column(v):
    lane = lax.broadcasted_iota(jnp.int32, (1, LANES), 1)
    unit = lambda at: jnp.where(lane == at, 1.0, 0.0).astype(v.dtype)
    low = lane < HALF
    return [jnp.where(low, v, unit(SUM_LANE[0])), jnp.where(low, unit(SUM_LANE[1]), v)]


def _flash_finish(acc_ref, first=0):
    outs = [acc_ref[first + hh] / acc_ref[first + hh][:, SUM_LANE[hh]:SUM_LANE[hh] + 1] for hh in range(2)]
    return jnp.where(_lane_lo(), outs[0], outs[1])


def _row_max(s):
    mx = s[:, :LANES]
    for c0 in range(LANES, s.shape[1], LANES):
        mx = jnp.maximum(mx, s[:, c0:c0 + LANES])
    return jnp.max(mx, axis=-1, keepdims=True)


def _flash_init(tq, heads=2):
    return (jnp.full((tq, LANES), NEG, F32),) * heads


def _flash_pair(ss, v, m_old, acc_ref):
    hs = range(len(ss))
    vs = _with_sum_column(v)
    cols = [slice(c0, c0 + LANES) for c0 in range(0, ss[0].shape[1], LANES)]
    ms = [jnp.maximum(m_old[hh], _row_max(ss[hh])) for hh in hs]
    ps = [jnp.concatenate([jnp.exp2(ss[hh][:, c] - ms[hh]) for c in cols], axis=1).astype(BF16) for hh in hs]
    al = [jnp.exp2(m_old[hh] - ms[hh]) for hh in hs]
    for hh in hs:
        acc_ref[hh] = al[hh] * acc_ref[hh] + _dot(ps[hh], vs[hh % 2])
    return tuple(ms)


SOFTMAX_ROWS = 32


def _flash_scratch(tq, tk):
    return [pltpu.VMEM((2, tq, tk), F32), pltpu.VMEM((2, tq, tk), BF16), pltpu.VMEM((2, tq, LANES), F32),
            pltpu.VMEM((2, tq, LANES), F32)]


def _flash_reset(m_ref, acc_ref):
    m_ref[...] = jnp.full(m_ref.shape, NEG, F32)
    acc_ref[...] = jnp.zeros(acc_ref.shape, F32)


def _flash_update(s_ref, p_ref, m_ref, acc_ref, v):
    _, tq, tk = s_ref.shape
    vs = _with_sum_column(v)
    for c in range(tq // SOFTMAX_ROWS):
        rows = slice(c * SOFTMAX_ROWS, (c + 1) * SOFTMAX_ROWS)
        for hh in range(2):
            s = s_ref[hh, rows, :]
            m_old = m_ref[hh, rows, :]
            m_new = jnp.maximum(m_old, _row_max(s))
            m_ref[hh, rows, :] = m_new
            acc_ref[hh, rows, :] = jnp.exp2(m_old - m_new) * acc_ref[hh, rows, :]
            for c0 in range(0, tk, LANES):
                p_ref[hh, rows, c0:c0 + LANES] = jnp.exp2(s[:, c0:c0 + LANES] - m_new).astype(BF16)
    for hh in range(2):
        acc_ref[hh] += _dot(p_ref[hh], vs[hh])


def _tile_rel(tq, tk):
    return (lax.broadcasted_iota(jnp.int32, (tq, tk), 0) - lax.broadcasted_iota(jnp.int32, (tq, tk), 1))


def _mla_attn_kernel(q_ref, k_ref, v_ref, o_ref, s_ref, p_ref, m_ref, acc_ref, *, tq, tk):
    r = tq // tk
    rel = _tile_rel(tq, tk)
    sls = [slice(hh * LANES, (hh + 1) * LANES) for hh in range(2)]
    for i in range(q_ref.shape[1] // tq):
        rows = slice(i * tq, (i + 1) * tq)

        def tile(j, diag, rows=rows):
            start = pl.multiple_of(j * tk, tk)
            for hh in range(2):
                s = _dot_t(q_ref[0, rows, sls[hh]], k_ref[0, pl.ds(start, tk), sls[hh]])
                if diag is not None:
                    s = jnp.where(rel >= diag * tk, s, NEG)
                s_ref[hh] = s
            _flash_update(s_ref, p_ref, m_ref, acc_ref, v_ref[0, pl.ds(start, tk), :])

        _flash_reset(m_ref, acc_ref)
        for d in range(r):
            tile(i * r + d, d)
        for j in range(i * r):
            tile(j, None)
        o_ref[0, rows, :] = _flash_finish(acc_ref).astype(BF16)


def _mla_attention(q, k, v, tq=512, tk=512):
    batch, seq, _ = q.shape
    tq, tk = min(tq, seq), min(tk, seq)
    return pl.pallas_call(
        functools.partial(_mla_attn_kernel, tq=tq, tk=tk),
        grid=(batch, MLA_HEADS // 2),
        in_specs=[pl.BlockSpec((1, seq, 2 * LANES), lambda b, m: (b, 0, m)),
                  pl.BlockSpec((1, seq, 2 * LANES), lambda b, m: (b, 0, m)),
                  pl.BlockSpec((1, seq, LANES), lambda b, m: (b, 0, m))],
        out_specs=pl.BlockSpec((1, seq, LANES), lambda b, m: (b, 0, m)),
        out_shape=jax.ShapeDtypeStruct(v.shape, BF16),
        scratch_shapes=_flash_scratch(tq, tk),
        compiler_params=_params("parallel", "parallel"),
        name="mla_attention",
    )(q, k, v)


def _order_key(x):
    bits = lax.bitcast_convert_type(x, jnp.int32)
    return bits ^ (lax.shift_right_arithmetic(bits, 31) & 0x7FFFFFFF)


def _dsa_kernel(slope_ref, pos_ref, qi_ref, ki_ref, wi_ref, q_ref, k_ref, v_ref, o_ref, key_ref, bias_ref,
                byte_ref, acc_ref, *, t, topk):
    i = pl.program_id(1)
    lo = _lane_lo()
    row = lax.broadcasted_iota(jnp.int32, (t, t), 0)
    col = lax.broadcasted_iota(jnp.int32, (t, t), 1)
    rel = row - col

    def _select():
        wi_t = jnp.transpose(wi_ref[0] * (IDX_HEADS ** -0.5))
        rel_t = col - row
        pack = 16

        def score_tile(j, carry):
            start = pl.multiple_of(j * t, t)
            ki = ki_ref[0, pl.ds(start, t), :]
            sc = jnp.zeros((t, t), F32)
            for p in range(IDX_HEADS // 2):
                qp = qi_ref[0, :, p * LANES:(p + 1) * LANES]
                for hh in range(2):
                    h = 2 * p + hh
                    qm = jnp.where(lo if hh == 0 else jnp.logical_not(lo), qp, jnp.zeros_like(qp))
                    dots = _dot_t(ki, qm)
                    sc = sc + wi_t[h:h + 1, :] * jnp.maximum(dots, 0.0)
            sc = sc + 0.0
            sc = jnp.where(rel_t >= (j - i) * t, sc, NEG)
            key_ref[j] = _order_key(sc)
            return carry

        lax.fori_loop(0, i + 1, score_tile, 0)

        one_b, zero_b = jnp.ones((), BF16), jnp.zeros((), BF16)

        def count_ge(cand):
            cand_b = cand.astype(BF16)

            def body(j, part):
                ge = jnp.where(byte_ref[j] >= cand_b, one_b, zero_b)
                slabs = [ge[r0:r0 + pack] for r0 in range(0, t, pack)]
                while len(slabs) > 1:
                    slabs = [a + b for a, b in zip(slabs[0::2], slabs[1::2])]
                return part + slabs[0]
            part = lax.fori_loop(0, i + 1, body, jnp.zeros((pack, t), BF16))
            return jnp.sum(part.astype(F32), axis=0, keepdims=True)

        prefix = jnp.zeros((1, t), jnp.int32)
        above = jnp.zeros((1, t), F32)
        for stage in range(4):
            shift = 24 - 8 * stage

            def fill(j, carry, stage=stage, shift=shift, prefix=prefix):
                key = key_ref[j]
                if stage == 0:
                    byte = lax.shift_right_arithmetic(key, shift) + 128
                else:
                    match = lax.shift_right_arithmetic(key, shift + 8) == prefix
                    byte = jnp.where(match, lax.shift_right_logical(key, shift) & 255, -1)
                byte_ref[j] = byte.astype(F32).astype(BF16)
                return carry

            lax.fori_loop(0, i + 1, fill, 0)
            val = jnp.zeros((1, t), F32)
            rej = jnp.zeros((1, t), F32)
            for bit in (128, 64, 32, 16, 8, 4, 2, 1):
                cand = val + float(bit)
                cnt = count_ge(cand)
                ok = above + cnt >= topk
                val = jnp.where(ok, cand, val)
                rej = jnp.where(ok, rej, cnt)
            above = above + rej
            byte_i = val.astype(jnp.int32) - (128 if stage == 0 else 0)
            prefix = byte_i if stage == 0 else lax.shift_left(prefix, 8) | byte_i
        thr = prefix
        need = topk - above
        before = jnp.where(row > col, 1.0, 0.0).astype(BF16)

        def select_tile(j, seen):
            key = key_ref[j]
            eq = key == thr
            eq_b = jnp.where(eq, 1.0, 0.0).astype(BF16)
            rank = seen + _dot(before, eq_b)
            sel = jnp.logical_or(key > thr, jnp.logical_and(eq, rank < need))
            sel = jnp.logical_and(sel, rel_t >= (j - i) * t)
            bias_ref[j] = jnp.transpose(jnp.where(sel, 0.0, NEG))
            return seen + jnp.sum(eq_b.astype(F32), axis=0, keepdims=True)

        lax.fori_loop(0, i + 1, select_tile, jnp.zeros((1, t), F32))

    _select()
    rep = DSA_HEADS // DSA_KV_HEADS
    for g in range(DSA_KV_HEADS):
        kv_lanes = slice(g * LANES, (g + 1) * LANES)
        qs = []
        for h in range(g * rep, (g + 1) * rep):
            qp = q_ref[0, :, (h // 2) * LANES:(h // 2 + 1) * LANES]
            extra = slope_ref[h:h + 1, :]
            qs.append(jnp.where(lo, qp, extra) if h % 2 == 0 else jnp.where(lo, extra, qp))

        def tile(j, carry, qs=qs, kv_lanes=kv_lanes):
            start = pl.multiple_of(j * t, t)
            k = k_ref[0, pl.ds(start, t), kv_lanes]
            v = v_ref[0, pl.ds(start, t), kv_lanes]
            ks = [jnp.where(lo, k, pos_ref[0, pl.ds(start, t), :]), jnp.where(lo, pos_ref[1, pl.ds(start, t), :], k)]
            bias = bias_ref[j]
            ss = [_dot_t(qs[n], ks[n % 2]) + bias for n in range(rep)]
            return _flash_pair(ss, v, carry, acc_ref)

        acc_ref[...] = jnp.zeros(acc_ref.shape, F32)
        carry = tile(i, _flash_init(t, rep))
        lax.fori_loop(0, i, tile, carry)
        for n in range(0, rep, 2):
            pair = (g * rep + n) // 2
            o_ref[0, :, pair * LANES:(pair + 1) * LANES] = _flash_finish(acc_ref, n).astype(BF16)


ALIBI_PIECES = 6


def _dsa_alibi_tables(seq):
    bf = lambda a: np.asarray(np.asarray(a, np.float32).astype(jnp.bfloat16), np.float32)
    slopes = np.asarray(_alibi_slopes(DSA_HEADS), np.float64) * LOG2E
    slopes = slopes.astype(np.float32)
    s1 = bf(slopes)
    s2 = bf(slopes - s1)
    s3 = bf(slopes - s1 - s2)
    slope_tab = np.zeros((DSA_HEADS, LANES), np.float32)
    for h in range(DSA_HEADS):
        off = HALF if h % 2 == 0 else 0
        slope_tab[h, off:off + ALIBI_PIECES] = [s1[h], s2[h], s3[h], s1[h], s2[h], s3[h]]
    pos = np.arange(seq)
    hi, lw = (pos // 256 * 256).astype(np.float32), (pos % 256).astype(np.float32)
    pieces = np.stack([hi, hi, hi, lw, lw, lw], axis=1)
    pos_tab = np.zeros((2, seq, LANES), np.float32)
    pos_tab[0, :, HALF:HALF + ALIBI_PIECES] = pieces
    pos_tab[1, :, :ALIBI_PIECES] = pieces
    return jnp.asarray(slope_tab, BF16), jnp.asarray(pos_tab, BF16)


def _dsa_attention(proj, wi, batch, seq, t=512):
    t = min(t, seq)
    nt = seq // t
    topk = min(IDX_TOPK, seq // 4)
    qw = DSA_HEADS * DSA_HEAD_DIM
    q0 = 0
    k0 = qw // LANES
    v0 = k0 + DSA_KV_HEADS
    qi0 = (v0 + DSA_KV_HEADS) * LANES // (IDX_HEADS * IDX_DIM)
    ki0 = v0 + DSA_KV_HEADS + IDX_HEADS * IDX_DIM // LANES
    kvw = DSA_KV_HEADS * LANES
    slope_tab, pos_tab = _dsa_alibi_tables(seq)
    return pl.pallas_call(
        functools.partial(_dsa_kernel, t=t, topk=topk),
        grid=(batch, nt),
        in_specs=[pl.BlockSpec(slope_tab.shape, lambda b, i: (0, 0)),
                  pl.BlockSpec(pos_tab.shape, lambda b, i: (0, 0, 0)),
                  pl.BlockSpec((1, t, IDX_HEADS * IDX_DIM), lambda b, i: (b, i, qi0)),
                  pl.BlockSpec((1, seq, LANES), lambda b, i: (b, 0, ki0)),
                  pl.BlockSpec((1, t, LANES), lambda b, i: (b, i, 0)),
                  pl.BlockSpec((1, t, qw), lambda b, i: (b, i, q0)),
                  pl.BlockSpec((1, seq, kvw), lambda b, i: (b, 0, k0 * LANES // kvw)),
                  pl.BlockSpec((1, seq, kvw), lambda b, i: (b, 0, v0 * LANES // kvw))],
        out_specs=pl.BlockSpec((1, t, qw), lambda b, i: (b, i, 0)),
        out_shape=jax.ShapeDtypeStruct((batch, seq, qw), BF16),
        scratch_shapes=[pltpu.VMEM((nt, t, t), jnp.int32), pltpu.VMEM((nt, t, t), F32),
                        pltpu.VMEM((nt, t, t), BF16),
                        pltpu.VMEM((DSA_HEADS // DSA_KV_HEADS, t, LANES), F32)],
        compiler_params=_params("parallel", "arbitrary"),
        name="dsa_attention",
    )(slope_tab, pos_tab, proj, proj, wi, proj, proj, proj)


def _sb_kernel(q_ref, k_ref, v_ref, o_ref, later_ref, acc_ref, *, tq, tk):
    assert tq == tk
    lo = _lane_lo()
    blk = min(tk, 2 * LANES)
    suffix = jnp.where(_tile_rel(blk, blk) >= 0, 1.0, 0.0).astype(BF16)
    half = tk // 2
    strict = _tile_rel(half, half) > 0
    hs = range(2)

    def on_diagonal(x, fill):
        top = jnp.where(strict, x[:half], fill)
        return top if x.shape[0] == half else jnp.concatenate([top, x[half:]], axis=0)

    for i in range(q_ref.shape[1] // tq):
        rows = slice(i * tq, (i + 1) * tq)
        qp = q_ref[0, rows, :]
        qs = [jnp.where(lo if hh == 0 else jnp.logical_not(lo), qp, jnp.zeros_like(qp)) for hh in hs]

        def tile(start, nk, r0, diag, qs=qs):
            nr = tq - r0
            k = k_ref[0, pl.ds(start, nk), :]
            v = v_ref[0, pl.ds(start, nk), :]
            zs = [_dot_t(qs[hh][r0:], k) for hh in hs]
            lgs = []
            for z in zs:
                nz = -z
                lgs.append(jnp.minimum(nz, 0.0) - jnp.log2(1.0 + jnp.exp2(jnp.minimum(z, nz))))
            if diag:
                lgs = [on_diagonal(lg, 0.0) for lg in lgs]
            lbs = [lg.astype(BF16) for lg in lgs]
            runs, tots = [], []
            for hh in hs:
                parts, after = [], later_ref[hh, r0:, :]
                for c0 in reversed(range(0, nk, blk)):
                    part = _dot(lbs[hh][:, c0:c0 + blk], suffix) + jnp.concatenate([after] * (blk // LANES), axis=1)
                    after = jnp.broadcast_to(part[:, 0:1], (nr, LANES))
                    parts.insert(0, part)
                runs.append(jnp.concatenate(parts, axis=1))
                tots.append(after)
            a_s = [jnp.exp2(zs[hh] + runs[hh]) for hh in hs]
            if diag:
                a_s = [on_diagonal(a, 0.0) for a in a_s]
            for hh in hs:
                later_ref[hh, r0:, :] = tots[hh]
            for hh in hs:
                acc_ref[hh, r0:, :] += _dot(a_s[hh].astype(BF16), v)

        later_ref[...] = jnp.zeros(later_ref.shape, F32)
        acc_ref[...] = jnp.zeros(acc_ref.shape, F32)
        tile(i * tq + half, half, half, True)
        tile(i * tq, half, 0, True)
        for j in reversed(range(i)):
            tile(j * tk, tk, 0, False)
        o_ref[0, rows, :] = jnp.where(lo, acc_ref[0], acc_ref[1]).astype(BF16)


def _sb_attention(qkv, batch, seq, tq=512, tk=512):
    tq, tk = min(tq, seq), min(tk, seq)
    w = SB_HEADS * SB_HEAD_DIM
    pairs = SB_HEADS // 2
    return pl.pallas_call(
        functools.partial(_sb_kernel, tq=tq, tk=tk),
        grid=(batch, pairs),
        in_specs=[pl.BlockSpec((1, seq, LANES), lambda b, m: (b, 0, m)),
                  pl.BlockSpec((1, seq, LANES), lambda b, m: (b, 0, pairs + m)),
                  pl.BlockSpec((1, seq, LANES), lambda b, m: (b, 0, 2 * pairs + m))],
        out_specs=pl.BlockSpec((1, seq, LANES), lambda b, m: (b, 0, m)),
        out_shape=jax.ShapeDtypeStruct((batch, seq, w), BF16),
        scratch_shapes=[pltpu.VMEM((2, tq, LANES), F32), pltpu.VMEM((2, tq, LANES), F32)],
        compiler_params=_params("parallel", "parallel"),
        name="sb_attention",
    )(qkv, qkv, qkv)


def _dup_heads(w, n_heads, dim):
    d = w.shape[0]
    w = w.reshape(d, n_heads, 1, dim)
    return jnp.broadcast_to(w, (d, n_heads, 2, dim)).reshape(d, n_heads * 2 * dim)


def _pad_heads(w, n_heads, dim, to):
    d = w.shape[0]
    w = w.reshape(d, n_heads, dim)
    return jnp.pad(w, ((0, 0), (0, 0), (0, to - dim))).reshape(d, n_heads * to)


def _swa_w_in(w):
    qw = SWA_HEADS * SWA_HEAD_DIM
    kw = SWA_KV_HEADS * SWA_HEAD_DIM
    q, k, v = w[:, :qw], w[:, qw:qw + kw], w[:, qw + kw:]
    q = q * (LOG2E * SWA_HEAD_DIM ** -0.5)
    return jnp.concatenate([q, _dup_heads(k, SWA_KV_HEADS, SWA_HEAD_DIM),
                            _dup_heads(v, SWA_KV_HEADS, SWA_HEAD_DIM)], axis=1).astype(BF16)


def _mla_w_in(w):
    d = w.shape[0]
    lat = MLA_Q_LORA + MLA_KV_LORA
    z = lambda c: jnp.zeros((d, c), w.dtype)
    return jnp.concatenate([w[:, :lat], z(MLA_NOPE), w[:, lat:], z(LANES - MLA_NOPE - MLA_ROPE)],
                           axis=1).astype(BF16)


def _sb_w_in(w):
    qw = SB_HEADS * SB_HEAD_DIM
    return jnp.concatenate([w[:, :qw] * (LOG2E * SB_HEAD_DIM ** -0.5), w[:, qw:]], axis=1).astype(BF16)


def _dsa_w_in(w):
    qw = DSA_HEADS * DSA_HEAD_DIM
    kw = DSA_KV_HEADS * DSA_HEAD_DIM
    iw = IDX_HEADS * IDX_DIM
    c = np.cumsum([qw, kw, kw, iw, IDX_DIM]).tolist()
    q, k, v, qi, ki, wi = (w[:, :c[0]], w[:, c[0]:c[1]], w[:, c[1]:c[2]], w[:, c[2]:c[3]],
                           w[:, c[3]:c[4]], w[:, c[4]:])
    q = q * (LOG2E * DSA_HEAD_DIM ** -0.5)
    qi = qi * (IDX_DIM ** -0.5)
    main = jnp.concatenate([q, _dup_heads(k, DSA_KV_HEADS, DSA_HEAD_DIM),
                            _dup_heads(v, DSA_KV_HEADS, DSA_HEAD_DIM), qi, _dup_heads(ki, 1, IDX_DIM)],
                           axis=1).astype(BF16)
    wi = jnp.pad(wi, ((0, 0), (0, LANES - IDX_HEADS))).astype(BF16)
    return main, wi


def kernel(x, ffn1_w_gu, ffn1_w_down, ffn2_w_gu, ffn2_w_down, ln_g, ln_b, a_w_in, a_sinks, a_w_o,
           b_w_in, b_q_norm, b_w_uq, b_kv_norm, b_w_ukv, b_w_o, c_w_in, c_w_o, d_w_in, d_w_o):
    batch, seq, d = x.shape
    n = batch * seq
    x = x.reshape(n, d)
    ffn1_gu, ffn1_down = ffn1_w_gu.astype(BF16), ffn1_w_down.astype(BF16)
    ffn2_gu, ffn2_down = ffn2_w_gu.astype(BF16), ffn2_w_down.astype(BF16)
    for i in range(DEPTH):
        mixer, j = i % N_MIXERS, i // N_MIXERS

        def ffn1(projs):
            return _ffn_proj(x, ffn1_gu, ffn1_down, i, ln_g[i, 0], ln_b[i, 0], projs)

        if mixer == 0:
            x, qkv = ffn1([(_swa_w_in(a_w_in[j]), BF16)])
            o = _swa_attention(qkv.reshape(batch, seq, -1), a_sinks[j] * LOG2E, batch, seq)
            w_o = a_w_o[j]
        elif mixer == 1:
            x, c = ffn1([(_mla_w_in(b_w_in[j]), F32)])
            hd = MLA_NOPE + MLA_ROPE
            wq = _pad_heads(b_w_uq[j], MLA_HEADS, hd, LANES).astype(BF16)
            wkv = b_w_ukv[j].reshape(MLA_KV_LORA, MLA_HEADS, MLA_NOPE + MLA_V)
            wk = _pad_heads(wkv[:, :, :MLA_NOPE].reshape(MLA_KV_LORA, -1), MLA_HEADS, MLA_NOPE,
                            LANES).astype(BF16)
            wv = wkv[:, :, MLA_NOPE:].reshape(MLA_KV_LORA, -1).astype(BF16)
            q, k, v = _mla_up(c, b_q_norm[j], b_kv_norm[j], wq, wk, wv, seq)
            o = _mla_attention(q.reshape(batch, seq, -1), k.reshape(batch, seq, -1),
                               v.reshape(batch, seq, -1))
            w_o = b_w_o[j]
        elif mixer == 2:
            w_main, w_wi = _dsa_w_in(c_w_in[j])
            x, proj, wi = ffn1([(w_main, BF16), (w_wi, F32)])
            o = _dsa_attention(proj.reshape(batch, seq, -1), wi.reshape(batch, seq, -1), batch, seq)
            w_o = c_w_o[j]
        else:
            x, qkv = ffn1([(_sb_w_in(d_w_in[j]), BF16)])
            o = _sb_attention(qkv.reshape(batch, seq, -1), batch, seq)
            w_o = d_w_o[j]
        x = _proj_ffn(o.reshape(n, -1), w_o.astype(BF16), x, ln_g[i, 1], ln_b[i, 1], ffn2_gu, ffn2_down, i,
                      ln_g[i, 2], ln_b[i, 2])
    return x.reshape(batch, seq, d)
```

```python
import functools
import math

import numpy as np
import jax
import jax.numpy as jnp
from jax import lax
from jax.experimental import pallas as pl
from jax.experimental.pallas import tpu as pltpu

DEPTH = 4
N_MIXERS = 4
LN_EPS = 1e-5
RMS_EPS = 1e-6
NEG = -1e30

SWA_HEADS, SWA_KV_HEADS, SWA_HEAD_DIM, SWA_BLOCK = 16, 4, 64, 128
MLA_HEADS, MLA_Q_LORA, MLA_KV_LORA, MLA_NOPE, MLA_ROPE, MLA_V = 16, 768, 256, 64, 32, 64
ROPE_THETA = 10000.0
DSA_HEADS, DSA_KV_HEADS, DSA_HEAD_DIM = 16, 4, 64
IDX_HEADS, IDX_DIM, IDX_TOPK = 8, 64, 256
SB_HEADS, SB_HEAD_DIM = 16, 64

ALPHA = (2.0 * DEPTH) ** 0.25
LOG2E = math.log2(math.e)

LANES = 128
HALF = 64
VMEM_LIMIT = 56 * 1024 * 1024
BF16 = jnp.bfloat16
F32 = jnp.float32


def _params(*sem):
    return pltpu.CompilerParams(dimension_semantics=sem, vmem_limit_bytes=VMEM_LIMIT)


def _dot(a, b):
    return jnp.dot(a, b, preferred_element_type=F32)


def _dot_t(a, b):
    return lax.dot_general(a, b, (((1,), (1,)), ((), ())), preferred_element_type=F32)


def _alibi_slopes(n_heads):
    return [2.0 ** (-8.0 * (i + 1) / n_heads) for i in range(n_heads)]


def _layer_norm(y, g, b):
    mu = jnp.mean(y, axis=-1, keepdims=True)
    yc = y - mu
    var = jnp.mean(yc * yc, axis=-1, keepdims=True)
    return yc * lax.rsqrt(var + LN_EPS) * g + b


def _lane_lo():
    return lax.broadcasted_iota(jnp.int32, (1, LANES), 1) < HALF


def _resident(shape):
    return pl.BlockSpec(shape, lambda i: (0,) * len(shape), pipeline_mode=pl.Buffered(1))


def _layer(stack, layer):
    return pl.BlockSpec((None,) + stack.shape[1:], lambda i: (layer, 0, 0), pipeline_mode=pl.Buffered(1))


def _rows(tm, width):
    return pl.BlockSpec((tm, width), lambda i: (i, 0))


def _ffn_ln_value(x, wgu_ref, wd_ref, g_ref, b_ref):
    f = wd_ref.shape[0]
    xb = x.astype(BF16)
    h = _dot(xb, wgu_ref[:, :f])
    u = _dot(xb, wgu_ref[:, f:])
    a = h * (1.0 / (1.0 + jnp.exp(-h))) * u
    y = ALPHA * x + 0.5 * _dot(a.astype(BF16), wd_ref[...])
    return _layer_norm(y, g_ref[...], b_ref[...])


def _ffn_proj_kernel(x_ref, wgu_ref, wd_ref, g_ref, b_ref, *refs, n_proj):
    w_refs, o_ref, p_refs = refs[:n_proj], refs[n_proj], refs[n_proj + 1:]
    y = _ffn_ln_value(x_ref[...], wgu_ref, wd_ref, g_ref, b_ref)
    o_ref[...] = y
    yb = y.astype(BF16)
    for w_ref, p_ref in zip(w_refs, p_refs):
        p_ref[...] = _dot(yb, w_ref[...]).astype(p_ref.dtype)


def _ffn_proj(x, w_gu, w_down, layer, g, b, projs, tm=512):
    n, d = x.shape
    tm = min(tm, n)
    ws = [w for w, _ in projs]
    return pl.pallas_call(
        functools.partial(_ffn_proj_kernel, n_proj=len(projs)),
        grid=(n // tm,),
        in_specs=[_rows(tm, d), _layer(w_gu, layer), _layer(w_down, layer), _resident((1, d)),
                  _resident((1, d))] + [_resident(w.shape) for w in ws],
        out_specs=[_rows(tm, d)] + [_rows(tm, w.shape[1]) for w in ws],
        out_shape=[jax.ShapeDtypeStruct((n, d), F32)]
        + [jax.ShapeDtypeStruct((n, w.shape[1]), dt) for w, dt in projs],
        compiler_params=_params("parallel"),
        name="ffn_in_proj",
    )(x, w_gu, w_down, g.reshape(1, d), b.reshape(1, d), *ws)


def _proj_ffn_kernel(o_ref, wo_ref, x_ref, g1_ref, b1_ref, wgu_ref, wd_ref, g2_ref, b2_ref, out_ref):
    y = ALPHA * x_ref[...] + _dot(o_ref[...], wo_ref[...])
    x2 = _layer_norm(y, g1_ref[...], b1_ref[...])
    out_ref[...] = _ffn_ln_value(x2, wgu_ref, wd_ref, g2_ref, b2_ref)


def _proj_ffn(o, w_o, x, g1, b1, w_gu, w_down, layer, g2, b2, tm=512):
    n, k = o.shape
    d = w_o.shape[1]
    tm = min(tm, n)
    vec = lambda a: a.reshape(1, d)
    return pl.pallas_call(
        _proj_ffn_kernel,
        grid=(n // tm,),
        in_specs=[_rows(tm, k), _resident(w_o.shape), _rows(tm, d), _resident((1, d)), _resident((1, d)),
                  _layer(w_gu, layer), _layer(w_down, layer), _resident((1, d)), _resident((1, d))],
        out_specs=_rows(tm, d),
        out_shape=jax.ShapeDtypeStruct((n, d), F32),
        compiler_params=_params("parallel"),
        name="out_proj_ffn",
    )(o, w_o, x, vec(g1), vec(b1), w_gu, w_down, vec(g2), vec(b2))


def _swa_kernel(sink_ref, q_ref, kc_ref, kp_ref, vc_ref, vp_ref, o_ref, *, tq):
    i = pl.program_id(1)
    t = SWA_BLOCK
    rep = SWA_HEADS // SWA_KV_HEADS
    kvs = range(SWA_KV_HEADS)
    lo = _lane_lo()
    rel = _tile_rel(t, 2 * t) + t
    col = lax.broadcasted_iota(jnp.int32, (t, 2 * t), 1)
    dist = rel.astype(F32)
    in_band = jnp.logical_and(rel >= 0, rel < SWA_BLOCK)
    slopes = [s * LOG2E for s in _alibi_slopes(SWA_HEADS)]
    lane = lambda g: slice(g * LANES, (g + 1) * LANES)
    for qb in range(tq // t):
        r0 = qb * t
        if qb == 0:
            ks = [jnp.concatenate([kp_ref[0, :, lane(g)], kc_ref[0, 0:t, lane(g)]], axis=0) for g in kvs]
            vs = [jnp.concatenate([vp_ref[0, :, lane(g)], vc_ref[0, 0:t, lane(g)]], axis=0) for g in kvs]
            valid = jnp.logical_and(in_band, col >= jnp.where(i == 0, t, 0))
        else:
            ks = [kc_ref[0, r0 - t:r0 + t, lane(g)] for g in kvs]
            vs = [vc_ref[0, r0 - t:r0 + t, lane(g)] for g in kvs]
            valid = in_band
        qst = []
        for g in kvs:
            parts = []
            for r in range(rep):
                h = g * rep + r
                qp = q_ref[0, r0:r0 + t, lane(h // 2)]
                parts.append(jnp.where(lo if h % 2 == 0 else jnp.logical_not(lo), qp, jnp.zeros_like(qp)))
            qst.append(jnp.concatenate(parts, axis=0))
        ss = [_dot_t(qst[g], ks[g]) for g in kvs]
        ps, dens = [], []
        for g in kvs:
            prow, drow = [], []
            for r in range(rep):
                h = g * rep + r
                s = jnp.where(valid, ss[g][r * t:(r + 1) * t] - slopes[h] * dist, NEG)
                sink = sink_ref[h]
                mx = jnp.maximum(jnp.max(s, axis=-1, keepdims=True), sink)
                p = jnp.exp2(s - mx)
                drow.append(jnp.sum(p, axis=-1, keepdims=True) + jnp.exp2(sink - mx))
                prow.append(p.astype(BF16))
            ps.append(jnp.concatenate(prow, axis=0))
            dens.append(drow)
        outs = [_dot(ps[g], vs[g]) for g in kvs]
        for g in kvs:
            for r in range(0, rep, 2):
                even = outs[g][r * t:(r + 1) * t] / dens[g][r]
                odd = outs[g][(r + 1) * t:(r + 2) * t] / dens[g][r + 1]
                o_ref[0, r0:r0 + t, lane((g * rep + r) // 2)] = jnp.where(lo, even, odd).astype(BF16)


def _swa_attention(qkv, sinks, batch, seq, tq=512):
    t = SWA_BLOCK
    tq = min(tq, seq)
    sub = tq // t
    qw = SWA_HEADS * SWA_HEAD_DIM
    kw = SWA_KV_HEADS * LANES
    qblk, kblk, vblk = 0, qw // kw, qw // kw + 1
    prev = lambda i: jnp.maximum(i * sub - 1, 0)
    return pl.pallas_call(
        functools.partial(_swa_kernel, tq=tq),
        grid=(batch, seq // tq),
        in_specs=[pl.BlockSpec(memory_space=pltpu.SMEM),
                  pl.BlockSpec((1, tq, qw), lambda b, i: (b, i, qblk)),
                  pl.BlockSpec((1, tq, kw), lambda b, i: (b, i, kblk)),
                  pl.BlockSpec((1, t, kw), lambda b, i: (b, prev(i), kblk)),
                  pl.BlockSpec((1, tq, kw), lambda b, i: (b, i, vblk)),
                  pl.BlockSpec((1, t, kw), lambda b, i: (b, prev(i), vblk))],
        out_specs=pl.BlockSpec((1, tq, qw), lambda b, i: (b, i, 0)),
        out_shape=jax.ShapeDtypeStruct((batch, seq, qw), BF16),
        compiler_params=_params("parallel", "arbitrary"),
        name="swa_attention",
    )(sinks, qkv, qkv, qkv, qkv, qkv)


def _mla_up_kernel(c_ref, gq_ref, gkv_ref, wq_ref, wk_ref, wv_ref, cq_ref, s1q_ref, s2q_ref,
                   ck_ref, s1k_ref, s2k_ref, q_ref, k_ref, v_ref):
    c = c_ref[...]
    cq = c[:, :MLA_Q_LORA]
    ckv = c[:, MLA_Q_LORA:MLA_Q_LORA + MLA_KV_LORA]
    kr = c[:, MLA_Q_LORA + MLA_KV_LORA:]
    cq = cq * lax.rsqrt(jnp.mean(cq * cq, axis=-1, keepdims=True) + RMS_EPS) * gq_ref[...]
    ckv = ckv * lax.rsqrt(jnp.mean(ckv * ckv, axis=-1, keepdims=True) + RMS_EPS) * gkv_ref[...]
    ckv_b = ckv.astype(BF16)
    q = _dot(cq.astype(BF16), wq_ref[...])
    kn = _dot(ckv_b, wk_ref[...])
    v_ref[...] = _dot(ckv_b, wv_ref[...]).astype(BF16)

    def rot(xh, cc, s1, s2):
        return xh * cc + pltpu.roll(xh, 16, 1) * s1 + pltpu.roll(xh, LANES - 16, 1) * s2

    kr = rot(kr, ck_ref[...], s1k_ref[...], s2k_ref[...])
    cq_t, s1q, s2q = cq_ref[...], s1q_ref[...], s2q_ref[...]
    for h in range(MLA_HEADS):
        sl = slice(h * LANES, (h + 1) * LANES)
        q_ref[:, sl] = rot(q[:, sl], cq_t, s1q, s2q).astype(BF16)
        k_ref[:, sl] = (kn[:, sl] + kr).astype(BF16)


def _mla_rope_tables(seq, scale):
    half = MLA_ROPE // 2
    inv = ROPE_THETA ** (-jnp.arange(0, MLA_ROPE, 2, dtype=F32) / MLA_ROPE)
    ang = jnp.arange(seq, dtype=F32)[:, None] * inv[None, :]
    cos, sin = jnp.cos(ang), jnp.sin(ang)
    z = lambda w: jnp.zeros((seq, w), F32)
    one = jnp.ones((seq, MLA_NOPE), F32)
    cc = jnp.concatenate([one, cos, cos, z(LANES - MLA_NOPE - MLA_ROPE)], axis=1) * scale
    s1 = jnp.concatenate([z(MLA_NOPE + half), sin, z(LANES - MLA_NOPE - MLA_ROPE)], axis=1) * scale
    s2 = jnp.concatenate([z(MLA_NOPE), -sin, z(LANES - MLA_NOPE - half)], axis=1) * scale
    return cc, s1, s2


def _mla_up(c, gq, gkv, wq, wk, wv, seq, tm=512):
    n, cw = c.shape
    tm = min(tm, seq)
    spt = seq // tm
    scale = LOG2E * (MLA_NOPE + MLA_ROPE) ** -0.5
    tq = _mla_rope_tables(seq, scale)
    tk = _mla_rope_tables(seq, 1.0)
    qk_w = MLA_HEADS * LANES
    v_w = MLA_HEADS * MLA_V
    full = lambda a: _resident(a.shape)
    tab = pl.BlockSpec((tm, LANES), lambda i: (i % spt, 0))
    gq2, gkv2 = gq.reshape(1, -1), gkv.reshape(1, -1)
    return pl.pallas_call(
        _mla_up_kernel,
        grid=(n // tm,),
        in_specs=[pl.BlockSpec((tm, cw), lambda i: (i, 0)), full(gq2), full(gkv2), full(wq), full(wk),
                  full(wv), tab, tab, tab, tab, tab, tab],
        out_specs=[pl.BlockSpec((tm, qk_w), lambda i: (i, 0)),
                   pl.BlockSpec((tm, qk_w), lambda i: (i, 0)),
                   pl.BlockSpec((tm, v_w), lambda i: (i, 0))],
        out_shape=[jax.ShapeDtypeStruct((n, qk_w), BF16), jax.ShapeDtypeStruct((n, qk_w), BF16),
                   jax.ShapeDtypeStruct((n, v_w), BF16)],
        compiler_params=_params("parallel"),
        name="mla_up_proj",
    )(c, gq2, gkv2, wq, wk, wv, *tq, *tk)


SUM_LANE = (HALF, 0)


def _with_sum_column(v):
    lane = lax.broadcasted_iota(jnp.int32, (1, LANES), 1)
    unit = lambda at: jnp.where(lane == at, 1.0, 0.0).astype(v.dtype)
    low = lane < HALF
    return [jnp.where(low, v, unit(SUM_LANE[0])), jnp.where(low, unit(SUM_LANE[1]), v)]


def _flash_finish(acc_ref, first=0):
    outs = [acc_ref[first + hh] / acc_ref[first + hh][:, SUM_LANE[hh]:SUM_LANE[hh] + 1] for hh in range(2)]
    return jnp.where(_lane_lo(), outs[0], outs[1])


def _row_max(s):
    mx = s[:, :LANES]
    for c0 in range(LANES, s.shape[1], LANES):
        mx = jnp.maximum(mx, s[:, c0:c0 + LANES])
    return jnp.max(mx, axis=-1, keepdims=True)


def _flash_init(tq, heads=2):
    return (jnp.full((tq, LANES), NEG, F32),) * heads


def _flash_pair(ss, v, m_old, acc_ref, r0=0):
    hs = range(len(ss))
    vs = _with_sum_column(v)
    cols = [slice(c0, c0 + LANES) for c0 in range(0, ss[0].shape[1], LANES)]
    old = [m_old[hh][r0:] for hh in hs]
    ms = [jnp.maximum(old[hh], _row_max(ss[hh])) for hh in hs]
    ps = [jnp.concatenate([jnp.exp2(ss[hh][:, c] - ms[hh]) for c in cols], axis=1).astype(BF16) for hh in hs]
    al = [jnp.exp2(old[hh] - ms[hh]) for hh in hs]
    for hh in hs:
        acc_ref[hh, r0:, :] = al[hh] * acc_ref[hh, r0:, :] + _dot(ps[hh], vs[hh % 2])
    if r0:
        ms = [jnp.concatenate([m_old[hh][:r0], ms[hh]], axis=0) for hh in hs]
    return tuple(ms)


SOFTMAX_ROWS = 32


def _flash_scratch(tq, tk):
    return [pltpu.VMEM((2, tq, tk), F32), pltpu.VMEM((2, tq, tk), BF16), pltpu.VMEM((2, tq, LANES), F32),
            pltpu.VMEM((2, tq, LANES), F32)]


def _flash_reset(m_ref, acc_ref):
    m_ref[...] = jnp.full(m_ref.shape, NEG, F32)
    acc_ref[...] = jnp.zeros(acc_ref.shape, F32)


def _flash_update(s_ref, p_ref, m_ref, acc_ref, v):
    _, tq, tk = s_ref.shape
    vs = _with_sum_column(v)
    for c in range(tq // SOFTMAX_ROWS):
        rows = slice(c * SOFTMAX_ROWS, (c + 1) * SOFTMAX_ROWS)
        for hh in range(2):
            s = s_ref[hh, rows, :]
            m_old = m_ref[hh, rows, :]
            m_new = jnp.maximum(m_old, _row_max(s))
            m_ref[hh, rows, :] = m_new
            acc_ref[hh, rows, :] = jnp.exp2(m_old - m_new) * acc_ref[hh, rows, :]
            for c0 in range(0, tk, LANES):
                p_ref[hh, rows, c0:c0 + LANES] = jnp.exp2(s[:, c0:c0 + LANES] - m_new).astype(BF16)
    for hh in range(2):
        acc_ref[hh] += _dot(p_ref[hh], vs[hh])


def _tile_rel(tq, tk):
    return (lax.broadcasted_iota(jnp.int32, (tq, tk), 0) - lax.broadcasted_iota(jnp.int32, (tq, tk), 1))


def _mla_attn_kernel(q_ref, k_ref, v_ref, o_ref, s_ref, p_ref, m_ref, acc_ref, *, tq, tk):
    r = tq // tk
    rel = _tile_rel(tq, tk)
    sls = [slice(hh * LANES, (hh + 1) * LANES) for hh in range(2)]
    for i in range(q_ref.shape[1] // tq):
        rows = slice(i * tq, (i + 1) * tq)

        def tile(j, diag, rows=rows):
            start = pl.multiple_of(j * tk, tk)
            for hh in range(2):
                s = _dot_t(q_ref[0, rows, sls[hh]], k_ref[0, pl.ds(start, tk), sls[hh]])
                if diag is not None:
                    s = jnp.where(rel >= diag * tk, s, NEG)
                s_ref[hh] = s
            _flash_update(s_ref, p_ref, m_ref, acc_ref, v_ref[0, pl.ds(start, tk), :])

        _flash_reset(m_ref, acc_ref)
        for d in range(r):
            tile(i * r + d, d)
        for j in range(i * r):
            tile(j, None)
        o_ref[0, rows, :] = _flash_finish(acc_ref).astype(BF16)


def _mla_attention(q, k, v, tq=512, tk=512):
    batch, seq, _ = q.shape
    tq, tk = min(tq, seq), min(tk, seq)
    return pl.pallas_call(
        functools.partial(_mla_attn_kernel, tq=tq, tk=tk),
        grid=(batch, MLA_HEADS // 2),
        in_specs=[pl.BlockSpec((1, seq, 2 * LANES), lambda b, m: (b, 0, m)),
                  pl.BlockSpec((1, seq, 2 * LANES), lambda b, m: (b, 0, m)),
                  pl.BlockSpec((1, seq, LANES), lambda b, m: (b, 0, m))],
        out_specs=pl.BlockSpec((1, seq, LANES), lambda b, m: (b, 0, m)),
        out_shape=jax.ShapeDtypeStruct(v.shape, BF16),
        scratch_shapes=_flash_scratch(tq, tk),
        compiler_params=_params("parallel", "parallel"),
        name="mla_attention",
    )(q, k, v)


def _order_key(x):
    bits = lax.bitcast_convert_type(x, jnp.int32)
    return bits ^ (lax.shift_right_arithmetic(bits, 31) & 0x7FFFFFFF)


def _dsa_kernel(slope_ref, pos_ref, qi_ref, ki_ref, wi_ref, q_ref, k_ref, v_ref, o_ref, key_ref, bias_ref,
                byte_ref, acc_ref, *, t, topk):
    i = pl.program_id(1)
    lo = _lane_lo()
    row = lax.broadcasted_iota(jnp.int32, (t, t), 0)
    col = lax.broadcasted_iota(jnp.int32, (t, t), 1)
    rel = row - col

    def _select():
        wi_t = jnp.transpose(wi_ref[0] * (IDX_HEADS ** -0.5))
        rel_t = col - row
        pack = 16

        def score_tile(j, carry):
            start = pl.multiple_of(j * t, t)
            ki = ki_ref[0, pl.ds(start, t), :]
            sc = jnp.zeros((t, t), F32)
            for p in range(IDX_HEADS // 2):
                qp = qi_ref[0, :, p * LANES:(p + 1) * LANES]
                for hh in range(2):
                    h = 2 * p + hh
                    qm = jnp.where(lo if hh == 0 else jnp.logical_not(lo), qp, jnp.zeros_like(qp))
                    dots = _dot_t(ki, qm)
                    sc = sc + wi_t[h:h + 1, :] * jnp.maximum(dots, 0.0)
            sc = sc + 0.0
            sc = jnp.where(rel_t >= (j - i) * t, sc, NEG)
            key_ref[j] = _order_key(sc)
            return carry

        lax.fori_loop(0, i + 1, score_tile, 0)

        one_b, zero_b = jnp.ones((), BF16), jnp.zeros((), BF16)

        def count_ge(cand):
            cand_b = cand.astype(BF16)

            def body(j, part):
                ge = jnp.where(byte_ref[j] >= cand_b, one_b, zero_b)
                slabs = [ge[r0:r0 + pack] for r0 in range(0, t, pack)]
                while len(slabs) > 1:
                    slabs = [a + b for a, b in zip(slabs[0::2], slabs[1::2])]
                return part + slabs[0]
            part = lax.fori_loop(0, i + 1, body, jnp.zeros((pack, t), BF16))
            return jnp.sum(part.astype(F32), axis=0, keepdims=True)

        prefix = jnp.zeros((1, t), jnp.int32)
        above = jnp.zeros((1, t), F32)
        for stage in range(4):
            shift = 24 - 8 * stage

            def fill(j, carry, stage=stage, shift=shift, prefix=prefix):
                key = key_ref[j]
                if stage == 0:
                    byte = lax.shift_right_arithmetic(key, shift) + 128
                else:
                    match = lax.shift_right_arithmetic(key, shift + 8) == prefix
                    byte = jnp.where(match, lax.shift_right_logical(key, shift) & 255, -1)
                byte_ref[j] = byte.astype(F32).astype(BF16)
                return carry

            lax.fori_loop(0, i + 1, fill, 0)
            val = jnp.zeros((1, t), F32)
            rej = jnp.zeros((1, t), F32)
            for bit in (128, 64, 32, 16, 8, 4, 2, 1):
                cand = val + float(bit)
                cnt = count_ge(cand)
                ok = above + cnt >= topk
                val = jnp.where(ok, cand, val)
                rej = jnp.where(ok, rej, cnt)
            above = above + rej
            byte_i = val.astype(jnp.int32) - (128 if stage == 0 else 0)
            prefix = byte_i if stage == 0 else lax.shift_left(prefix, 8) | byte_i
        thr = prefix
        need = topk - above
        before = jnp.where(row > col, 1.0, 0.0).astype(BF16)

        def select_tile(j, seen):
            key = key_ref[j]
            eq = key == thr
            eq_b = jnp.where(eq, 1.0, 0.0).astype(BF16)
            rank = seen + _dot(before, eq_b)
            sel = jnp.logical_or(key > thr, jnp.logical_and(eq, rank < need))
            sel = jnp.logical_and(sel, rel_t >= (j - i) * t)
            bias_ref[j] = jnp.transpose(jnp.where(sel, 0.0, NEG))
            return seen + jnp.sum(eq_b.astype(F32), axis=0, keepdims=True)

        lax.fori_loop(0, i + 1, select_tile, jnp.zeros((1, t), F32))

    _select()
    rep = DSA_HEADS // DSA_KV_HEADS
    for g in range(DSA_KV_HEADS):
        kv_lanes = slice(g * LANES, (g + 1) * LANES)
        qs = []
        for h in range(g * rep, (g + 1) * rep):
            qp = q_ref[0, :, (h // 2) * LANES:(h // 2 + 1) * LANES]
            extra = slope_ref[h:h + 1, :]
            qs.append(jnp.where(lo, qp, extra) if h % 2 == 0 else jnp.where(lo, extra, qp))

        def tile(j, carry, qs=qs, kv_lanes=kv_lanes):
            start = pl.multiple_of(j * t, t)
            k = k_ref[0, pl.ds(start, t), kv_lanes]
            v = v_ref[0, pl.ds(start, t), kv_lanes]
            ks = [jnp.where(lo, k, pos_ref[0, pl.ds(start, t), :]), jnp.where(lo, pos_ref[1, pl.ds(start, t), :], k)]
            bias = bias_ref[j]
            ss = [_dot_t(qs[n], ks[n % 2]) + bias for n in range(rep)]
            return _flash_pair(ss, v, carry, acc_ref)

        def own_keys(r0, carry, qs=qs, kv_lanes=kv_lanes):
            half = t // 2
            keys = pl.ds(pl.multiple_of(i * t + r0, half), half)
            k = k_ref[0, keys, kv_lanes]
            ks = [jnp.where(lo, k, pos_ref[0, keys, :]), jnp.where(lo, pos_ref[1, keys, :], k)]
            bias = bias_ref[i, r0:, r0:r0 + half]
            ss = [_dot_t(qs[n][r0:], ks[n % 2]) + bias for n in range(rep)]
            return _flash_pair(ss, v_ref[0, keys, kv_lanes], carry, acc_ref, r0)

        acc_ref[...] = jnp.zeros(acc_ref.shape, F32)
        carry = own_keys(0, _flash_init(t, rep))
        carry = own_keys(t // 2, carry)
        lax.fori_loop(0, i, tile, carry)
        for n in range(0, rep, 2):
            pair = (g * rep + n) // 2
            o_ref[0, :, pair * LANES:(pair + 1) * LANES] = _flash_finish(acc_ref, n).astype(BF16)


ALIBI_PIECES = 6


def _dsa_alibi_tables(seq):
    bf = lambda a: np.asarray(np.asarray(a, np.float32).astype(jnp.bfloat16), np.float32)
    slopes = np.asarray(_alibi_slopes(DSA_HEADS), np.float64) * LOG2E
    slopes = slopes.astype(np.float32)
    s1 = bf(slopes)
    s2 = bf(slopes - s1)
    s3 = bf(slopes - s1 - s2)
    slope_tab = np.zeros((DSA_HEADS, LANES), np.float32)
    for h in range(DSA_HEADS):
        off = HALF if h % 2 == 0 else 0
        slope_tab[h, off:off + ALIBI_PIECES] = [s1[h], s2[h], s3[h], s1[h], s2[h], s3[h]]
    pos = np.arange(seq)
    hi, lw = (pos // 256 * 256).astype(np.float32), (pos % 256).astype(np.float32)
    pieces = np.stack([hi, hi, hi, lw, lw, lw], axis=1)
    pos_tab = np.zeros((2, seq, LANES), np.float32)
    pos_tab[0, :, HALF:HALF + ALIBI_PIECES] = pieces
    pos_tab[1, :, :ALIBI_PIECES] = pieces
    return jnp.asarray(slope_tab, BF16), jnp.asarray(pos_tab, BF16)


def _dsa_attention(proj, wi, batch, seq, t=512):
    t = min(t, seq)
    nt = seq // t
    topk = min(IDX_TOPK, seq // 4)
    qw = DSA_HEADS * DSA_HEAD_DIM
    q0 = 0
    k0 = qw // LANES
    v0 = k0 + DSA_KV_HEADS
    qi0 = (v0 + DSA_KV_HEADS) * LANES // (IDX_HEADS * IDX_DIM)
    ki0 = v0 + DSA_KV_HEADS + IDX_HEADS * IDX_DIM // LANES
    kvw = DSA_KV_HEADS * LANES
    slope_tab, pos_tab = _dsa_alibi_tables(seq)
    return pl.pallas_call(
        functools.partial(_dsa_kernel, t=t, topk=topk),
        grid=(batch, nt),
        in_specs=[pl.BlockSpec(slope_tab.shape, lambda b, i: (0, 0)),
                  pl.BlockSpec(pos_tab.shape, lambda b, i: (0, 0, 0)),
                  pl.BlockSpec((1, t, IDX_HEADS * IDX_DIM), lambda b, i: (b, i, qi0)),
                  pl.BlockSpec((1, seq, LANES), lambda b, i: (b, 0, ki0)),
                  pl.BlockSpec((1, t, LANES), lambda b, i: (b, i, 0)),
                  pl.BlockSpec((1, t, qw), lambda b, i: (b, i, q0)),
                  pl.BlockSpec((1, seq, kvw), lambda b, i: (b, 0, k0 * LANES // kvw)),
                  pl.BlockSpec((1, seq, kvw), lambda b, i: (b, 0, v0 * LANES // kvw))],
        out_specs=pl.BlockSpec((1, t, qw), lambda b, i: (b, i, 0)),
        out_shape=jax.ShapeDtypeStruct((batch, seq, qw), BF16),
        scratch_shapes=[pltpu.VMEM((nt, t, t), jnp.int32), pltpu.VMEM((nt, t, t), F32),
                        pltpu.VMEM((nt, t, t), BF16),
                        pltpu.VMEM((DSA_HEADS // DSA_KV_HEADS, t, LANES), F32)],
        compiler_params=_params("parallel", "arbitrary"),
        name="dsa_attention",
    )(slope_tab, pos_tab, proj, proj, wi, proj, proj, proj)


def _sb_kernel(q_ref, k_ref, v_ref, o_ref, later_ref, acc_ref, *, tq, tk):
    assert tq == tk
    lo = _lane_lo()
    blk = min(tk, 2 * LANES)
    suffix = jnp.where(_tile_rel(blk, blk) >= 0, 1.0, 0.0).astype(BF16)
    half = tk // 2
    strict = _tile_rel(half, half) > 0
    hs = range(2)

    def on_diagonal(x, fill):
        top = jnp.where(strict, x[:half], fill)
        return top if x.shape[0] == half else jnp.concatenate([top, x[half:]], axis=0)

    for i in range(q_ref.shape[1] // tq):
        rows = slice(i * tq, (i + 1) * tq)
        qp = q_ref[0, rows, :]
        qs = [jnp.where(lo if hh == 0 else jnp.logical_not(lo), qp, jnp.zeros_like(qp)) for hh in hs]

        def tile(start, nk, r0, diag, qs=qs):
            nr = tq - r0
            k = k_ref[0, pl.ds(start, nk), :]
            v = v_ref[0, pl.ds(start, nk), :]
            zs = [_dot_t(qs[hh][r0:], k) for hh in hs]
            lgs = []
            for z in zs:
                nz = -z
                lgs.append(jnp.minimum(nz, 0.0) - jnp.log2(1.0 + jnp.exp2(jnp.minimum(z, nz))))
            if diag:
                lgs = [on_diagonal(lg, 0.0) for lg in lgs]
            lbs = [lg.astype(BF16) for lg in lgs]
            runs, tots = [], []
            for hh in hs:
                parts, after = [], later_ref[hh, r0:, :]
                for c0 in reversed(range(0, nk, blk)):
                    part = _dot(lbs[hh][:, c0:c0 + blk], suffix) + jnp.concatenate([after] * (blk // LANES), axis=1)
                    after = jnp.broadcast_to(part[:, 0:1], (nr, LANES))
                    parts.insert(0, part)
                runs.append(jnp.concatenate(parts, axis=1))
                tots.append(after)
            a_s = [jnp.exp2(zs[hh] + runs[hh]) for hh in hs]
            if diag:
                a_s = [on_diagonal(a, 0.0) for a in a_s]
            for hh in hs:
                later_ref[hh, r0:, :] = tots[hh]
            for hh in hs:
                acc_ref[hh, r0:, :] += _dot(a_s[hh].astype(BF16), v)

        later_ref[...] = jnp.zeros(later_ref.shape, F32)
        acc_ref[...] = jnp.zeros(acc_ref.shape, F32)
        tile(i * tq + half, half, half, True)
        tile(i * tq, half, 0, True)
        for j in reversed(range(i)):
            tile(j * tk, tk, 0, False)
        o_ref[0, rows, :] = jnp.where(lo, acc_ref[0], acc_ref[1]).astype(BF16)


def _sb_attention(qkv, batch, seq, tq=512, tk=512):
    tq, tk = min(tq, seq), min(tk, seq)
    w = SB_HEADS * SB_HEAD_DIM
    pairs = SB_HEADS // 2
    return pl.pallas_call(
        functools.partial(_sb_kernel, tq=tq, tk=tk),
        grid=(batch, pairs),
        in_specs=[pl.BlockSpec((1, seq, LANES), lambda b, m: (b, 0, m)),
                  pl.BlockSpec((1, seq, LANES), lambda b, m: (b, 0, pairs + m)),
                  pl.BlockSpec((1, seq, LANES), lambda b, m: (b, 0, 2 * pairs + m))],
        out_specs=pl.BlockSpec((1, seq, LANES), lambda b, m: (b, 0, m)),
        out_shape=jax.ShapeDtypeStruct((batch, seq, w), BF16),
        scratch_shapes=[pltpu.VMEM((2, tq, LANES), F32), pltpu.VMEM((2, tq, LANES), F32)],
        compiler_params=_params("parallel", "parallel"),
        name="sb_attention",
    )(qkv, qkv, qkv)


def _dup_heads(w, n_heads, dim):
    d = w.shape[0]
    w = w.reshape(d, n_heads, 1, dim)
    return jnp.broadcast_to(w, (d, n_heads, 2, dim)).reshape(d, n_heads * 2 * dim)


def _pad_heads(w, n_heads, dim, to):
    d = w.shape[0]
    w = w.reshape(d, n_heads, dim)
    return jnp.pad(w, ((0, 0), (0, 0), (0, to - dim))).reshape(d, n_heads * to)


def _swa_w_in(w):
    qw = SWA_HEADS * SWA_HEAD_DIM
    kw = SWA_KV_HEADS * SWA_HEAD_DIM
    q, k, v = w[:, :qw], w[:, qw:qw + kw], w[:, qw + kw:]
    q = q * (LOG2E * SWA_HEAD_DIM ** -0.5)
    return jnp.concatenate([q, _dup_heads(k, SWA_KV_HEADS, SWA_HEAD_DIM),
                            _dup_heads(v, SWA_KV_HEADS, SWA_HEAD_DIM)], axis=1).astype(BF16)


def _mla_w_in(w):
    d = w.shape[0]
    lat = MLA_Q_LORA + MLA_KV_LORA
    z = lambda c: jnp.zeros((d, c), w.dtype)
    return jnp.concatenate([w[:, :lat], z(MLA_NOPE), w[:, lat:], z(LANES - MLA_NOPE - MLA_ROPE)],
                           axis=1).astype(BF16)


def _sb_w_in(w):
    qw = SB_HEADS * SB_HEAD_DIM
    return jnp.concatenate([w[:, :qw] * (LOG2E * SB_HEAD_DIM ** -0.5), w[:, qw:]], axis=1).astype(BF16)


def _dsa_w_in(w):
    qw = DSA_HEADS * DSA_HEAD_DIM
    kw = DSA_KV_HEADS * DSA_HEAD_DIM
    iw = IDX_HEADS * IDX_DIM
    c = np.cumsum([qw, kw, kw, iw, IDX_DIM]).tolist()
    q, k, v, qi, ki, wi = (w[:, :c[0]], w[:, c[0]:c[1]], w[:, c[1]:c[2]], w[:, c[2]:c[3]],
                           w[:, c[3]:c[4]], w[:, c[4]:])
    q = q * (LOG2E * DSA_HEAD_DIM ** -0.5)
    qi = qi * (IDX_DIM ** -0.5)
    main = jnp.concatenate([q, _dup_heads(k, DSA_KV_HEADS, DSA_HEAD_DIM),
                            _dup_heads(v, DSA_KV_HEADS, DSA_HEAD_DIM), qi, _dup_heads(ki, 1, IDX_DIM)],
                           axis=1).astype(BF16)
    wi = jnp.pad(wi, ((0, 0), (0, LANES - IDX_HEADS))).astype(BF16)
    return main, wi


def kernel(x, ffn1_w_gu, ffn1_w_down, ffn2_w_gu, ffn2_w_down, ln_g, ln_b, a_w_in, a_sinks, a_w_o,
           b_w_in, b_q_norm, b_w_uq, b_kv_norm, b_w_ukv, b_w_o, c_w_in, c_w_o, d_w_in, d_w_o):
    batch, seq, d = x.shape
    n = batch * seq
    x = x.reshape(n, d)
    ffn1_gu, ffn1_down = ffn1_w_gu.astype(BF16), ffn1_w_down.astype(BF16)
    ffn2_gu, ffn2_down = ffn2_w_gu.astype(BF16), ffn2_w_down.astype(BF16)
    for i in range(DEPTH):
        mixer, j = i % N_MIXERS, i // N_MIXERS

        def ffn1(projs):
            return _ffn_proj(x, ffn1_gu, ffn1_down, i, ln_g[i, 0], ln_b[i, 0], projs)

        if mixer == 0:
            x, qkv = ffn1([(_swa_w_in(a_w_in[j]), BF16)])
            o = _swa_attention(qkv.reshape(batch, seq, -1), a_sinks[j] * LOG2E, batch, seq)
            w_o = a_w_o[j]
        elif mixer == 1:
            x, c = ffn1([(_mla_w_in(b_w_in[j]), F32)])
            hd = MLA_NOPE + MLA_ROPE
            wq = _pad_heads(b_w_uq[j], MLA_HEADS, hd, LANES).astype(BF16)
            wkv = b_w_ukv[j].reshape(MLA_KV_LORA, MLA_HEADS, MLA_NOPE + MLA_V)
            wk = _pad_heads(wkv[:, :, :MLA_NOPE].reshape(MLA_KV_LORA, -1), MLA_HEADS, MLA_NOPE,
                            LANES).astype(BF16)
            wv = wkv[:, :, MLA_NOPE:].reshape(MLA_KV_LORA, -1).astype(BF16)
            q, k, v = _mla_up(c, b_q_norm[j], b_kv_norm[j], wq, wk, wv, seq)
            o = _mla_attention(q.reshape(batch, seq, -1), k.reshape(batch, seq, -1),
                               v.reshape(batch, seq, -1))
            w_o = b_w_o[j]
        elif mixer == 2:
            w_main, w_wi = _dsa_w_in(c_w_in[j])
            x, proj, wi = ffn1([(w_main, BF16), (w_wi, F32)])
            o = _dsa_attention(proj.reshape(batch, seq, -1), wi.reshape(batch, seq, -1), batch, seq)
            w_o = c_w_o[j]
        else:
            x, qkv = ffn1([(_sb_w_in(d_w_in[j]), BF16)])
            o = _sb_attention(qkv.reshape(batch, seq, -1), batch, seq)
            w_o = d_w_o[j]
        x = _proj_ffn(o.reshape(n, -1), w_o.astype(BF16), x, ln_g[i, 1], ln_b[i, 1], ffn2_gu, ffn2_down, i,
                      ln_g[i, 2], ln_b[i, 2])
    return x.reshape(batch, seq, d)
```
